```python
import math
import jax
import jax.numpy as jnp
from jax import lax
import numpy as np

D_MODEL = 1024
BATCH = 32
SEQ = 2048
DEPTH = 2

GRID_W = 64
CTX_LEN = 256
EPS = 1e-6
N_MOD = 6

CONV_W = 512
CONV_K = 3

RG_W = 512
RG_BLOCKS = 8
RG_BLOCK_W = RG_W // RG_BLOCKS
RG_CONV_K = 4
RG_C = 8.0

DA_HEADS = 4
DA_HEAD_DIM = 64
DA_V_DIM = 2 * DA_HEAD_DIM
DA_QK_W = DA_HEADS * 2 * DA_HEAD_DIM
DA_W = DA_HEADS * DA_V_DIM
Q_BLOCK = 128
ROPE_AXIS_DIM = DA_HEAD_DIM // 2
ROPE_BASE = 10000.0

N_BRANCH = 3
SPLIT_SIZES = (CONV_W, CONV_W, CONV_W, RG_W, RG_W, DA_QK_W, DA_QK_W, DA_W, D_MODEL, D_MODEL, D_MODEL)
D_IN = 3 * CONV_W + 2 * RG_W + 2 * DA_QK_W + DA_W + N_BRANCH * D_MODEL

FF_DENSE = 3584
N_EXPERTS = 8
TOP_K = 2
FF_EXPERT = 3584
MOE_BLOCK = 512
N_DENSE = (DEPTH + 1) // 2
N_MOE = DEPTH // 2

kernel_name = 'hybrid_flow_block'


def rmsnorm(x, w):
    xf = x.astype(jnp.float32)
    y = xf * lax.rsqrt(jnp.mean(xf * xf, axis=-1, keepdims=True) + EPS)
    return (y * w.astype(jnp.float32)).astype(x.dtype)


def modulate(h, shift, scale):
    return h * (1.0 + scale) + shift


def split_columns(z):
    out, start = [], 0
    for size in SPLIT_SIZES:
        out.append(z[..., start:start + size])
        start += size
    return out


def dwconv(x, w, pad_left, pad_right):
    n = x.shape[1]
    xp = jnp.pad(x, ((0, 0), (pad_left, pad_right), (0, 0)))
    return sum(xp[:, k:k + n] * w[k] for k in range(w.shape[0]))


def short_conv_mixer(gate_b, gate_c, xin, w):
    return gate_b * dwconv(gate_c * xin, w, CONV_K // 2, CONV_K // 2)


def _linrec_combine(e1, e2):
    a1, b1 = e1
    a2, b2 = e2
    return a1 * a2, a2 * b1 + b2


def rglru_direction(xr, h0, conv_w, conv_b, wa, ba, wx, bx, lam, reverse):
    pad = (0, RG_CONV_K - 1) if reverse else (RG_CONV_K - 1, 0)
    xc = dwconv(xr, conv_w, pad[0], pad[1]) + conv_b
    bsz, n, _ = xc.shape
    xg = xc.reshape(bsz, n, RG_BLOCKS, RG_BLOCK_W)
    r = jax.nn.sigmoid(jnp.einsum('bngi,gij->bngj', xg, wa) + ba).reshape(bsz, n, RG_W)
    i = jax.nn.sigmoid(jnp.einsum('bngi,gij->bngj', xg, wx) + bx).reshape(bsz, n, RG_W)
    log_a = (-RG_C * r.astype(jnp.float32)) * jax.nn.softplus(-lam.astype(jnp.float32))
    a = jnp.exp(log_a)
    b = jnp.sqrt(-jnp.expm1(2.0 * log_a)) * (i * xc).astype(jnp.float32)
    if h0 is not None:
        t0 = n - 1 if reverse else 0
        b = b.at[:, t0].add(a[:, t0] * h0)
    _, h = lax.associative_scan(_linrec_combine, (a, b), reverse=reverse, axis=1)
    return h


def axial_rope_tables(n, dtype):
    rows = n // GRID_W
    pos_r = jnp.repeat(jnp.arange(rows, dtype=jnp.float32), GRID_W)
    pos_c = jnp.broadcast_to(jnp.arange(GRID_W, dtype=jnp.float32), (rows, GRID_W)).reshape(-1)
    inv_freq = ROPE_BASE ** (-jnp.arange(0, ROPE_AXIS_DIM, 2, dtype=jnp.float32) / ROPE_AXIS_DIM)
    ang = jnp.stack([pos_r[:, None] * inv_freq, pos_c[:, None] * inv_freq], axis=1)
    return jnp.cos(ang).astype(dtype), jnp.sin(ang).astype(dtype)


def apply_axial_rope(x, cos, sin):
    shp = x.shape
    xa = x.reshape(shp[:-1] + (2, 2, ROPE_AXIS_DIM // 2))
    x1, x2 = xa[..., 0, :], xa[..., 1, :]
    cb = cos[None, :, None, None]
    sb = sin[None, :, None, None]
    out = jnp.stack([x1 * cb - x2 * sb, x2 * cb + x1 * sb], axis=-2)
    return out.reshape(shp)


def diff_lambda(da_lambda, lam_init):
    lf = da_lambda.astype(jnp.float32)
    return jnp.exp(jnp.sum(lf[0] * lf[1])) - jnp.exp(jnp.sum(lf[2] * lf[3])) + lam_init


def diff_softmax_mix(q1, q2, k1, k2, v, lam):
    scale = DA_HEAD_DIM ** -0.5
    p1 = jax.nn.softmax(jnp.einsum('bqhd,bkhd->bhqk', q1, k1).astype(jnp.float32) * scale, axis=-1)
    p2 = jax.nn.softmax(jnp.einsum('bqhd,bkhd->bhqk', q2, k2).astype(jnp.float32) * scale, axis=-1)
    p = (p1 - lam * p2).astype(v.dtype)
    return jnp.einsum('bhqk,bkhe->bqhe', p, v)


def latent_diff_attention(q, k, v, ck, cv, lam):
    bsz, n = q.shape[:2]
    kk = jnp.concatenate([k, ck], axis=1)
    vv = jnp.concatenate([v, cv], axis=1)
    k1, k2 = kk[..., 0, :], kk[..., 1, :]
    nb = n // Q_BLOCK
    qb = q.reshape(bsz, nb, Q_BLOCK, DA_HEADS, 2, DA_HEAD_DIM).swapaxes(0, 1)
    o = lax.map(lambda qblk: diff_softmax_mix(qblk[..., 0, :], qblk[..., 1, :], k1, k2, vv, lam), qb)
    return o.swapaxes(0, 1).reshape(bsz, n, DA_HEADS, DA_V_DIM)


def diff_attn_out(o, subln_w, lam_init):
    o = rmsnorm(o, subln_w) * (1.0 - lam_init)
    return o.reshape(o.shape[:2] + (DA_W,))


def gated_merge(y_a, y_r, y_d, g_a, g_r, g_d, w_branch, w_out):
    m = (jax.nn.sigmoid(g_a) * (y_a @ w_branch[0])
         + jax.nn.sigmoid(g_r) * (y_r @ w_branch[1])
         + jax.nn.sigmoid(g_d) * (y_d @ w_branch[2]))
    return m @ w_out


def hybrid_mixer(u, uc, cos, sin, w_in, conv_a_w, rg_conv_w, rg_conv_b, rg_wa, rg_ba, rg_wx, rg_bx,
                 rg_lambda, da_lambda, da_subln_w, w_branch, w_out, lam_init, need_ctx):
    bsz, n, _ = u.shape
    n_ctx = uc.shape[1]
    a_b, a_c, a_x, r_g, r_x, q, k, v, g_a, g_r, g_d = split_columns(u @ w_in)
    ca_b, ca_c, ca_x, cr_g, cr_x, cq, ck, cv, cg_a, cg_r, cg_d = split_columns(uc @ w_in)

    fwd = (rg_conv_w[0], rg_conv_b[0], rg_wa[0], rg_ba[0], rg_wx[0], rg_bx[0], rg_lambda[0])
    bwd = (rg_conv_w[1], rg_conv_b[1], rg_wa[1], rg_ba[1], rg_wx[1], rg_bx[1], rg_lambda[1])
    hf_ctx = rglru_direction(cr_x, None, *fwd, reverse=False)
    hb_ctx = rglru_direction(cr_x, None, *bwd, reverse=True)
    hf = rglru_direction(r_x, hf_ctx[:, -1], *fwd, reverse=False)
    hb = rglru_direction(r_x, hb_ctx[:, 0], *bwd, reverse=True)
    y_r = jax.nn.gelu(r_g) * (hf + hb).astype(r_g.dtype)

    lam = diff_lambda(da_lambda, lam_init)
    q = apply_axial_rope(q.reshape(bsz, n, DA_HEADS, 2, DA_HEAD_DIM), cos, sin)
    k = apply_axial_rope(k.reshape(bsz, n, DA_HEADS, 2, DA_HEAD_DIM), cos, sin)
    v = v.reshape(bsz, n, DA_HEADS, DA_V_DIM)
    ck = ck.reshape(bsz, n_ctx, DA_HEADS, 2, DA_HEAD_DIM)
    cv = cv.reshape(bsz, n_ctx, DA_HEADS, DA_V_DIM)
    y_d = diff_attn_out(latent_diff_attention(q, k, v, ck, cv, lam), da_subln_w, lam_init)

    y_a = short_conv_mixer(a_b, a_c, a_x, conv_a_w)
    y = gated_merge(y_a, y_r, y_d, g_a, g_r, g_d, w_branch, w_out)
    if not need_ctx:
        return y, None

    yc_a = short_conv_mixer(ca_b, ca_c, ca_x, conv_a_w)
    yc_r = jax.nn.gelu(cr_g) * (hf_ctx + hb_ctx).astype(cr_g.dtype)
    cq = cq.reshape(bsz, n_ctx, DA_HEADS, 2, DA_HEAD_DIM)
    oc = diff_softmax_mix(cq[..., 0, :], cq[..., 1, :], ck[..., 0, :], ck[..., 1, :], cv, lam)
    yc_d = diff_attn_out(oc, da_subln_w, lam_init)
    yc = gated_merge(yc_a, yc_r, yc_d, cg_a, cg_r, cg_d, w_branch, w_out)
    return y, yc


def swiglu(h, w13, w2):
    g, u = jnp.split(h @ w13, 2, axis=-1)
    return (jax.nn.silu(g) * u) @ w2


def moe_swiglu(h, router_w, w13, w2):
    shp = h.shape
    t = h.reshape(-1, D_MODEL)
    n_tok = t.shape[0]
    logits = (t @ router_w).astype(jnp.float32)
    top_v, top_e = lax.top_k(logits, TOP_K)
    gates = jax.nn.softmax(top_v, axis=-1)
    flat_e = top_e.reshape(-1)
    flat_tok = jnp.repeat(jnp.arange(n_tok, dtype=jnp.int32), TOP_K)
    flat_g = gates.reshape(-1)
    order = jnp.argsort(flat_e)
    se, stok, sg = flat_e[order], flat_tok[order], flat_g[order]
    counts = jnp.bincount(flat_e, length=N_EXPERTS)
    padded = (counts + MOE_BLOCK - 1) // MOE_BLOCK * MOE_BLOCK
    pad_end = jnp.cumsum(padded)
    pad_start = pad_end - padded
    start = jnp.cumsum(counts) - counts
    n_slots = n_tok * TOP_K
    dest = pad_start[se] + jnp.arange(n_slots, dtype=pad_start.dtype) - start[se]
    n_blocks = -(-n_slots // MOE_BLOCK) + N_EXPERTS
    buf_tok = jnp.full((n_blocks * MOE_BLOCK,), n_tok, dtype=jnp.int32).at[dest].set(stok)
    block_e = jnp.minimum(jnp.searchsorted(pad_end, jnp.arange(n_blocks) * MOE_BLOCK, side='right'),
                          N_EXPERTS - 1)
    t_pad = jnp.concatenate([t, jnp.zeros((1, D_MODEL), t.dtype)], axis=0)
    xb = t_pad[buf_tok].reshape(n_blocks, MOE_BLOCK, D_MODEL)
    yb = lax.map(lambda a: swiglu(a[0], w13[a[1]], w2[a[1]]), (xb, block_e))
    ys = yb.reshape(-1, D_MODEL)[dest]
    out = jnp.zeros_like(t).at[stok].add(sg[:, None].astype(t.dtype) * ys)
    return out.reshape(shp)


def setup_inputs(seed: int = 0) -> dict:
    key = jax.random.key(seed)
    keys = jax.random.split(key, 32)
    f32 = jnp.float32

    def normal(i, shape, scale=1.0):
        return jax.random.normal(keys[i], shape, f32) * scale

    u = jax.random.uniform(keys[20], (DEPTH, 2, RG_W), f32, 0.9, 0.999)
    a0 = u ** (1.0 / RG_C)
    return {
        'x': normal(0, (BATCH, SEQ, D_MODEL)),
        'c': normal(1, (BATCH, D_MODEL)),
        'ctx': normal(2, (BATCH, CTX_LEN, D_MODEL)),
        'c_ctx': normal(3, (D_MODEL,)),
        'mod_w': normal(4, (DEPTH, D_MODEL, N_MOD * D_MODEL), D_MODEL ** -0.5),
        'mod_b': normal(5, (DEPTH, N_MOD * D_MODEL), 0.02),
        'norm1_w': 1.0 + normal(6, (DEPTH, D_MODEL), 0.02),
        'norm2_w': 1.0 + normal(7, (DEPTH, D_MODEL), 0.02),
        'w_in': normal(8, (DEPTH, D_MODEL, D_IN), D_MODEL ** -0.5),
        'conv_a_w': normal(9, (DEPTH, CONV_K, CONV_W), CONV_K ** -0.5),
        'rg_conv_w': normal(10, (DEPTH, 2, RG_CONV_K, RG_W), RG_CONV_K ** -0.5),
        'rg_conv_b': normal(11, (DEPTH, 2, RG_W), 0.02),
        'rg_wa': normal(12, (DEPTH, 2, RG_BLOCKS, RG_BLOCK_W, RG_BLOCK_W), RG_BLOCK_W ** -0.5),
        'rg_ba': normal(13, (DEPTH, 2, RG_BLOCKS, RG_BLOCK_W), 0.02),
        'rg_wx': normal(14, (DEPTH, 2, RG_BLOCKS, RG_BLOCK_W, RG_BLOCK_W), RG_BLOCK_W ** -0.5),
        'rg_bx': normal(15, (DEPTH, 2, RG_BLOCKS, RG_BLOCK_W), 0.02),
        'rg_lambda': jnp.log(a0) - jnp.log1p(-a0),
        'da_lambda': normal(16, (DEPTH, 4, DA_HEAD_DIM), 0.1),
        'da_subln_w': 1.0 + normal(17, (DEPTH, DA_V_DIM), 0.02),
        'w_branch': normal(18, (DEPTH, N_BRANCH, CONV_W, D_MODEL), CONV_W ** -0.5),
        'w_out': normal(19, (DEPTH, D_MODEL, D_MODEL), D_MODEL ** -0.5),
        'ffn_w13': normal(21, (N_DENSE, D_MODEL, 2 * FF_DENSE), D_MODEL ** -0.5),
        'ffn_w2': normal(22, (N_DENSE, FF_DENSE, D_MODEL), FF_DENSE ** -0.5),
        'router_w': normal(23, (N_MOE, D_MODEL, N_EXPERTS), D_MODEL ** -0.5),
        'moe_w13': normal(24, (N_MOE, N_EXPERTS, D_MODEL, 2 * FF_EXPERT), D_MODEL ** -0.5),
        'moe_w2': normal(25, (N_MOE, N_EXPERTS, FF_EXPERT, D_MODEL), FF_EXPERT ** -0.5),
        'final_norm_w': 1.0 + normal(26, (D_MODEL,), 0.02),
    }


def reference(x, c, ctx, c_ctx, mod_w, mod_b, norm1_w, norm2_w, w_in, conv_a_w, rg_conv_w, rg_conv_b,
              rg_wa, rg_ba, rg_wx, rg_bx, rg_lambda, da_lambda, da_subln_w, w_branch, w_out,
              ffn_w13, ffn_w2, router_w, moe_w13, moe_w2, final_norm_w):
    n = x.shape[1]
    cos, sin = axial_rope_tables(n, x.dtype)
    cx = ctx
    for l in range(DEPTH):
        last = l == DEPTH - 1
        lam_init = 0.8 - 0.6 * math.exp(-0.3 * l)
        mod = jax.nn.silu(c) @ mod_w[l] + mod_b[l]
        mod_c = jax.nn.silu(c_ctx) @ mod_w[l] + mod_b[l]
        sh1, sc1, g1, sh2, sc2, g2 = jnp.split(mod[:, None, :], N_MOD, axis=-1)
        csh1, csc1, cg1, csh2, csc2, cg2 = jnp.split(mod_c, N_MOD)

        u = modulate(rmsnorm(x, norm1_w[l]), sh1, sc1)
        uc = modulate(rmsnorm(cx, norm1_w[l]), csh1, csc1)
        y, yc = hybrid_mixer(u, uc, cos, sin, w_in[l], conv_a_w[l], rg_conv_w[l], rg_conv_b[l],
                             rg_wa[l], rg_ba[l], rg_wx[l], rg_bx[l], rg_lambda[l], da_lambda[l],
                             da_subln_w[l], w_branch[l], w_out[l], lam_init, not last)
        x = x + g1 * y
        if not last:
            cx = cx + cg1 * yc

        j = l // 2
        h = modulate(rmsnorm(x, norm2_w[l]), sh2, sc2)
        if l % 2 == 0:
            x = x + g2 * swiglu(h, ffn_w13[j], ffn_w2[j])
        else:
            x = x + g2 * moe_swiglu(h, router_w[j], moe_w13[j], moe_w2[j])
        if not last:
            hc = modulate(rmsnorm(cx, norm2_w[l]), csh2, csc2)
            if l % 2 == 0:
                cx = cx + cg2 * swiglu(hc, ffn_w13[j], ffn_w2[j])
            else:
                cx = cx + cg2 * moe_swiglu(hc, router_w[j], moe_w13[j], moe_w2[j])
    return rmsnorm(x, final_norm_w)
```

```python
import functools
import math

import jax
import jax.numpy as jnp
import numpy as np
from jax import lax
from jax.experimental import pallas as pl
from jax.experimental.pallas import tpu as pltpu

F32 = jnp.float32
BF16 = jnp.bfloat16

EPS = 1e-6
N_MOD = 6
GRID_W = 64
CONV_W = 512
RG_W = 512
RG_BLOCKS = 8
RG_C = 8.0
DA_HEADS = 4
DA_HEAD_DIM = 64
DA_V_DIM = 128
ROPE_AXIS_DIM = 32
ROPE_BASE = 10000.0
N_EXPERTS = 8
LANES = 128
HALF = 256
COL = 512
VMEM_LIMIT = 56 * 1024 * 1024

Z_FULL = dict(a_b=0, a_c=4, a_x=8, r_g=12, r_x=16, q=20, k=24, v=28, g_a=32, g_r=40, g_d=48)
Z_CTX_LAST = dict(r_x=0, k=4, v=8)


def _cparams(sem, vmem=VMEM_LIMIT):
    return pltpu.CompilerParams(dimension_semantics=sem, vmem_limit_bytes=vmem)


def _sigmoid(x):
    return 1.0 / (1.0 + jnp.exp(-x))


def _silu(x):
    return x * _sigmoid(x)


def _gelu_tanh(x):
    return 0.5 * x * (1.0 + jnp.tanh(math.sqrt(2.0 / math.pi) * (x + 0.044715 * (x * x * x))))


def _rms(x):
    return x * lax.rsqrt(jnp.mean(x * x, axis=-1, keepdims=True) + EPS)


def _mod_kernel(cc_ref, w_ref, b_ref, dl_ref, mod_ref, lam_ref, *, lam_init):
    cc = cc_ref[...]
    s = _silu(cc)
    mod_ref[...] = jnp.dot(s, w_ref[...], preferred_element_type=F32,
                           precision=lax.Precision.HIGHEST) + b_ref[...]
    dl = dl_ref[...]
    s1 = jnp.sum(dl[0:1] * dl[1:2], axis=-1, keepdims=True)
    s2 = jnp.sum(dl[2:3] * dl[3:4], axis=-1, keepdims=True)
    lam = jnp.exp(s1) - jnp.exp(s2) + lam_init
    lam_ref[...] = jnp.broadcast_to(lam, lam_ref.shape)


def _mod_call(cc, w, b, dl, lam_init):
    bp, d = cc.shape
    nd = w.shape[1]
    tn = 1024
    return pl.pallas_call(
        functools.partial(_mod_kernel, lam_init=lam_init),
        grid=(nd // tn,),
        in_specs=[
            pl.BlockSpec((bp, d), lambda j: (0, 0)),
            pl.BlockSpec((d, tn), lambda j: (0, j)),
            pl.BlockSpec((1, tn), lambda j: (0, j)),
            pl.BlockSpec(dl.shape, lambda j: (0, 0)),
        ],
        out_specs=[
            pl.BlockSpec((bp, tn), lambda j: (0, j)),
            pl.BlockSpec((8, LANES), lambda j: (0, 0)),
        ],
        out_shape=[jax.ShapeDtypeStruct((bp, nd), F32), jax.ShapeDtypeStruct((8, LANES), F32)],
        compiler_params=_cparams(("arbitrary",)),
        name="mod",
    )(cc, w, b, dl)


def _in_kernel(*refs, rope_blocks):
    if rope_blocks:
        x_ref, nw_ref, sh_ref, sc_ref, cos_ref, sin_ref, w_ref, o_ref, u_scr = refs
    else:
        x_ref, nw_ref, sh_ref, sc_ref, w_ref, o_ref, u_scr = refs
    j = pl.program_id(2)

    @pl.when(j == 0)
    def _():
        y = _rms(x_ref[0]) * nw_ref[...]
        u_scr[...] = (y * (1.0 + sc_ref[0]) + sh_ref[0]).astype(BF16)

    acc = jnp.dot(u_scr[...], w_ref[...], preferred_element_type=F32)
    if not rope_blocks:
        o_ref[0] = acc.astype(o_ref.dtype)
        return
    is_rope = (j == rope_blocks[0]) | (j == rope_blocks[1])

    @pl.when(is_rope)
    def _():
        c = cos_ref[...]
        s = sin_ref[...]
        for g in range(acc.shape[1] // LANES):
            a = acc[:, g * LANES:(g + 1) * LANES]
            r = a * c + pltpu.roll(a, LANES // 2, 1) * s
            o_ref[0, :, g * LANES:(g + 1) * LANES] = r.astype(o_ref.dtype)

    @pl.when(jnp.logical_not(is_rope))
    def _():
        o_ref[0] = acc.astype(o_ref.dtype)


def _in_call(x, nw, mod3, mod_row, w, col_blocks, rope_blocks, cos_t=None, sin_t=None):
    bsz, n, d = x.shape
    tm = min(1024, n)
    nj = len(col_blocks)
    cb = tuple(col_blocks)
    if cb == tuple(range(nj)):
        wmap = lambda b, i, j: (0, j)
    else:
        assert cb == (4, 6, 7)
        wmap = lambda b, i, j: (0, 4 + j + jnp.minimum(j, 1))
    in_specs = [
        pl.BlockSpec((1, tm, d), lambda b, i, j: (b, i, 0)),
        pl.BlockSpec((1, d), lambda b, i, j: (0, 0)),
        pl.BlockSpec((1, 1, d), lambda b, i, j: (mod_row(b), 0, 0)),
        pl.BlockSpec((1, 1, d), lambda b, i, j: (mod_row(b), 0, 1)),
    ]
    args = [x, nw, mod3, mod3]
    if rope_blocks:
        in_specs += [pl.BlockSpec((tm, LANES), lambda b, i, j: (i, 0)),
                     pl.BlockSpec((tm, LANES), lambda b, i, j: (i, 0))]
        args += [cos_t, sin_t]
    in_specs.append(pl.BlockSpec((d, COL), wmap))
    args.append(w)
    return pl.pallas_call(
        functools.partial(_in_kernel, rope_blocks=rope_blocks),
        grid=(bsz, n // tm, nj),
        in_specs=in_specs,
        out_specs=pl.BlockSpec((1, tm, COL), lambda b, i, j: (b, i, j)),
        out_shape=jax.ShapeDtypeStruct((bsz, n, nj * COL), BF16),
        scratch_shapes=[pltpu.VMEM((tm, d), BF16)],
        compiler_params=_cparams(("parallel", "parallel", "arbitrary")),
        name="in_proj",
    )(*args)


SEQ_TILE = 64
PAD = 8


def _seq_kernel(*refs, n, nc, need_ctx_out):
    it = iter(refs)
    ab_ref, ac_ref, ax_ref, rg_ref, rx_ref = (next(it) for _ in range(5))
    if need_ctx_out:
        cab_ref, cac_ref, cax_ref, crg_ref = (next(it) for _ in range(4))
    crx_ref = next(it)
    wa_ref, cw_ref, cbias_ref, wg_ref, ba_ref, bx_ref, lam_ref = (next(it) for _ in range(7))
    ya_ref, yr_ref = next(it), next(it)
    if need_ctx_out:
        yca_ref, ycr_ref = next(it), next(it)
    xp, a_scr, b_scr = next(it), next(it), next(it)
    T = SEQ_TILE
    W = HALF

    def stage(val, ns):
        xp[0:PAD, :] = jnp.zeros((PAD, W), F32)
        xp[PAD:PAD + ns, :] = val
        xp[PAD + ns:2 * PAD + ns, :] = jnp.zeros((PAD, W), F32)

    def conv_mixer(b_ref, c_ref, x_ref, out_ref, ns):
        stage(c_ref[0].astype(F32) * x_ref[0].astype(F32), ns)
        w0, w1, w2 = wa_ref[0:1, :], wa_ref[1:2, :], wa_ref[2:3, :]

        def body(ti, carry):
            t0 = pl.multiple_of(ti * T, T)
            ps = xp[pl.ds(t0, T + 2 * PAD), :]
            conv = w0 * ps[PAD - 1:PAD - 1 + T] + w1 * ps[PAD:PAD + T] + w2 * ps[PAD + 1:PAD + 1 + T]
            gate = b_ref[0, pl.ds(t0, T), :].astype(F32)
            out_ref[0, pl.ds(t0, T), :] = (gate * conv).astype(out_ref.dtype)
            return carry

        lax.fori_loop(0, ns // T, body, 0)

    nl = -lam_ref[...]
    sp = jnp.maximum(nl, 0.0) + jnp.log1p(jnp.exp(-jnp.abs(nl)))
    sub = lax.broadcasted_iota(jnp.int32, (8, W), 0)

    def rglru(x_ref, ns, hf0, hb0):
        stage(x_ref[0].astype(F32), ns)

        def gates_body(ti, carry):
            t0 = pl.multiple_of(ti * T, T)
            xs = xp[pl.ds(t0, T + 2 * PAD), :]
            for d in range(2):
                off = PAD - 3 if d == 0 else PAD
                xc = cbias_ref[d]
                for k in range(4):
                    xc = xc + cw_ref[d, k:k + 1, :] * xs[off + k:off + k + T]
                g = jnp.dot(xc.astype(BF16), wg_ref[d, 0], preferred_element_type=F32)
                r = _sigmoid(g[:, :W] + ba_ref[d])
                i = _sigmoid(g[:, W:] + bx_ref[d])
                log_a = (-RG_C * r) * sp[d]
                a = jnp.exp(log_a)
                a_scr[d, pl.ds(t0, T), :] = a
                b_scr[d, pl.ds(t0, T), :] = jnp.sqrt(1.0 - a * a) * (i * xc)
            return carry

        lax.fori_loop(0, ns // T, gates_body, 0)
        nch = ns // 8

        def scan_body(ci, carry):
            hf, hb = carry
            r0 = pl.multiple_of(ci * 8, 8)
            a = a_scr[0, pl.ds(r0, 8), :]
            b = b_scr[0, pl.ds(r0, 8), :]
            for s in (1, 2, 4):
                m = sub >= s
                b = b + a * jnp.where(m, pltpu.roll(b, s, 0), 0.0)
                a = a * jnp.where(m, pltpu.roll(a, s, 0), 1.0)
            h = b + a * hf
            b_scr[0, pl.ds(r0, 8), :] = h
            hf = jnp.broadcast_to(h[7:8, :], (8, W))
            r1 = pl.multiple_of((nch - 1 - ci) * 8, 8)
            a = a_scr[1, pl.ds(r1, 8), :]
            b = b_scr[1, pl.ds(r1, 8), :]
            for s in (1, 2, 4):
                m = sub < 8 - s
                b = b + a * jnp.where(m, pltpu.roll(b, 8 - s, 0), 0.0)
                a = a * jnp.where(m, pltpu.roll(a, 8 - s, 0), 1.0)
            h = b + a * hb
            b_scr[1, pl.ds(r1, 8), :] = h
            hb = jnp.broadcast_to(h[0:1, :], (8, W))
            return hf, hb

        return lax.fori_loop(0, nch, scan_body, (hf0, hb0))

    def rg_out(g_ref, out_ref, ns):
        def body(ti, carry):
            t0 = pl.multiple_of(ti * T, T)
            hs = b_scr[0, pl.ds(t0, T), :] + b_scr[1, pl.ds(t0, T), :]
            g = g_ref[0, pl.ds(t0, T), :].astype(F32)
            out_ref[0, pl.ds(t0, T), :] = (_gelu_tanh(g) * hs).astype(out_ref.dtype)
            return carry

        lax.fori_loop(0, ns // T, body, 0)

    zero = jnp.zeros((8, W), F32)
    hf, hb = rglru(crx_ref, nc, zero, zero)
    if need_ctx_out:
        rg_out(crg_ref, ycr_ref, nc)
        conv_mixer(cab_ref, cac_ref, cax_ref, yca_ref, nc)
    rglru(rx_ref, n, hf, hb)
    rg_out(rg_ref, yr_ref, n)
    conv_mixer(ab_ref, ac_ref, ax_ref, ya_ref, n)


def _seq_call(z, zc, zc_off, p, need_ctx_out):
    bsz, n, _ = z.shape
    nc = zc.shape[1]
    u = HALF // LANES

    def zspec(rows, off):
        return pl.BlockSpec((1, rows, HALF), lambda b, c: (b, 0, off // u + c))

    names = ("a_b", "a_c", "a_x", "r_g", "r_x")
    in_specs = [zspec(n, Z_FULL[k]) for k in names]
    args = [z] * 5
    if need_ctx_out:
        in_specs += [zspec(nc, zc_off[k]) for k in names[:4]]
        args += [zc] * 4
    in_specs.append(zspec(nc, zc_off["r_x"]))
    args.append(zc)
    in_specs += [
        pl.BlockSpec((3, HALF), lambda b, c: (0, c)),
        pl.BlockSpec((2, 4, HALF), lambda b, c: (0, 0, c)),
        pl.BlockSpec((2, 1, HALF), lambda b, c: (0, 0, c)),
        pl.BlockSpec((2, 1, HALF, 2 * HALF), lambda b, c: (0, c, 0, 0)),
        pl.BlockSpec((2, 1, HALF), lambda b, c: (0, 0, c)),
        pl.BlockSpec((2, 1, HALF), lambda b, c: (0, 0, c)),
        pl.BlockSpec((2, 1, HALF), lambda b, c: (0, 0, c)),
    ]
    args += [p["conv_a_w"], p["rg_conv_w"], p["rg_conv_b"], p["rg_wg"], p["rg_ba"], p["rg_bx"], p["rg_lambda"]]
    out_specs = [pl.BlockSpec((1, n, HALF), lambda b, c: (b, 0, c))] * 2
    out_shape = [jax.ShapeDtypeStruct((bsz, n, CONV_W), BF16), jax.ShapeDtypeStruct((bsz, n, RG_W), BF16)]
    if need_ctx_out:
        out_specs += [pl.BlockSpec((1, nc, HALF), lambda b, c: (b, 0, c))] * 2
        out_shape += [jax.ShapeDtypeStruct((bsz, nc, CONV_W), BF16), jax.ShapeDtypeStruct((bsz, nc, RG_W), BF16)]
    npad = max(n, nc) + 2 * PAD
    return pl.pallas_call(
        functools.partial(_seq_kernel, n=n, nc=nc, need_ctx_out=need_ctx_out),
        grid=(bsz, CONV_W // HALF),
        in_specs=in_specs,
        out_specs=out_specs,
        out_shape=out_shape,
        scratch_shapes=[pltpu.VMEM((npad, HALF), F32), pltpu.VMEM((2, max(n, nc), HALF), F32),
                        pltpu.VMEM((2, max(n, nc), HALF), F32)],
        compiler_params=_cparams(("parallel", "parallel")),
        name="seq_mixers",
    )(*args)


def _attn_kernel(*refs, has_lat, lam_init):
    if has_lat:
        q_ref, k_ref, v_ref, ck_ref, cv_ref, lam_ref, sw_ref, o_ref = refs
    else:
        q_ref, ck_ref, cv_ref, lam_ref, sw_ref, o_ref = refs
    q = q_ref[0]
    lane = lax.broadcasted_iota(jnp.int32, (1, LANES), 1)
    map1 = (lane % DA_HEAD_DIM) < (DA_HEAD_DIM // 2)
    scale = DA_HEAD_DIM ** -0.5
    qs = q * jnp.asarray(scale, q.dtype)
    zero = jnp.zeros_like(qs)
    dn = (((1,), (1,)), ((), ()))
    lam = lam_ref[0:1, 0:1]
    segs = [(ck_ref[0], cv_ref[0])]
    if has_lat:
        segs.append((k_ref[0], v_ref[0]))

    def soft(qm):
        ss = [lax.dot_general(qm, k, dn, preferred_element_type=F32) for k, _ in segs]
        mx = functools.reduce(jnp.maximum, [jnp.max(s, axis=-1, keepdims=True) for s in ss])
        es = [jnp.exp(s - mx) for s in ss]
        den = functools.reduce(lambda a, b: a + b, [jnp.sum(e, axis=-1, keepdims=True) for e in es])
        return es, den

    e1, l1 = soft(jnp.where(map1, qs, zero))
    e2, l2 = soft(jnp.where(map1, zero, qs))
    w1 = 1.0 / l1
    w2 = lam / l2
    o = None
    for (ea, eb, (_, v)) in zip(e1, e2, segs):
        p = (ea * w1 - eb * w2).astype(BF16)
        t = jnp.dot(p, v, preferred_element_type=F32)
        o = t if o is None else o + t
    y = _rms(o) * sw_ref[...] * (1.0 - lam_init)
    o_ref[0] = y.astype(o_ref.dtype)


def _attn_call(zq, q_off, z, zc, zc_off, lam, subw, lam_init, has_lat):
    bsz, nq, _ = zq.shape
    nc = zc.shape[1]
    tq = min(256, nq)
    in_specs = [pl.BlockSpec((1, tq, LANES), lambda b, h, i: (b, i, q_off + h))]
    args = [zq]
    if has_lat:
        n = z.shape[1]
        in_specs += [pl.BlockSpec((1, n, LANES), lambda b, h, i: (b, 0, Z_FULL["k"] + h)),
                     pl.BlockSpec((1, n, LANES), lambda b, h, i: (b, 0, Z_FULL["v"] + h))]
        args += [z, z]
    in_specs += [pl.BlockSpec((1, nc, LANES), lambda b, h, i: (b, 0, zc_off["k"] + h)),
                 pl.BlockSpec((1, nc, LANES), lambda b, h, i: (b, 0, zc_off["v"] + h)),
                 pl.BlockSpec((8, LANES), lambda b, h, i: (0, 0)),
                 pl.BlockSpec((1, LANES), lambda b, h, i: (0, 0))]
    args += [zc, zc, lam, subw]
    return pl.pallas_call(
        functools.partial(_attn_kernel, has_lat=has_lat, lam_init=lam_init),
        grid=(bsz, DA_HEADS, nq // tq),
        in_specs=in_specs,
        out_specs=pl.BlockSpec((1, tq, LANES), lambda b, h, i: (b, i, h)),
        out_shape=jax.ShapeDtypeStruct((bsz, nq, DA_HEADS * DA_V_DIM), BF16),
        compiler_params=_cparams(("parallel", "parallel", "arbitrary")),
        name="diff_attn",
    )(*args)


def _merge_kernel(*refs, route):
    if route:
        (ya_ref, yr_ref, yd_ref, ga_ref, gr_ref, gd_ref, x_ref, g1_ref, nw_ref, sh_ref, sc_ref,
         wb_ref, wo_ref, rw_ref, xo_ref, h_ref, rt_ref) = refs
    else:
        (ya_ref, yr_ref, yd_ref, ga_ref, gr_ref, gd_ref, x_ref, g1_ref, nw_ref, sh_ref, sc_ref,
         wb_ref, wo_ref, xo_ref, h_ref) = refs
    m = None
    for i, (y_ref, g_ref) in enumerate(((ya_ref, ga_ref), (yr_ref, gr_ref), (yd_ref, gd_ref))):
        t = _sigmoid(g_ref[0].astype(F32)) * jnp.dot(y_ref[0], wb_ref[i], preferred_element_type=F32)
        m = t if m is None else m + t
    out = jnp.dot(m.astype(BF16), wo_ref[...], preferred_element_type=F32)
    xn = x_ref[0] + g1_ref[0] * out
    xo_ref[0] = xn
    h = (_rms(xn) * nw_ref[...]) * (1.0 + sc_ref[0]) + sh_ref[0]
    h_ref[0] = h.astype(h_ref.dtype)
    if route:
        logits = jnp.dot(h, rw_ref[...], preferred_element_type=F32, precision=lax.Precision.HIGHEST)
        lane = lax.broadcasted_iota(jnp.int32, logits.shape, 1)
        neg = jnp.float32(-jnp.inf)
        l1 = jnp.where(lane < N_EXPERTS, logits, neg)
        m1 = jnp.max(l1, axis=-1, keepdims=True)
        i1 = jnp.min(jnp.where(l1 == m1, lane, LANES), axis=-1, keepdims=True)
        l2 = jnp.where(lane == i1, neg, l1)
        m2 = jnp.max(l2, axis=-1, keepdims=True)
        i2 = jnp.min(jnp.where(l2 == m2, lane, LANES), axis=-1, keepdims=True)
        ex = jnp.exp(m2 - m1)
        gt1 = 1.0 / (1.0 + ex)
        gt2 = ex * gt1
        rt = jnp.where(lane == 0, i1.astype(F32),
                       jnp.where(lane == 1, i2.astype(F32),
                                 jnp.where(lane == 2, gt1, jnp.where(lane == 3, gt2, 0.0))))
        rt_ref[0] = rt


def _merge_call(ya, yr, yd, z, x, mod3, mod_row, nw2, wb, wo, router_w=None):
    bsz, n, d = x.shape
    tm = min(512, n)
    route = router_w is not None
    tok = lambda b, i: (b, i, 0)
    gu = d // LANES
    in_specs = [
        pl.BlockSpec((1, tm, CONV_W), tok), pl.BlockSpec((1, tm, RG_W), tok),
        pl.BlockSpec((1, tm, DA_HEADS * DA_V_DIM), tok),
        pl.BlockSpec((1, tm, d), lambda b, i: (b, i, Z_FULL["g_a"] // gu)),
        pl.BlockSpec((1, tm, d), lambda b, i: (b, i, Z_FULL["g_r"] // gu)),
        pl.BlockSpec((1, tm, d), lambda b, i: (b, i, Z_FULL["g_d"] // gu)),
        pl.BlockSpec((1, tm, d), tok),
        pl.BlockSpec((1, 1, d), lambda b, i: (mod_row(b), 0, 2)),
        pl.BlockSpec((1, d), lambda b, i: (0, 0)),
        pl.BlockSpec((1, 1, d), lambda b, i: (mod_row(b), 0, 3)),
        pl.BlockSpec((1, 1, d), lambda b, i: (mod_row(b), 0, 4)),
        pl.BlockSpec(wb.shape, lambda b, i: (0, 0, 0)),
        pl.BlockSpec(wo.shape, lambda b, i: (0, 0)),
    ]
    args = [ya, yr, yd, z, z, z, x, mod3, nw2, mod3, mod3, wb, wo]
    out_specs = [pl.BlockSpec((1, tm, d), tok), pl.BlockSpec((1, tm, d), tok)]
    out_shape = [jax.ShapeDtypeStruct((bsz, n, d), F32),
                 jax.ShapeDtypeStruct((bsz, n, d), F32 if route else BF16)]
    if route:
        in_specs.append(pl.BlockSpec(router_w.shape, lambda b, i: (0, 0)))
        args.append(router_w)
        out_specs.append(pl.BlockSpec((1, tm, LANES), tok))
        out_shape.append(jax.ShapeDtypeStruct((bsz, n, LANES), F32))
    return pl.pallas_call(
        functools.partial(_merge_kernel, route=route),
        grid=(bsz, n // tm),
        in_specs=in_specs,
        out_specs=out_specs,
        out_shape=out_shape,
        compiler_params=_cparams(("parallel", "parallel")),
        name="merge",
    )(*args)


def _ffn_kernel(h_ref, x_ref, g2_ref, wg_ref, wu_ref, w2_ref, o_ref, acc_ref):
    f = pl.program_id(2)
    h = h_ref[0]
    g = jnp.dot(h, wg_ref[...], preferred_element_type=F32)
    u = jnp.dot(h, wu_ref[...], preferred_element_type=F32)
    act = (_silu(g) * u).astype(BF16)
    part = jnp.dot(act, w2_ref[...], preferred_element_type=F32)

    @pl.when(f == 0)
    def _():
        acc_ref[...] = part

    @pl.when(f > 0)
    def _():
        acc_ref[...] += part

    @pl.when(f == pl.num_programs(2) - 1)
    def _():
        o_ref[0] = x_ref[0] + g2_ref[0] * acc_ref[...]


def _ffn_call(h, x, mod3, mod_row, w13, w2):
    bsz, n, d = x.shape
    ff = w2.shape[0]
    tm = min(1024, n)
    tf = 512
    nf = ff // tf
    tok = lambda b, i, f: (b, i, 0)
    return pl.pallas_call(
        _ffn_kernel,
        grid=(bsz, n // tm, nf),
        in_specs=[
            pl.BlockSpec((1, tm, d), tok),
            pl.BlockSpec((1, tm, d), tok),
            pl.BlockSpec((1, 1, d), lambda b, i, f: (mod_row(b), 0, 5)),
            pl.BlockSpec((d, tf), lambda b, i, f: (0, f)),
            pl.BlockSpec((d, tf), lambda b, i, f: (0, nf + f)),
            pl.BlockSpec((tf, d), lambda b, i, f: (f, 0)),
        ],
        out_specs=pl.BlockSpec((1, tm, d), tok),
        out_shape=jax.ShapeDtypeStruct((bsz, n, d), F32),
        scratch_shapes=[pltpu.VMEM((tm, d), F32)],
        compiler_params=_cparams(("parallel", "parallel", "arbitrary")),
        name="ffn",
    )(h, x, mod3, w13, w13, w2)


def _moe_kernel(be_ref, nu_ref, tok_ref, slot_ref, h_hbm, wg_ref, wu_ref, w2_ref, y_hbm,
                xg, xb, acc_ref, gsem, ssem, *, rows):
    i = pl.program_id(0)
    f = pl.program_id(1)
    nf = pl.num_programs(1)
    used = i < nu_ref[0]

    def gather_copy(r):
        return pltpu.make_async_copy(h_hbm.at[pl.ds(tok_ref[0, 0, r], 1)], xg.at[pl.ds(r, 1)], gsem)

    def scatter_copy(r):
        return pltpu.make_async_copy(acc_ref.at[pl.ds(r, 1)], y_hbm.at[pl.ds(slot_ref[0, 0, r], 1)], ssem)

    def for_rows(fn):
        def body(r, carry):
            fn(r)
            return carry
        lax.fori_loop(0, rows, body, 0)

    @pl.when(used & (f == 0))
    def _():
        for_rows(lambda r: gather_copy(r).start())
        for_rows(lambda r: gather_copy(r).wait())
        xb[...] = xg[...].astype(BF16)

    @pl.when(used)
    def _():
        x = xb[...]
        g = jnp.dot(x, wg_ref[0], preferred_element_type=F32)
        u = jnp.dot(x, wu_ref[0], preferred_element_type=F32)
        act = (_silu(g) * u).astype(BF16)
        part = jnp.dot(act, w2_ref[0], preferred_element_type=F32)

        @pl.when(f == 0)
        def _():
            acc_ref[...] = part

        @pl.when(f > 0)
        def _():
            acc_ref[...] += part

    @pl.when(used & (f == nf - 1))
    def _():
        for_rows(lambda r: scatter_copy(r).start())
        for_rows(lambda r: scatter_copy(r).wait())


def _moe_call(h2, block_e, n_used, buf_tok, buf_slot, w13, w2, rows, n_slots_out):
    t, d = h2.shape
    n_blocks = block_e.shape[0]
    ff = w2.shape[1]
    tf = 512
    nf = ff // tf

    def fi(i, f, be, nu):
        return jnp.where(i < nu[0], f, nf - 1)

    grid_spec = pltpu.PrefetchScalarGridSpec(
        num_scalar_prefetch=2,
        grid=(n_blocks, nf),
        in_specs=[
            pl.BlockSpec((1, 1, rows), lambda i, f, be, nu: (i, 0, 0), memory_space=pltpu.SMEM),
            pl.BlockSpec((1, 1, rows), lambda i, f, be, nu: (i, 0, 0), memory_space=pltpu.SMEM),
            pl.BlockSpec(memory_space=pl.ANY),
            pl.BlockSpec((1, d, tf), lambda i, f, be, nu: (be[i], 0, fi(i, f, be, nu))),
            pl.BlockSpec((1, d, tf), lambda i, f, be, nu: (be[i], 0, nf + fi(i, f, be, nu))),
            pl.BlockSpec((1, tf, d), lambda i, f, be, nu: (be[i], fi(i, f, be, nu), 0)),
        ],
        out_specs=pl.BlockSpec(memory_space=pl.ANY),
        scratch_shapes=[pltpu.VMEM((rows, d), F32), pltpu.VMEM((rows, d), BF16), pltpu.VMEM((rows, d), F32),
                        pltpu.SemaphoreType.DMA(()), pltpu.SemaphoreType.DMA(())],
    )
    return pl.pallas_call(
        functools.partial(_moe_kernel, rows=rows),
        grid_spec=grid_spec,
        out_shape=jax.ShapeDtypeStruct((n_slots_out, d), F32),
        compiler_params=_cparams(("arbitrary", "arbitrary")),
        name="moe",
    )(block_e, n_used, buf_tok, buf_slot, h2, w13, w13, w2)


def _route_plan(route, rows):
    t = route.shape[0]
    e = route[:, :2].astype(jnp.int32)
    flat_e = e.reshape(-1)
    n_slots = 2 * t
    onehot = (flat_e[:, None] == jnp.arange(N_EXPERTS, dtype=jnp.int32)[None, :]).astype(jnp.int32)
    csum = jnp.cumsum(onehot, axis=0)
    rank = jnp.sum(csum * onehot, axis=1) - 1
    counts = csum[-1]
    padded = (counts + rows - 1) // rows * rows
    pad_end = jnp.cumsum(padded)
    pad_start = pad_end - padded
    dest = pad_start[flat_e] + rank
    n_blocks = -(-n_slots // rows) + N_EXPERTS
    n_rows = n_blocks * rows
    ids = jnp.arange(n_slots, dtype=jnp.int32)
    tok = ids // 2
    slot = (ids % 2) * t + tok
    buf_tok = jnp.zeros((n_rows,), jnp.int32).at[dest].set(tok)
    dummy = n_slots + jnp.arange(n_rows, dtype=jnp.int32) % rows
    buf_slot = dummy.at[dest].set(slot)
    block_e = jnp.minimum(
        jnp.searchsorted(pad_end, jnp.arange(n_blocks, dtype=jnp.int32) * rows, side="right"),
        N_EXPERTS - 1).astype(jnp.int32)
    n_used = (pad_end[-1] // rows).astype(jnp.int32).reshape(1)
    return (block_e, n_used, buf_tok.reshape(n_blocks, 1, rows), buf_slot.reshape(n_blocks, 1, rows))


def _final_kernel(x_ref, y0_ref, y1_ref, rt_ref, g2_ref, fw_ref, o_ref):
    rt = rt_ref[0]
    y = rt[:, 2:3] * y0_ref[...] + rt[:, 3:4] * y1_ref[...]
    xn = x_ref[0] + g2_ref[0] * y
    o_ref[0] = _rms(xn) * fw_ref[...]


def _final_call(x, y, route3, mod3, mod_row, fw):
    bsz, n, d = x.shape
    tm = min(512, n)
    nt = n // tm
    t_blocks = bsz * nt
    tok = lambda b, i: (b, i, 0)
    return pl.pallas_call(
        _final_kernel,
        grid=(bsz, nt),
        in_specs=[
            pl.BlockSpec((1, tm, d), tok),
            pl.BlockSpec((tm, d), lambda b, i: (b * nt + i, 0)),
            pl.BlockSpec((tm, d), lambda b, i: (t_blocks + b * nt + i, 0)),
            pl.BlockSpec((1, tm, LANES), tok),
            pl.BlockSpec((1, 1, d), lambda b, i: (mod_row(b), 0, 5)),
            pl.BlockSpec((1, d), lambda b, i: (0, 0)),
        ],
        out_specs=pl.BlockSpec((1, tm, d), tok),
        out_shape=jax.ShapeDtypeStruct((bsz, n, d), F32),
        compiler_params=_cparams(("parallel", "parallel")),
        name="combine_final",
    )(x, y, y, route3, mod3, fw)


def _norm_kernel(x_ref, fw_ref, o_ref):
    o_ref[0] = _rms(x_ref[0]) * fw_ref[...]


def _norm_call(x, fw):
    bsz, n, d = x.shape
    tm = min(512, n)
    tok = lambda b, i: (b, i, 0)
    return pl.pallas_call(
        _norm_kernel,
        grid=(bsz, n // tm),
        in_specs=[pl.BlockSpec((1, tm, d), tok), pl.BlockSpec((1, d), lambda b, i: (0, 0))],
        out_specs=pl.BlockSpec((1, tm, d), tok),
        out_shape=jax.ShapeDtypeStruct((bsz, n, d), F32),
        compiler_params=_cparams(("parallel", "parallel")),
        name="final_norm",
    )(x, fw)


def _rope_perm():
    perm = np.zeros((LANES,), np.int32)
    half = ROPE_AXIS_DIM // 2
    for i in range(LANES // 2):
        comp, axis, j = i // 32, (i % 32) // half, i % half
        perm[i] = comp * DA_HEAD_DIM + axis * ROPE_AXIS_DIM + j
        perm[LANES // 2 + i] = perm[i] + half
    return perm


def _rope_tables(n):
    rows = n // GRID_W
    pos_r = jnp.repeat(jnp.arange(rows, dtype=F32), GRID_W)
    pos_c = jnp.broadcast_to(jnp.arange(GRID_W, dtype=F32), (rows, GRID_W)).reshape(-1)
    inv_freq = ROPE_BASE ** (-jnp.arange(0, ROPE_AXIS_DIM, 2, dtype=F32) / ROPE_AXIS_DIM)
    ang = jnp.stack([pos_r[:, None] * inv_freq, pos_c[:, None] * inv_freq], axis=1)
    cos, sin = jnp.cos(ang), jnp.sin(ang)
    half = jnp.concatenate([cos.reshape(n, 32), cos.reshape(n, 32)], axis=1)
    cos_t = jnp.concatenate([half, half], axis=1)
    sh = jnp.concatenate([sin.reshape(n, 32), sin.reshape(n, 32)], axis=1)
    sin_t = jnp.concatenate([-sh, sh], axis=1)
    return cos_t, sin_t


def _blockdiag_gate(wa, wx):
    per = HALF // (RG_W // RG_BLOCKS)
    eye = jnp.eye(per, dtype=wa.dtype)

    def bd(w):
        w4 = w.reshape(2, RG_BLOCKS // per, per, w.shape[-2], w.shape[-1])
        return jnp.einsum("dhbij,bc->dhbicj", w4, eye).reshape(2, RG_BLOCKS // per, HALF, HALF)

    return jnp.concatenate([bd(wa), bd(wx)], axis=-1).astype(BF16)


def kernel(x, c, ctx, c_ctx, mod_w, mod_b, norm1_w, norm2_w, w_in, conv_a_w, rg_conv_w, rg_conv_b, rg_wa, rg_ba, rg_wx, rg_bx, rg_lambda, da_lambda, da_subln_w, w_branch, w_out, ffn_w13, ffn_w2, router_w, moe_w13, moe_w2, final_norm_w):
    bsz, n, d = x.shape
    nc = ctx.shape[1]
    depth = mod_w.shape[0]
    cos_t, sin_t = _rope_tables(n)
    perm = _rope_perm()
    qk_cols = np.arange(w_in.shape[2], dtype=np.int32)
    for name in ("q", "k"):
        for h in range(DA_HEADS):
            base = (Z_FULL[name] + h) * LANES
            qk_cols[base:base + LANES] = base + perm

    bp = -(-(bsz + 1) // 8) * 8
    cc = jnp.zeros((bp, d), F32).at[:bsz].set(c).at[bsz].set(c_ctx)
    lat_row = lambda b: b
    ctx_row = lambda b: bsz
    cx = ctx
    moe_rows = 1024 if bsz * n >= 8192 else 128

    for l in range(depth):
        last = l == depth - 1
        lam_init = 0.8 - 0.6 * math.exp(-0.3 * l)
        j = l // 2
        mod, lam = _mod_call(cc, mod_w[l], mod_b[l][None, :], da_lambda[l], lam_init)
        mod3 = mod.reshape(bp, 1, N_MOD * d)
        w_in_l = w_in[l][:, qk_cols].astype(BF16)
        nw1 = norm1_w[l][None, :]
        nw2 = norm2_w[l][None, :]
        p = dict(
            conv_a_w=conv_a_w[l], rg_conv_w=rg_conv_w[l], rg_conv_b=rg_conv_b[l][:, None, :],
            rg_wg=_blockdiag_gate(rg_wa[l], rg_wx[l]),
            rg_ba=rg_ba[l].reshape(2, 1, RG_W), rg_bx=rg_bx[l].reshape(2, 1, RG_W),
            rg_lambda=rg_lambda[l][:, None, :])
        subw = da_subln_w[l][None, :]
        wb = w_branch[l].astype(BF16)
        wo = w_out[l].astype(BF16)

        ncol = w_in.shape[2] // COL
        z = _in_call(x, nw1, mod3, lat_row, w_in_l, range(ncol),
                     (Z_FULL["q"] * LANES // COL, Z_FULL["k"] * LANES // COL), cos_t, sin_t)
        if last:
            zc = _in_call(cx, nw1, mod3, ctx_row, w_in_l, (4, 6, 7), None)
            zc_off = Z_CTX_LAST
        else:
            zc = _in_call(cx, nw1, mod3, ctx_row, w_in_l, range(ncol), None)
            zc_off = Z_FULL

        seq_out = _seq_call(z, zc, zc_off, p, not last)
        yd = _attn_call(z, Z_FULL["q"], z, zc, zc_off, lam, subw, lam_init, True)
        moe_layer = l % 2 == 1
        if moe_layer:
            rw = jnp.zeros((d, LANES), F32).at[:, :N_EXPERTS].set(router_w[j])
            xn, h, route = _merge_call(seq_out[0], seq_out[1], yd, z, x, mod3, lat_row, nw2, wb, wo, rw)
            t = bsz * n
            route2 = route.reshape(t, LANES)
            block_e, n_used, buf_tok, buf_slot = _route_plan(route2, moe_rows)
            y = _moe_call(h.reshape(t, d), block_e, n_used, buf_tok, buf_slot,
                          moe_w13[j].astype(BF16), moe_w2[j].astype(BF16), moe_rows, 2 * t + moe_rows)
            if last:
                return _final_call(xn, y, route, mod3, lat_row, final_norm_w[None, :])
            raise NotImplementedError("expert layer followed by another layer")
        xn, h = _merge_call(seq_out[0], seq_out[1], yd, z, x, mod3, lat_row, nw2, wb, wo)
        w13 = ffn_w13[j].astype(BF16)
        w2 = ffn_w2[j].astype(BF16)
        x = _ffn_call(h, xn, mod3, lat_row, w13, w2)
        if not last:
            ycd = _attn_call(zc, Z_FULL["q"], None, zc, zc_off, lam, subw, lam_init, False)
            cxn, hc = _merge_call(seq_out[2], seq_out[3], ycd, zc, cx, mod3, ctx_row, nw2, wb, wo)
            cx = _ffn_call(hc, cxn, mod3, ctx_row, w13, w2)
    return _norm_call(x, final_norm_w[None, :])
```

```python
import functools
import math

import jax
import jax.numpy as jnp
import numpy as np
from jax import lax
from jax.experimental import pallas as pl
from jax.experimental.pallas import tpu as pltpu

F32 = jnp.float32
BF16 = jnp.bfloat16

EPS = 1e-6
N_MOD = 6
GRID_W = 64
CONV_W = 512
RG_W = 512
RG_BLOCKS = 8
RG_C = 8.0
DA_HEADS = 4
DA_HEAD_DIM = 64
DA_V_DIM = 128
ROPE_AXIS_DIM = 32
ROPE_BASE = 10000.0
N_EXPERTS = 8
LANES = 128
HALF = 256
VMEM_LIMIT = 56 * 1024 * 1024

COL = 1024
REF_COLS = (("a_b", 0, 512), ("a_c", 512, 512), ("a_x", 1024, 512), ("r_g", 1536, 512), ("r_x", 2048, 512),
            ("q", 2560, 512), ("k", 3072, 512), ("v", 3584, 512), ("g_a", 4096, 1024), ("g_r", 5120, 1024),
            ("g_d", 6144, 1024))
Z_ORDER = ("q", "k", "v", "a_b", "a_c", "a_x", "r_g", "r_x", "g_a", "g_r", "g_d")
Z_FULL = dict(q=0, k=4, v=8, a_b=12, a_c=16, a_x=20, r_g=24, r_x=28, g_a=32, g_r=40, g_d=48)
CTX_LAST_BLOCKS = (0, 1, 3)
Z_CTX_LAST = dict(k=4, v=8, r_x=20)
LOG2E = 1.4426950408889634


def _cparams(sem, vmem=VMEM_LIMIT):
    return pltpu.CompilerParams(dimension_semantics=sem, vmem_limit_bytes=vmem)


def _sigmoid(x):
    return 1.0 / (1.0 + jnp.exp(-x))


def _silu(x):
    return x * _sigmoid(x)


def _gelu_tanh(x):
    return 0.5 * x * (1.0 + jnp.tanh(math.sqrt(2.0 / math.pi) * (x + 0.044715 * (x * x * x))))


def _rms(x):
    return x * lax.rsqrt(jnp.mean(x * x, axis=-1, keepdims=True) + EPS)


def _mod_kernel(cc_ref, w_ref, b_ref, dl_ref, mod_ref, lam_ref, *, lam_init):
    cc = cc_ref[...]
    s = _silu(cc)
    mod_ref[...] = jnp.dot(s, w_ref[...], preferred_element_type=F32,
                           precision=lax.Precision.HIGHEST) + b_ref[...]
    dl = dl_ref[...]
    s1 = jnp.sum(dl[0:1] * dl[1:2], axis=-1, keepdims=True)
    s2 = jnp.sum(dl[2:3] * dl[3:4], axis=-1, keepdims=True)
    lam = jnp.exp(s1) - jnp.exp(s2) + lam_init
    lam_ref[...] = jnp.broadcast_to(lam, lam_ref.shape)


def _mod_call(cc, w, b, dl, lam_init):
    bp, d = cc.shape
    nd = w.shape[1]
    tn = 1024
    return pl.pallas_call(
        functools.partial(_mod_kernel, lam_init=lam_init),
        grid=(nd // tn,),
        in_specs=[
            pl.BlockSpec((bp, d), lambda j: (0, 0)),
            pl.BlockSpec((d, tn), lambda j: (0, j)),
            pl.BlockSpec((1, tn), lambda j: (0, j)),
            pl.BlockSpec(dl.shape, lambda j: (0, 0)),
        ],
        out_specs=[
            pl.BlockSpec((bp, tn), lambda j: (0, j)),
            pl.BlockSpec((8, LANES), lambda j: (0, 0)),
        ],
        out_shape=[jax.ShapeDtypeStruct((bp, nd), F32), jax.ShapeDtypeStruct((8, LANES), F32)],
        compiler_params=_cparams(("arbitrary",)),
        name="mod",
    )(cc, w, b, dl)


def _in_kernel(*refs, rope):
    if rope:
        x_ref, nw_ref, sh_ref, sc_ref, cos_ref, sin_ref, w_ref, o_ref, u_scr = refs
    else:
        x_ref, nw_ref, sh_ref, sc_ref, w_ref, o_ref, u_scr = refs
    j = pl.program_id(2)

    @pl.when(j == 0)
    def _():
        y = _rms(x_ref[0]) * nw_ref[...]
        u_scr[...] = (y * (1.0 + sc_ref[0]) + sh_ref[0]).astype(BF16)

    acc = jnp.dot(u_scr[...], w_ref[...], preferred_element_type=F32)
    q_scale = DA_HEAD_DIM ** -0.5 * LOG2E
    n_q = DA_HEADS * 2 * DA_HEAD_DIM // LANES

    @pl.when(j == 0)
    def _():
        for g in range(acc.shape[1] // LANES):
            r = acc[:, g * LANES:(g + 1) * LANES]
            if rope:
                r = r * cos_ref[...] + pltpu.roll(r, LANES // 2, 1) * sin_ref[...]
            if g < n_q:
                r = r * q_scale
            o_ref[0, :, g * LANES:(g + 1) * LANES] = r.astype(o_ref.dtype)

    @pl.when(j != 0)
    def _():
        o_ref[0] = acc.astype(o_ref.dtype)


def _in_call(x, nw, mod3, mod_row, w, col_blocks, rope, cos_t=None, sin_t=None):
    bsz, n, d = x.shape
    tm = min(1024, n)
    nj = len(col_blocks)
    cb = tuple(col_blocks)
    if cb == tuple(range(nj)):
        wmap = lambda b, i, j: (0, j)
    else:
        assert cb == CTX_LAST_BLOCKS
        wmap = lambda b, i, j: (0, j + j // 2)
    in_specs = [
        pl.BlockSpec((1, tm, d), lambda b, i, j: (b, i, 0)),
        pl.BlockSpec((1, d), lambda b, i, j: (0, 0)),
        pl.BlockSpec((1, 1, d), lambda b, i, j: (mod_row(b), 0, 0)),
        pl.BlockSpec((1, 1, d), lambda b, i, j: (mod_row(b), 0, 1)),
    ]
    args = [x, nw, mod3, mod3]
    if rope:
        in_specs += [pl.BlockSpec((tm, LANES), lambda b, i, j: (i, 0)),
                     pl.BlockSpec((tm, LANES), lambda b, i, j: (i, 0))]
        args += [cos_t, sin_t]
    in_specs.append(pl.BlockSpec((d, COL), wmap))
    args.append(w)
    return pl.pallas_call(
        functools.partial(_in_kernel, rope=rope),
        grid=(bsz, n // tm, nj),
        in_specs=in_specs,
        out_specs=pl.BlockSpec((1, tm, COL), lambda b, i, j: (b, i, j)),
        out_shape=jax.ShapeDtypeStruct((bsz, n, nj * COL), BF16),
        scratch_shapes=[pltpu.VMEM((tm, d), BF16)],
        compiler_params=_cparams(("parallel", "parallel", "arbitrary")),
        name="in_proj",
    )(*args)


SEQ_TILE = 64
PAD = 8


def _seq_kernel(*refs, n, nc, need_ctx_out):
    it = iter(refs)
    ab_ref, ac_ref, ax_ref, rg_ref, rx_ref = (next(it) for _ in range(5))
    if need_ctx_out:
        cab_ref, cac_ref, cax_ref, crg_ref = (next(it) for _ in range(4))
    crx_ref = next(it)
    wa_ref, cw_ref, cbias_ref, wg_ref, ba_ref, bx_ref, lam_ref = (next(it) for _ in range(7))
    ya_ref, yr_ref = next(it), next(it)
    if need_ctx_out:
        yca_ref, ycr_ref = next(it), next(it)
    xp, a_scr, b_scr = next(it), next(it), next(it)
    T = SEQ_TILE
    W = HALF

    def stage(val, ns):
        xp[0:PAD, :] = jnp.zeros((PAD, W), F32)
        xp[PAD:PAD + ns, :] = val
        xp[PAD + ns:2 * PAD + ns, :] = jnp.zeros((PAD, W), F32)

    def conv_mixer(b_ref, c_ref, x_ref, out_ref, ns):
        stage(c_ref[0].astype(F32) * x_ref[0].astype(F32), ns)
        w0, w1, w2 = wa_ref[0:1, :], wa_ref[1:2, :], wa_ref[2:3, :]

        def body(ti, carry):
            t0 = pl.multiple_of(ti * T, T)
            ps = xp[pl.ds(t0, T + 2 * PAD), :]
            conv = w0 * ps[PAD - 1:PAD - 1 + T] + w1 * ps[PAD:PAD + T] + w2 * ps[PAD + 1:PAD + 1 + T]
            gate = b_ref[0, pl.ds(t0, T), :].astype(F32)
            out_ref[0, pl.ds(t0, T), :] = (gate * conv).astype(out_ref.dtype)
            return carry

        lax.fori_loop(0, ns // T, body, 0)

    nl = -lam_ref[...]
    sp = jnp.maximum(nl, 0.0) + jnp.log1p(jnp.exp(-jnp.abs(nl)))
    NG = W // LANES

    def rglru(x_ref, ns, h0):
        stage(x_ref[0].astype(F32), ns)

        def gates_body(ti, carry):
            t0 = pl.multiple_of(ti * T, T)
            xs = xp[pl.ds(t0, T + 2 * PAD), :]
            for d in range(2):
                off = PAD - 3 if d == 0 else PAD
                xc = cbias_ref[d]
                for k in range(4):
                    xc = xc + cw_ref[d, k:k + 1, :] * xs[off + k:off + k + T]
                g = jnp.dot(xc.astype(BF16), wg_ref[d, 0], preferred_element_type=F32)
                r = _sigmoid(g[:, :W] + ba_ref[d])
                i = _sigmoid(g[:, W:] + bx_ref[d])
                log_a = (-RG_C * r) * sp[d]
                a = jnp.exp(log_a)
                bb = jnp.sqrt(1.0 - a * a) * (i * xc)
                for g in range(NG):
                    a_scr[d, g, pl.ds(t0, T), :] = a[:, g * LANES:(g + 1) * LANES]
                    b_scr[d, g, pl.ds(t0, T), :] = bb[:, g * LANES:(g + 1) * LANES]
            return carry

        lax.fori_loop(0, ns // T, gates_body, 0)
        nch = ns // 8
        sub = lax.broadcasted_iota(jnp.int32, (8, LANES), 0)

        def scan_body(ci, carry):
            out = []
            for d in range(2):
                r0 = pl.multiple_of((ci if d == 0 else nch - 1 - ci) * 8, 8)
                for g in range(NG):
                    a = a_scr[d, g, pl.ds(r0, 8), :]
                    b = b_scr[d, g, pl.ds(r0, 8), :]
                    for s in (1, 2, 4):
                        sh = s if d == 0 else 8 - s
                        m = (sub >= s) if d == 0 else (sub < 8 - s)
                        b = b + a * jnp.where(m, pltpu.roll(b, sh, 0), 0.0)
                        a = a * jnp.where(m, pltpu.roll(a, sh, 0), 1.0)
                    h = b + a * carry[d * NG + g]
                    b_scr[d, g, pl.ds(r0, 8), :] = h
                    last = h[7:8, :] if d == 0 else h[0:1, :]
                    out.append(jnp.broadcast_to(last, (8, LANES)))
            return tuple(out)

        init = tuple(jnp.broadcast_to(h0[d][g], (8, LANES)) for d in range(2) for g in range(NG))
        ends = lax.fori_loop(0, nch, scan_body, init, unroll=2)
        return [[ends[d * NG + g][0:1, :] for g in range(NG)] for d in range(2)]

    def rg_out(g_ref, out_ref, ns):
        def body(ti, carry):
            t0 = pl.multiple_of(ti * T, T)
            for g in range(NG):
                hs = b_scr[0, g, pl.ds(t0, T), :] + b_scr[1, g, pl.ds(t0, T), :]
                gate = g_ref[0, pl.ds(t0, T), g * LANES:(g + 1) * LANES].astype(F32)
                out_ref[0, pl.ds(t0, T), g * LANES:(g + 1) * LANES] = (_gelu_tanh(gate) * hs).astype(out_ref.dtype)
            return carry

        lax.fori_loop(0, ns // T, body, 0)

    zero = [[jnp.zeros((1, LANES), F32)] * NG] * 2
    h_ctx = rglru(crx_ref, nc, zero)
    if need_ctx_out:
        rg_out(crg_ref, ycr_ref, nc)
        conv_mixer(cab_ref, cac_ref, cax_ref, yca_ref, nc)
    rglru(rx_ref, n, h_ctx)
    rg_out(rg_ref, yr_ref, n)
    conv_mixer(ab_ref, ac_ref, ax_ref, ya_ref, n)


def _seq_call(z, zc, zc_off, p, need_ctx_out):
    bsz, n, _ = z.shape
    nc = zc.shape[1]
    u = HALF // LANES

    def zspec(rows, off):
        return pl.BlockSpec((1, rows, HALF), lambda b, c: (b, 0, off // u + c))

    names = ("a_b", "a_c", "a_x", "r_g", "r_x")
    in_specs = [zspec(n, Z_FULL[k]) for k in names]
    args = [z] * 5
    if need_ctx_out:
        in_specs += [zspec(nc, zc_off[k]) for k in names[:4]]
        args += [zc] * 4
    in_specs.append(zspec(nc, zc_off["r_x"]))
    args.append(zc)
    in_specs += [
        pl.BlockSpec((3, HALF), lambda b, c: (0, c)),
        pl.BlockSpec((2, 4, HALF), lambda b, c: (0, 0, c)),
        pl.BlockSpec((2, 1, HALF), lambda b, c: (0, 0, c)),
        pl.BlockSpec((2, 1, HALF, 2 * HALF), lambda b, c: (0, c, 0, 0)),
        pl.BlockSpec((2, 1, HALF), lambda b, c: (0, 0, c)),
        pl.BlockSpec((2, 1, HALF), lambda b, c: (0, 0, c)),
        pl.BlockSpec((2, 1, HALF), lambda b, c: (0, 0, c)),
    ]
    args += [p["conv_a_w"], p["rg_conv_w"], p["rg_conv_b"], p["rg_wg"], p["rg_ba"], p["rg_bx"], p["rg_lambda"]]
    out_specs = [pl.BlockSpec((1, n, HALF), lambda b, c: (b, 0, c))] * 2
    out_shape = [jax.ShapeDtypeStruct((bsz, n, CONV_W), BF16), jax.ShapeDtypeStruct((bsz, n, RG_W), BF16)]
    if need_ctx_out:
        out_specs += [pl.BlockSpec((1, nc, HALF), lambda b, c: (b, 0, c))] * 2
        out_shape += [jax.ShapeDtypeStruct((bsz, nc, CONV_W), BF16), jax.ShapeDtypeStruct((bsz, nc, RG_W), BF16)]
    npad = max(n, nc) + 2 * PAD
    return pl.pallas_call(
        functools.partial(_seq_kernel, n=n, nc=nc, need_ctx_out=need_ctx_out),
        grid=(bsz, CONV_W // HALF),
        in_specs=in_specs,
        out_specs=out_specs,
        out_shape=out_shape,
        scratch_shapes=[pltpu.VMEM((npad, HALF), F32), pltpu.VMEM((2, u, max(n, nc), LANES), F32),
                        pltpu.VMEM((2, u, max(n, nc), LANES), F32)],
        compiler_params=_cparams(("parallel", "parallel")),
        name="seq_mixers",
    )(*args)


def _attn_kernel(*refs, has_lat, lam_init):
    if has_lat:
        q_ref, k_ref, v_ref, ck_ref, cv_ref, lam_ref, sw_ref, o_ref, k_all, v_ext = refs
    else:
        q_ref, ck_ref, cv_ref, lam_ref, sw_ref, o_ref, k_all, v_ext = refs
    nc = ck_ref.shape[1]
    nk = k_all.shape[0]

    @pl.when(pl.program_id(1) == 0)
    def _():
        k_all[0:nc, :] = ck_ref[0]
        if has_lat:
            k_all[nc:nk, :] = k_ref[0]
        for h in range(DA_HEADS):
            sl = slice(h * LANES, (h + 1) * LANES)
            v_ext[h, 0:nc, 0:LANES] = cv_ref[0, :, sl]
            if has_lat:
                v_ext[h, nc:nk, 0:LANES] = v_ref[0, :, sl]
            v_ext[h, :, LANES:2 * LANES] = jnp.ones((nk, LANES), v_ext.dtype)

    lane = lax.broadcasted_iota(jnp.int32, (1, LANES), 1)
    map1 = (lane % DA_HEAD_DIM) < (DA_HEAD_DIM // 2)
    dn = (((1,), (1,)), ((), ()))
    lam = lam_ref[0:1, 0:1]

    for h in range(DA_HEADS):
        sl = slice(h * LANES, (h + 1) * LANES)
        q = q_ref[0, :, sl]
        zero = jnp.zeros_like(q)

        def soft_v(qm):
            s = lax.dot_general(qm, k_all[:, sl], dn, preferred_element_type=F32)
            e = jnp.exp2(s - jnp.max(s, axis=-1, keepdims=True)).astype(BF16)
            r = jnp.dot(e, v_ext[h], preferred_element_type=F32)
            return r[:, :LANES] / r[:, LANES:LANES + 1]

        o = soft_v(jnp.where(map1, q, zero)) - lam * soft_v(jnp.where(map1, zero, q))
        y = _rms(o) * sw_ref[...] * (1.0 - lam_init)
        o_ref[0, :, sl] = y.astype(o_ref.dtype)


def _attn_call(zq, z, zc, zc_off, lam, subw, lam_init, has_lat):
    bsz, nq, _ = zq.shape
    nc = zc.shape[1]
    tq = min(256, nq)
    w = DA_HEADS * LANES
    u = DA_HEADS
    in_specs = [pl.BlockSpec((1, tq, w), lambda b, i: (b, i, Z_FULL["q"] // u))]
    args = [zq]
    nk = nc
    if has_lat:
        n = z.shape[1]
        nk = nc + n
        in_specs += [pl.BlockSpec((1, n, w), lambda b, i: (b, 0, Z_FULL["k"] // u)),
                     pl.BlockSpec((1, n, w), lambda b, i: (b, 0, Z_FULL["v"] // u))]
        args += [z, z]
    in_specs += [pl.BlockSpec((1, nc, w), lambda b, i: (b, 0, zc_off["k"] // u)),
                 pl.BlockSpec((1, nc, w), lambda b, i: (b, 0, zc_off["v"] // u)),
                 pl.BlockSpec((8, LANES), lambda b, i: (0, 0)),
                 pl.BlockSpec((1, LANES), lambda b, i: (0, 0))]
    args += [zc, zc, lam, subw]
    return pl.pallas_call(
        functools.partial(_attn_kernel, has_lat=has_lat, lam_init=lam_init),
        grid=(bsz, nq // tq),
        in_specs=in_specs,
        out_specs=pl.BlockSpec((1, tq, w), lambda b, i: (b, i, 0)),
        out_shape=jax.ShapeDtypeStruct((bsz, nq, DA_HEADS * DA_V_DIM), BF16),
        scratch_shapes=[pltpu.VMEM((nk, w), BF16), pltpu.VMEM((DA_HEADS, nk, 2 * LANES), BF16)],
        compiler_params=_cparams(("parallel", "arbitrary")),
        name="diff_attn",
    )(*args)


def _merge_kernel(*refs, route):
    if route:
        (ya_ref, yr_ref, yd_ref, ga_ref, gr_ref, gd_ref, x_ref, g1_ref, nw_ref, sh_ref, sc_ref,
         wb_ref, wo_ref, rw_ref, xo_ref, h_ref, rt_ref) = refs
    else:
        (ya_ref, yr_ref, yd_ref, ga_ref, gr_ref, gd_ref, x_ref, g1_ref, nw_ref, sh_ref, sc_ref,
         wb_ref, wo_ref, xo_ref, h_ref) = refs
    m = None
    for i, (y_ref, g_ref) in enumerate(((ya_ref, ga_ref), (yr_ref, gr_ref), (yd_ref, gd_ref))):
        t = _sigmoid(g_ref[0].astype(F32)) * jnp.dot(y_ref[0], wb_ref[i], preferred_element_type=F32)
        m = t if m is None else m + t
    out = jnp.dot(m.astype(BF16), wo_ref[...], preferred_element_type=F32)
    xn = x_ref[0] + g1_ref[0] * out
    xo_ref[0] = xn
    h = (_rms(xn) * nw_ref[...]) * (1.0 + sc_ref[0]) + sh_ref[0]
    h_ref[0] = h.astype(h_ref.dtype)
    if route:
        logits = jnp.dot(h, rw_ref[...], preferred_element_type=F32, precision=lax.Precision.HIGHEST)
        lane = lax.broadcasted_iota(jnp.int32, logits.shape, 1)
        neg = jnp.float32(-jnp.inf)
        l1 = jnp.where(lane < N_EXPERTS, logits, neg)
        m1 = jnp.max(l1, axis=-1, keepdims=True)
        i1 = jnp.min(jnp.where(l1 == m1, lane, LANES), axis=-1, keepdims=True)
        l2 = jnp.where(lane == i1, neg, l1)
        m2 = jnp.max(l2, axis=-1, keepdims=True)
        i2 = jnp.min(jnp.where(l2 == m2, lane, LANES), axis=-1, keepdims=True)
        ex = jnp.exp(m2 - m1)
        gt1 = 1.0 / (1.0 + ex)
        gt2 = ex * gt1
        rt = jnp.where(lane == 0, i1.astype(F32),
                       jnp.where(lane == 1, i2.astype(F32),
                                 jnp.where(lane == 2, gt1, jnp.where(lane == 3, gt2, 0.0))))
        rt_ref[0] = rt


def _merge_call(ya, yr, yd, z, x, mod3, mod_row, nw2, wb, wo, router_w=None):
    bsz, n, d = x.shape
    tm = min(512, n)
    route = router_w is not None
    tok = lambda b, i: (b, i, 0)
    gu = d // LANES
    in_specs = [
        pl.BlockSpec((1, tm, CONV_W), tok), pl.BlockSpec((1, tm, RG_W), tok),
        pl.BlockSpec((1, tm, DA_HEADS * DA_V_DIM), tok),
        pl.BlockSpec((1, tm, d), lambda b, i: (b, i, Z_FULL["g_a"] // gu)),
        pl.BlockSpec((1, tm, d), lambda b, i: (b, i, Z_FULL["g_r"] // gu)),
        pl.BlockSpec((1, tm, d), lambda b, i: (b, i, Z_FULL["g_d"] // gu)),
        pl.BlockSpec((1, tm, d), tok),
        pl.BlockSpec((1, 1, d), lambda b, i: (mod_row(b), 0, 2)),
        pl.BlockSpec((1, d), lambda b, i: (0, 0)),
        pl.BlockSpec((1, 1, d), lambda b, i: (mod_row(b), 0, 3)),
        pl.BlockSpec((1, 1, d), lambda b, i: (mod_row(b), 0, 4)),
        pl.BlockSpec(wb.shape, lambda b, i: (0, 0, 0)),
        pl.BlockSpec(wo.shape, lambda b, i: (0, 0)),
    ]
    args = [ya, yr, yd, z, z, z, x, mod3, nw2, mod3, mod3, wb, wo]
    out_specs = [pl.BlockSpec((1, tm, d), tok), pl.BlockSpec((1, tm, d), tok)]
    out_shape = [jax.ShapeDtypeStruct((bsz, n, d), F32),
                 jax.ShapeDtypeStruct((bsz, n, d), F32 if route else BF16)]
    if route:
        in_specs.append(pl.BlockSpec(router_w.shape, lambda b, i: (0, 0)))
        args.append(router_w)
        out_specs.append(pl.BlockSpec((1, tm, LANES), tok))
        out_shape.append(jax.ShapeDtypeStruct((bsz, n, LANES), F32))
    return pl.pallas_call(
        functools.partial(_merge_kernel, route=route),
        grid=(bsz, n // tm),
        in_specs=in_specs,
        out_specs=out_specs,
        out_shape=out_shape,
        compiler_params=_cparams(("parallel", "parallel")),
        name="merge",
    )(*args)


def _ffn_kernel(h_ref, x_ref, g2_ref, wg_ref, wu_ref, w2_ref, o_ref, acc_ref):
    f = pl.program_id(2)
    h = h_ref[0]
    g = jnp.dot(h, wg_ref[...], preferred_element_type=F32)
    u = jnp.dot(h, wu_ref[...], preferred_element_type=F32)
    act = (_silu(g) * u).astype(BF16)
    part = jnp.dot(act, w2_ref[...], preferred_element_type=F32)

    @pl.when(f == 0)
    def _():
        acc_ref[...] = part

    @pl.when(f > 0)
    def _():
        acc_ref[...] += part

    @pl.when(f == pl.num_programs(2) - 1)
    def _():
        o_ref[0] = x_ref[0] + g2_ref[0] * acc_ref[...]


def _ffn_call(h, x, mod3, mod_row, w13, w2):
    bsz, n, d = x.shape
    ff = w2.shape[0]
    tm = min(1024, n)
    tf = 512
    nf = ff // tf
    tok = lambda b, i, f: (b, i, 0)
    return pl.pallas_call(
        _ffn_kernel,
        grid=(bsz, n // tm, nf),
        in_specs=[
            pl.BlockSpec((1, tm, d), tok),
            pl.BlockSpec((1, tm, d), tok),
            pl.BlockSpec((1, 1, d), lambda b, i, f: (mod_row(b), 0, 5)),
            pl.BlockSpec((d, tf), lambda b, i, f: (0, f)),
            pl.BlockSpec((d, tf), lambda b, i, f: (0, nf + f)),
            pl.BlockSpec((tf, d), lambda b, i, f: (f, 0)),
        ],
        out_specs=pl.BlockSpec((1, tm, d), tok),
        out_shape=jax.ShapeDtypeStruct((bsz, n, d), F32),
        scratch_shapes=[pltpu.VMEM((tm, d), F32)],
        compiler_params=_cparams(("parallel", "parallel", "arbitrary")),
        name="ffn",
    )(h, x, mod3, w13, w13, w2)


def _row_dma_params():
    return pltpu.CompilerParams(dimension_semantics=("arbitrary",), vmem_limit_bytes=VMEM_LIMIT,
                                disable_bounds_checks=True)


def _dispatch_kernel(dest_ref, pad_ref, h_ref, xs_hbm, scr, zrow, sem, *, tm, n_pad):
    i = pl.program_id(0)
    last = pl.num_programs(0) - 1
    slot = i % 2
    scr[slot] = h_ref[...]

    @pl.when(i == 0)
    def _():
        zrow[...] = jnp.zeros(zrow.shape, zrow.dtype)

    def body(r, carry):
        for k in range(2):
            pltpu.make_async_copy(scr.at[slot, pl.ds(r, 1)], xs_hbm.at[pl.ds(dest_ref[0, 0, 2 * r + k], 1)],
                                  sem.at[slot]).start()
        return carry

    lax.fori_loop(0, tm, body, 0, unroll=4)

    def pad_body(r, carry):
        pltpu.make_async_copy(zrow.at[pl.ds(0, 1)], xs_hbm.at[pl.ds(pad_ref[0, 0, r], 1)], sem.at[slot]).start()
        return carry

    lax.fori_loop(0, n_pad, pad_body, 0, unroll=4)

    def wait_all(s):
        left = 2 * tm + n_pad
        while left > 0:
            rows = min(left, tm)
            pltpu.make_async_copy(xs_hbm.at[pl.ds(0, rows)], scr.at[s, pl.ds(0, rows)], sem.at[s]).wait()
            left -= rows

    @pl.when(i > 0)
    def _():
        wait_all(1 - slot)

    @pl.when(i == last)
    def _():
        wait_all(slot)


def _dispatch_call(h2, dest3, pad3, n_rows, tm):
    t, d = h2.shape
    n_pad = pad3.shape[2]
    return pl.pallas_call(
        functools.partial(_dispatch_kernel, tm=tm, n_pad=n_pad),
        grid=(t // tm,),
        in_specs=[pl.BlockSpec((1, 1, 2 * tm), lambda i: (i, 0, 0), memory_space=pltpu.SMEM),
                  pl.BlockSpec((1, 1, n_pad), lambda i: (i, 0, 0), memory_space=pltpu.SMEM),
                  pl.BlockSpec((tm, d), lambda i: (i, 0))],
        out_specs=pl.BlockSpec(memory_space=pl.ANY),
        out_shape=jax.ShapeDtypeStruct((n_rows, d), F32),
        scratch_shapes=[pltpu.VMEM((2, tm, d), F32), pltpu.VMEM((8, d), F32), pltpu.SemaphoreType.DMA((2,))],
        compiler_params=_row_dma_params(),
        name="moe_dispatch",
    )(dest3, pad3, h2)


def _moe_kernel(be_ref, nu_ref, x_ref, wg_ref, wu_ref, w2_ref, o_ref, xb):
    i = pl.program_id(0)
    f = pl.program_id(1)
    used = i < nu_ref[0]

    @pl.when(used & (f == 0))
    def _():
        xb[...] = x_ref[...].astype(BF16)

    @pl.when(jnp.logical_not(used) & (f == 0))
    def _():
        o_ref[...] = jnp.zeros(o_ref.shape, o_ref.dtype)

    @pl.when(used)
    def _():
        x = xb[...]
        g = jnp.dot(x, wg_ref[0], preferred_element_type=F32)
        u = jnp.dot(x, wu_ref[0], preferred_element_type=F32)
        act = (_silu(g) * u).astype(BF16)
        part = jnp.dot(act, w2_ref[0], preferred_element_type=F32)

        @pl.when(f == 0)
        def _():
            o_ref[...] = part

        @pl.when(f > 0)
        def _():
            o_ref[...] += part


def _moe_call(xs, block_e, n_used, w13, w2, rows):
    n_rows, d = xs.shape
    n_blocks = n_rows // rows
    ff = w2.shape[1]
    tf = 512
    nf = ff // tf

    def bi(i, nu):
        return jnp.minimum(i, nu[0] - 1)

    def fi(i, f, nu):
        return jnp.where(i < nu[0], f, nf - 1)

    grid_spec = pltpu.PrefetchScalarGridSpec(
        num_scalar_prefetch=2,
        grid=(n_blocks, nf),
        in_specs=[
            pl.BlockSpec((rows, d), lambda i, f, be, nu: (bi(i, nu), 0)),
            pl.BlockSpec((1, d, tf), lambda i, f, be, nu: (be[i], 0, fi(i, f, nu))),
            pl.BlockSpec((1, d, tf), lambda i, f, be, nu: (be[i], 0, nf + fi(i, f, nu))),
            pl.BlockSpec((1, tf, d), lambda i, f, be, nu: (be[i], fi(i, f, nu), 0)),
        ],
        out_specs=pl.BlockSpec((rows, d), lambda i, f, be, nu: (i, 0)),
        scratch_shapes=[pltpu.VMEM((rows, d), BF16)],
    )
    return pl.pallas_call(
        _moe_kernel,
        grid_spec=grid_spec,
        out_shape=jax.ShapeDtypeStruct((n_rows, d), F32),
        compiler_params=_cparams(("arbitrary", "arbitrary")),
        name="moe",
    )(block_e, n_used, xs, w13, w13, w2)


def _route_plan(route, rows):
    t = route.shape[0]
    flat_e = route[:, :2].astype(jnp.int32).reshape(-1)
    n_slots = 2 * t
    experts = jnp.arange(N_EXPERTS, dtype=jnp.int32)
    onehot = (flat_e[:, None] == experts[None, :]).astype(jnp.int32)
    csum = jnp.cumsum(onehot, axis=0)
    rank = jnp.sum(csum * onehot, axis=1) - 1
    counts = csum[-1]
    padded = (counts + rows - 1) // rows * rows
    pad_end = jnp.cumsum(padded)
    pad_start = pad_end - padded
    dest = jnp.sum(onehot * pad_start[None, :], axis=1) + rank
    n_blocks = -(-n_slots // rows) + N_EXPERTS
    block_start = jnp.arange(n_blocks, dtype=jnp.int32) * rows
    block_e = jnp.minimum(jnp.sum((block_start[:, None] >= pad_end[None, :]).astype(jnp.int32), axis=1),
                          N_EXPERTS - 1)
    n_used = (pad_end[-1] // rows).astype(jnp.int32).reshape(1)
    n_rows = n_blocks * rows
    gap_len = jnp.concatenate([padded - counts, (n_rows - pad_end[-1])[None]])
    gap_start = jnp.concatenate([pad_start + counts, pad_end[-1:]])
    gap_end = jnp.cumsum(gap_len)
    j = jnp.arange(n_rows - n_slots, dtype=jnp.int32)
    gap = jnp.sum((j[:, None] >= gap_end[None, :]).astype(jnp.int32), axis=1)
    gap_hot = (gap[:, None] == jnp.arange(N_EXPERTS + 1, dtype=jnp.int32)[None, :]).astype(jnp.int32)
    pad_rows = j + jnp.sum(gap_hot * (gap_start - (gap_end - gap_len))[None, :], axis=1)
    return dest.astype(jnp.int32), pad_rows.astype(jnp.int32), block_e.astype(jnp.int32), n_used, n_blocks


def _final_kernel(d0_ref, d1_ref, x_ref, rt_ref, g2_ref, fw_ref, ys_hbm, o_ref, buf, sem, *, tm):
    i = pl.program_id(0)
    n = pl.num_programs(0)
    slot = i % 2

    def issue(d_ref, s):
        def body(r, carry):
            for k in range(2):
                pltpu.make_async_copy(ys_hbm.at[pl.ds(d_ref[0, 0, 2 * r + k], 1)], buf.at[s, k, pl.ds(r, 1)],
                                      sem.at[s]).start()
            return carry

        lax.fori_loop(0, tm, body, 0, unroll=4)

    @pl.when(i == 0)
    def _():
        issue(d0_ref, 0)

    @pl.when(i + 1 < n)
    def _():
        issue(d1_ref, 1 - slot)

    for k in range(2):
        pltpu.make_async_copy(ys_hbm.at[pl.ds(0, tm)], buf.at[slot, k], sem.at[slot]).wait()
    rt = rt_ref[...]
    y = rt[:, 2:3] * buf[slot, 0] + rt[:, 3:4] * buf[slot, 1]
    xn = x_ref[...] + g2_ref[0] * y
    o_ref[...] = _rms(xn) * fw_ref[...]


def _final_call(x2, ys, dest3, route2, mod3, n_per_seq, fw, tm):
    t, d = x2.shape
    nt = t // tm
    per = n_per_seq // tm
    return pl.pallas_call(
        functools.partial(_final_kernel, tm=tm),
        grid=(nt,),
        in_specs=[
            pl.BlockSpec((1, 1, 2 * tm), lambda i: (i, 0, 0), memory_space=pltpu.SMEM),
            pl.BlockSpec((1, 1, 2 * tm), lambda i: (jnp.minimum(i + 1, nt - 1), 0, 0), memory_space=pltpu.SMEM),
            pl.BlockSpec((tm, d), lambda i: (i, 0)),
            pl.BlockSpec((tm, LANES), lambda i: (i, 0)),
            pl.BlockSpec((1, 1, d), lambda i: (i // per, 0, 5)),
            pl.BlockSpec((1, d), lambda i: (0, 0)),
            pl.BlockSpec(memory_space=pl.ANY),
        ],
        out_specs=pl.BlockSpec((tm, d), lambda i: (i, 0)),
        out_shape=jax.ShapeDtypeStruct((t, d), F32),
        scratch_shapes=[pltpu.VMEM((2, 2, tm, d), F32), pltpu.SemaphoreType.DMA((2,))],
        compiler_params=_row_dma_params(),
        name="combine_final",
    )(dest3, dest3, x2, route2, mod3, fw, ys)


def _norm_kernel(x_ref, fw_ref, o_ref):
    o_ref[0] = _rms(x_ref[0]) * fw_ref[...]


def _norm_call(x, fw):
    bsz, n, d = x.shape
    tm = min(512, n)
    tok = lambda b, i: (b, i, 0)
    return pl.pallas_call(
        _norm_kernel,
        grid=(bsz, n // tm),
        in_specs=[pl.BlockSpec((1, tm, d), tok), pl.BlockSpec((1, d), lambda b, i: (0, 0))],
        out_specs=pl.BlockSpec((1, tm, d), tok),
        out_shape=jax.ShapeDtypeStruct((bsz, n, d), F32),
        compiler_params=_cparams(("parallel", "parallel")),
        name="final_norm",
    )(x, fw)


def _rope_perm():
    perm = np.zeros((LANES,), np.int32)
    half = ROPE_AXIS_DIM // 2
    for i in range(LANES // 2):
        comp, axis, j = i // 32, (i % 32) // half, i % half
        perm[i] = comp * DA_HEAD_DIM + axis * ROPE_AXIS_DIM + j
        perm[LANES // 2 + i] = perm[i] + half
    return perm


def _rope_tables(n):
    rows = n // GRID_W
    pos_r = jnp.repeat(jnp.arange(rows, dtype=F32), GRID_W)
    pos_c = jnp.broadcast_to(jnp.arange(GRID_W, dtype=F32), (rows, GRID_W)).reshape(-1)
    inv_freq = ROPE_BASE ** (-jnp.arange(0, ROPE_AXIS_DIM, 2, dtype=F32) / ROPE_AXIS_DIM)
    ang = jnp.stack([pos_r[:, None] * inv_freq, pos_c[:, None] * inv_freq], axis=1)
    cos, sin = jnp.cos(ang), jnp.sin(ang)
    half = jnp.concatenate([cos.reshape(n, 32), cos.reshape(n, 32)], axis=1)
    cos_t = jnp.concatenate([half, half], axis=1)
    sh = jnp.concatenate([sin.reshape(n, 32), sin.reshape(n, 32)], axis=1)
    sin_t = jnp.concatenate([-sh, sh], axis=1)
    return cos_t, sin_t


def _blockdiag_gate(wa, wx):
    per = HALF // (RG_W // RG_BLOCKS)
    eye = jnp.eye(per, dtype=wa.dtype)

    def bd(w):
        w4 = w.reshape(2, RG_BLOCKS // per, per, w.shape[-2], w.shape[-1])
        return jnp.einsum("dhbij,bc->dhbicj", w4, eye).reshape(2, RG_BLOCKS // per, HALF, HALF)

    return jnp.concatenate([bd(wa), bd(wx)], axis=-1).astype(BF16)


def kernel(x, c, ctx, c_ctx, mod_w, mod_b, norm1_w, norm2_w, w_in, conv_a_w, rg_conv_w, rg_conv_b, rg_wa, rg_ba, rg_wx, rg_bx, rg_lambda, da_lambda, da_subln_w, w_branch, w_out, ffn_w13, ffn_w2, router_w, moe_w13, moe_w2, final_norm_w):
    bsz, n, d = x.shape
    nc = ctx.shape[1]
    depth = mod_w.shape[0]
    cos_t, sin_t = _rope_tables(n)
    perm = _rope_perm()
    ref_cols = {name: np.arange(start, start + width, dtype=np.int32) for name, start, width in REF_COLS}
    for name in ("q", "k"):
        for h in range(DA_HEADS):
            base = ref_cols[name][0] + h * LANES
            ref_cols[name][h * LANES:(h + 1) * LANES] = base + perm
    z_cols = np.concatenate([ref_cols[name] for name in Z_ORDER])

    bp = -(-(bsz + 1) // 8) * 8
    cc = jnp.zeros((bp, d), F32).at[:bsz].set(c).at[bsz].set(c_ctx)
    lat_row = lambda b: b
    ctx_row = lambda b: bsz
    cx = ctx
    moe_rows = 1024 if bsz * n >= 8192 else 128

    for l in range(depth):
        last = l == depth - 1
        lam_init = 0.8 - 0.6 * math.exp(-0.3 * l)
        j = l // 2
        mod, lam = _mod_call(cc, mod_w[l], mod_b[l][None, :], da_lambda[l], lam_init)
        mod3 = mod.reshape(bp, 1, N_MOD * d)
        w_in_l = w_in[l][:, z_cols].astype(BF16)
        nw1 = norm1_w[l][None, :]
        nw2 = norm2_w[l][None, :]
        p = dict(
            conv_a_w=conv_a_w[l], rg_conv_w=rg_conv_w[l], rg_conv_b=rg_conv_b[l][:, None, :],
            rg_wg=_blockdiag_gate(rg_wa[l], rg_wx[l]),
            rg_ba=rg_ba[l].reshape(2, 1, RG_W), rg_bx=rg_bx[l].reshape(2, 1, RG_W),
            rg_lambda=rg_lambda[l][:, None, :])
        subw = da_subln_w[l][None, :]
        wb = w_branch[l].astype(BF16)
        wo = w_out[l].astype(BF16)

        ncol = w_in.shape[2] // COL
        z = _in_call(x, nw1, mod3, lat_row, w_in_l, range(ncol), True, cos_t, sin_t)
        if last:
            zc = _in_call(cx, nw1, mod3, ctx_row, w_in_l, CTX_LAST_BLOCKS, False)
            zc_off = Z_CTX_LAST
        else:
            zc = _in_call(cx, nw1, mod3, ctx_row, w_in_l, range(ncol), False)
            zc_off = Z_FULL

        seq_out = _seq_call(z, zc, zc_off, p, not last)
        yd = _attn_call(z, z, zc, zc_off, lam, subw, lam_init, True)
        moe_layer = l % 2 == 1
        if moe_layer:
            rw = jnp.zeros((d, LANES), F32).at[:, :N_EXPERTS].set(router_w[j])
            xn, h, route = _merge_call(seq_out[0], seq_out[1], yd, z, x, mod3, lat_row, nw2, wb, wo, rw)
            t = bsz * n
            tm = min(512, n)
            route2 = route.reshape(t, LANES)
            dest, pad_rows, block_e, n_used, n_blocks = _route_plan(route2, moe_rows)
            dest3 = dest.reshape(t // tm, 1, 2 * tm)
            pad3 = pad_rows.reshape(t // tm, 1, -1)
            xs = _dispatch_call(h.reshape(t, d), dest3, pad3, n_blocks * moe_rows, tm)
            ys = _moe_call(xs, block_e, n_used, moe_w13[j].astype(BF16), moe_w2[j].astype(BF16), moe_rows)
            if last:
                out = _final_call(xn.reshape(t, d), ys, dest3, route2, mod3, n, final_norm_w[None, :], tm)
                return out.reshape(bsz, n, d)
            raise NotImplementedError("expert layer followed by another layer")
        xn, h = _merge_call(seq_out[0], seq_out[1], yd, z, x, mod3, lat_row, nw2, wb, wo)
        w13 = ffn_w13[j].astype(BF16)
        w2 = ffn_w2[j].astype(BF16)
        x = _ffn_call(h, xn, mod3, lat_row, w13, w2)
        if not last:
            ycd = _attn_call(zc, None, zc, zc_off, lam, subw, lam_init, False)
            cxn, hc = _merge_call(seq_out[2], seq_out[3], ycd, zc, cx, mod3, ctx_row, nw2, wb, wo)
            cx = _ffn_call(hc, cxn, mod3, ctx_row, w13, w2)
    return _norm_call(x, final_norm_w[None, :])
```

```python
import functools
import math

import jax
import jax.numpy as jnp
import numpy as np
from jax import lax
from jax.experimental import pallas as pl
from jax.experimental.pallas import tpu as pltpu

F32 = jnp.float32
BF16 = jnp.bfloat16

EPS = 1e-6
N_MOD = 6
GRID_W = 64
CONV_W = 512
RG_W = 512
RG_BLOCKS = 8
RG_C = 8.0
DA_HEADS = 4
DA_HEAD_DIM = 64
DA_V_DIM = 128
ROPE_AXIS_DIM = 32
ROPE_BASE = 10000.0
N_EXPERTS = 8
LANES = 128
HALF = 256
VMEM_LIMIT = 56 * 1024 * 1024
SWIGLU_ROWS = 512
SWIGLU_FF_TILE = 1792

COL = 1024
REF_COLS = (("a_b", 0, 512), ("a_c", 512, 512), ("a_x", 1024, 512), ("r_g", 1536, 512), ("r_x", 2048, 512),
            ("q", 2560, 512), ("k", 3072, 512), ("v", 3584, 512), ("g_a", 4096, 1024), ("g_r", 5120, 1024),
            ("g_d", 6144, 1024))
Z_ORDER = ("q", "k", "v", "a_b", "a_c", "a_x", "r_g", "r_x", "g_a", "g_r", "g_d")
Z_FULL = dict(q=0, k=4, v=8, a_b=12, a_c=16, a_x=20, r_g=24, r_x=28, g_a=32, g_r=40, g_d=48)
CTX_LAST_BLOCKS = (0, 1, 3)
Z_CTX_LAST = dict(k=4, v=8, r_x=20)
LOG2E = 1.4426950408889634


def _cparams(sem, vmem=VMEM_LIMIT):
    return pltpu.CompilerParams(dimension_semantics=sem, vmem_limit_bytes=vmem)


def _sigmoid(x):
    return 1.0 / (1.0 + jnp.exp(-x))


def _silu(x):
    return x * _sigmoid(x)


def _gelu_tanh(x):
    return 0.5 * x * (1.0 + jnp.tanh(math.sqrt(2.0 / math.pi) * (x + 0.044715 * (x * x * x))))


def _rms(x):
    return x * lax.rsqrt(jnp.mean(x * x, axis=-1, keepdims=True) + EPS)


def _mod_kernel(cc_ref, w_ref, b_ref, dl_ref, mod_ref, lam_ref, *, lam_init):
    cc = cc_ref[...]
    s = _silu(cc)
    mod_ref[...] = jnp.dot(s, w_ref[...], preferred_element_type=F32,
                           precision=lax.Precision.HIGHEST) + b_ref[...]
    dl = dl_ref[...]
    s1 = jnp.sum(dl[0:1] * dl[1:2], axis=-1, keepdims=True)
    s2 = jnp.sum(dl[2:3] * dl[3:4], axis=-1, keepdims=True)
    lam = jnp.exp(s1) - jnp.exp(s2) + lam_init
    lam_ref[...] = jnp.broadcast_to(lam, lam_ref.shape)


def _mod_call(cc, w, b, dl, lam_init):
    bp, d = cc.shape
    nd = w.shape[1]
    tn = 1024
    return pl.pallas_call(
        functools.partial(_mod_kernel, lam_init=lam_init),
        grid=(nd // tn,),
        in_specs=[
            pl.BlockSpec((bp, d), lambda j: (0, 0)),
            pl.BlockSpec((d, tn), lambda j: (0, j)),
            pl.BlockSpec((1, tn), lambda j: (0, j)),
            pl.BlockSpec(dl.shape, lambda j: (0, 0)),
        ],
        out_specs=[
            pl.BlockSpec((bp, tn), lambda j: (0, j)),
            pl.BlockSpec((8, LANES), lambda j: (0, 0)),
        ],
        out_shape=[jax.ShapeDtypeStruct((bp, nd), F32), jax.ShapeDtypeStruct((8, LANES), F32)],
        compiler_params=_cparams(("arbitrary",)),
        name="mod",
    )(cc, w, b, dl)


def _in_kernel(*refs, rope):
    if rope:
        x_ref, nw_ref, sh_ref, sc_ref, cos_ref, sin_ref, w_ref, o_ref, u_scr = refs
    else:
        x_ref, nw_ref, sh_ref, sc_ref, w_ref, o_ref, u_scr = refs
    j = pl.program_id(2)

    @pl.when(j == 0)
    def _():
        y = _rms(x_ref[0]) * nw_ref[...]
        u_scr[...] = (y * (1.0 + sc_ref[0]) + sh_ref[0]).astype(BF16)

    acc = jnp.dot(u_scr[...], w_ref[...], preferred_element_type=F32)
    q_scale = DA_HEAD_DIM ** -0.5 * LOG2E
    n_q = DA_HEADS * 2 * DA_HEAD_DIM // LANES

    @pl.when(j == 0)
    def _():
        for g in range(acc.shape[1] // LANES):
            r = acc[:, g * LANES:(g + 1) * LANES]
            if rope:
                r = r * cos_ref[...] + pltpu.roll(r, LANES // 2, 1) * sin_ref[...]
            if g < n_q:
                r = r * q_scale
            o_ref[0, :, g * LANES:(g + 1) * LANES] = r.astype(o_ref.dtype)

    @pl.when(j != 0)
    def _():
        o_ref[0] = acc.astype(o_ref.dtype)


def _in_call(x, nw, mod3, mod_row, w, col_blocks, rope, cos_t=None, sin_t=None):
    bsz, n, d = x.shape
    tm = min(1024, n)
    nj = len(col_blocks)
    cb = tuple(col_blocks)
    if cb == tuple(range(nj)):
        wmap = lambda b, i, j: (0, j)
    else:
        assert cb == CTX_LAST_BLOCKS
        wmap = lambda b, i, j: (0, j + j // 2)
    in_specs = [
        pl.BlockSpec((1, tm, d), lambda b, i, j: (b, i, 0)),
        pl.BlockSpec((1, d), lambda b, i, j: (0, 0)),
        pl.BlockSpec((1, 1, d), lambda b, i, j: (mod_row(b), 0, 0)),
        pl.BlockSpec((1, 1, d), lambda b, i, j: (mod_row(b), 0, 1)),
    ]
    args = [x, nw, mod3, mod3]
    if rope:
        in_specs += [pl.BlockSpec((tm, LANES), lambda b, i, j: (i, 0)),
                     pl.BlockSpec((tm, LANES), lambda b, i, j: (i, 0))]
        args += [cos_t, sin_t]
    in_specs.append(pl.BlockSpec((d, COL), wmap))
    args.append(w)
    return pl.pallas_call(
        functools.partial(_in_kernel, rope=rope),
        grid=(bsz, n // tm, nj),
        in_specs=in_specs,
        out_specs=pl.BlockSpec((1, tm, COL), lambda b, i, j: (b, i, j)),
        out_shape=jax.ShapeDtypeStruct((bsz, n, nj * COL), BF16),
        scratch_shapes=[pltpu.VMEM((tm, d), BF16)],
        compiler_params=_cparams(("parallel", "parallel", "arbitrary")),
        name="in_proj",
    )(*args)


SEQ_TILE = 64
PAD = 8
SCAN_ROWS = 64


def _seq_kernel(*refs, n, nc, need_ctx_out):
    it = iter(refs)
    ab_ref, ac_ref, ax_ref, rg_ref, rx_ref = (next(it) for _ in range(5))
    if need_ctx_out:
        cab_ref, cac_ref, cax_ref, crg_ref = (next(it) for _ in range(4))
    crx_ref = next(it)
    wa_ref, cw_ref, cbias_ref, wg_ref, ba_ref, bx_ref, lam_ref = (next(it) for _ in range(7))
    ya_ref, yr_ref = next(it), next(it)
    if need_ctx_out:
        yca_ref, ycr_ref = next(it), next(it)
    xp, a_scr, b_scr = next(it), next(it), next(it)
    T = SEQ_TILE
    W = HALF

    def stage(val, ns):
        xp[0:PAD, :] = jnp.zeros((PAD, W), F32)
        xp[PAD:PAD + ns, :] = val
        xp[PAD + ns:2 * PAD + ns, :] = jnp.zeros((PAD, W), F32)

    def conv_mixer(b_ref, c_ref, x_ref, out_ref, ns):
        stage(c_ref[0].astype(F32) * x_ref[0].astype(F32), ns)
        w0, w1, w2 = wa_ref[0:1, :], wa_ref[1:2, :], wa_ref[2:3, :]

        def body(ti, carry):
            t0 = pl.multiple_of(ti * T, T)
            ps = xp[pl.ds(t0, T + 2 * PAD), :]
            conv = w0 * ps[PAD - 1:PAD - 1 + T] + w1 * ps[PAD:PAD + T] + w2 * ps[PAD + 1:PAD + 1 + T]
            gate = b_ref[0, pl.ds(t0, T), :].astype(F32)
            out_ref[0, pl.ds(t0, T), :] = (gate * conv).astype(out_ref.dtype)
            return carry

        lax.fori_loop(0, ns // T, body, 0)

    nl = -lam_ref[...]
    sp = jnp.maximum(nl, 0.0) + jnp.log1p(jnp.exp(-jnp.abs(nl)))
    NG = W // LANES

    def rglru(x_ref, ns, h0):
        stage(x_ref[0].astype(F32), ns)

        def gates_body(ti, carry):
            t0 = pl.multiple_of(ti * T, T)
            xs = xp[pl.ds(t0, T + 2 * PAD), :]
            for d in range(2):
                off = PAD - 3 if d == 0 else PAD
                xc = cbias_ref[d]
                for k in range(4):
                    xc = xc + cw_ref[d, k:k + 1, :] * xs[off + k:off + k + T]
                g = jnp.dot(xc.astype(BF16), wg_ref[d, 0], preferred_element_type=F32)
                r = _sigmoid(g[:, :W] + ba_ref[d])
                i = _sigmoid(g[:, W:] + bx_ref[d])
                log_a = (-RG_C * r) * sp[d]
                a = jnp.exp(log_a)
                bb = jnp.sqrt(1.0 - a * a) * (i * xc)
                for g in range(NG):
                    a_scr[d, g, pl.ds(t0, T), :] = a[:, g * LANES:(g + 1) * LANES]
                    b_scr[d, g, pl.ds(t0, T), :] = bb[:, g * LANES:(g + 1) * LANES]
            return carry

        lax.fori_loop(0, ns // T, gates_body, 0, unroll=2)
        R = SCAN_ROWS
        nch = R // 8
        steps = ns // R
        sub = lax.broadcasted_iota(jnp.int32, (nch, 8, LANES), 1)

        def scan_body(ci, carry):
            out = []
            for d in range(2):
                r0 = pl.multiple_of((ci if d == 0 else steps - 1 - ci) * R, R)
                edge = 7 if d == 0 else 0
                for g in range(NG):
                    a = a_scr[d, g, pl.ds(r0, R), :].reshape(nch, 8, LANES)
                    b = b_scr[d, g, pl.ds(r0, R), :].reshape(nch, 8, LANES)
                    for s in (1, 2, 4):
                        sh = s if d == 0 else 8 - s
                        m = (sub >= s) if d == 0 else (sub < 8 - s)
                        b = b + a * jnp.where(m, pltpu.roll(b, sh, 1), 0.0)
                        a = a * jnp.where(m, pltpu.roll(a, sh, 1), 1.0)
                    c = carry[d * NG + g]
                    hs = [None] * nch
                    for j in (range(nch) if d == 0 else range(nch - 1, -1, -1)):
                        hs[j] = b[j] + a[j] * c
                        c = (jnp.broadcast_to(a[j, edge:edge + 1, :], (8, LANES)) * c
                             + jnp.broadcast_to(b[j, edge:edge + 1, :], (8, LANES)))
                    b_scr[d, g, pl.ds(r0, R), :] = jnp.concatenate(hs, axis=0)
                    out.append(c)
            return tuple(out)

        init = tuple(jnp.broadcast_to(h0[d][g], (8, LANES)) for d in range(2) for g in range(NG))
        ends = lax.fori_loop(0, steps, scan_body, init)
        return [[ends[d * NG + g][0:1, :] for g in range(NG)] for d in range(2)]

    def rg_out(g_ref, out_ref, ns):
        def body(ti, carry):
            t0 = pl.multiple_of(ti * T, T)
            for g in range(NG):
                hs = b_scr[0, g, pl.ds(t0, T), :] + b_scr[1, g, pl.ds(t0, T), :]
                gate = g_ref[0, pl.ds(t0, T), g * LANES:(g + 1) * LANES].astype(F32)
                out_ref[0, pl.ds(t0, T), g * LANES:(g + 1) * LANES] = (_gelu_tanh(gate) * hs).astype(out_ref.dtype)
            return carry

        lax.fori_loop(0, ns // T, body, 0)

    zero = [[jnp.zeros((1, LANES), F32)] * NG] * 2
    h_ctx = rglru(crx_ref, nc, zero)
    if need_ctx_out:
        rg_out(crg_ref, ycr_ref, nc)
        conv_mixer(cab_ref, cac_ref, cax_ref, yca_ref, nc)
    rglru(rx_ref, n, h_ctx)
    rg_out(rg_ref, yr_ref, n)
    conv_mixer(ab_ref, ac_ref, ax_ref, ya_ref, n)


def _seq_call(z, zc, zc_off, p, need_ctx_out):
    bsz, n, _ = z.shape
    nc = zc.shape[1]
    u = HALF // LANES

    def zspec(rows, off):
        return pl.BlockSpec((1, rows, HALF), lambda b, c: (b, 0, off // u + c))

    names = ("a_b", "a_c", "a_x", "r_g", "r_x")
    in_specs = [zspec(n, Z_FULL[k]) for k in names]
    args = [z] * 5
    if need_ctx_out:
        in_specs += [zspec(nc, zc_off[k]) for k in names[:4]]
        args += [zc] * 4
    in_specs.append(zspec(nc, zc_off["r_x"]))
    args.append(zc)
    in_specs += [
        pl.BlockSpec((3, HALF), lambda b, c: (0, c)),
        pl.BlockSpec((2, 4, HALF), lambda b, c: (0, 0, c)),
        pl.BlockSpec((2, 1, HALF), lambda b, c: (0, 0, c)),
        pl.BlockSpec((2, 1, HALF, 2 * HALF), lambda b, c: (0, c, 0, 0)),
        pl.BlockSpec((2, 1, HALF), lambda b, c: (0, 0, c)),
        pl.BlockSpec((2, 1, HALF), lambda b, c: (0, 0, c)),
        pl.BlockSpec((2, 1, HALF), lambda b, c: (0, 0, c)),
    ]
    args += [p["conv_a_w"], p["rg_conv_w"], p["rg_conv_b"], p["rg_wg"], p["rg_ba"], p["rg_bx"], p["rg_lambda"]]
    out_specs = [pl.BlockSpec((1, n, HALF), lambda b, c: (b, 0, c))] * 2
    out_shape = [jax.ShapeDtypeStruct((bsz, n, CONV_W), BF16), jax.ShapeDtypeStruct((bsz, n, RG_W), BF16)]
    if need_ctx_out:
        out_specs += [pl.BlockSpec((1, nc, HALF), lambda b, c: (b, 0, c))] * 2
        out_shape += [jax.ShapeDtypeStruct((bsz, nc, CONV_W), BF16), jax.ShapeDtypeStruct((bsz, nc, RG_W), BF16)]
    npad = max(n, nc) + 2 * PAD
    return pl.pallas_call(
        functools.partial(_seq_kernel, n=n, nc=nc, need_ctx_out=need_ctx_out),
        grid=(bsz, CONV_W // HALF),
        in_specs=in_specs,
        out_specs=out_specs,
        out_shape=out_shape,
        scratch_shapes=[pltpu.VMEM((npad, HALF), F32), pltpu.VMEM((2, u, max(n, nc), LANES), F32),
                        pltpu.VMEM((2, u, max(n, nc), LANES), F32)],
        compiler_params=_cparams(("parallel", "parallel")),
        name="seq_mixers",
    )(*args)


def _attn_kernel(*refs, has_lat, lam_init):
    if has_lat:
        q_ref, k_ref, v_ref, ck_ref, cv_ref, lam_ref, sw_ref, o_ref, k_all, v_ext = refs
    else:
        q_ref, ck_ref, cv_ref, lam_ref, sw_ref, o_ref, k_all, v_ext = refs
    nc = ck_ref.shape[1]
    nk = k_all.shape[0]

    @pl.when(pl.program_id(1) == 0)
    def _():
        k_all[0:nc, :] = ck_ref[0]
        if has_lat:
            k_all[nc:nk, :] = k_ref[0]
        for h in range(DA_HEADS):
            sl = slice(h * LANES, (h + 1) * LANES)
            v_ext[h, 0:nc, 0:LANES] = cv_ref[0, :, sl]
            if has_lat:
                v_ext[h, nc:nk, 0:LANES] = v_ref[0, :, sl]
            v_ext[h, :, LANES:2 * LANES] = jnp.ones((nk, LANES), v_ext.dtype)

    lane = lax.broadcasted_iota(jnp.int32, (1, LANES), 1)
    map1 = (lane % DA_HEAD_DIM) < (DA_HEAD_DIM // 2)
    dn = (((1,), (1,)), ((), ()))
    lam = lam_ref[0:1, 0:1]

    for h in range(DA_HEADS):
        sl = slice(h * LANES, (h + 1) * LANES)
        q = q_ref[0, :, sl]
        zero = jnp.zeros_like(q)

        def soft_v(qm):
            s = lax.dot_general(qm, k_all[:, sl], dn, preferred_element_type=F32)
            e = jnp.exp2(s - jnp.max(s, axis=-1, keepdims=True)).astype(BF16)
            r = jnp.dot(e, v_ext[h], preferred_element_type=F32)
            return r[:, :LANES] / r[:, LANES:LANES + 1]

        o = soft_v(jnp.where(map1, q, zero)) - lam * soft_v(jnp.where(map1, zero, q))
        y = _rms(o) * sw_ref[...] * (1.0 - lam_init)
        o_ref[0, :, sl] = y.astype(o_ref.dtype)


def _attn_call(zq, z, zc, zc_off, lam, subw, lam_init, has_lat):
    bsz, nq, _ = zq.shape
    nc = zc.shape[1]
    tq = min(256, nq)
    w = DA_HEADS * LANES
    u = DA_HEADS
    in_specs = [pl.BlockSpec((1, tq, w), lambda b, i: (b, i, Z_FULL["q"] // u))]
    args = [zq]
    nk = nc
    if has_lat:
        n = z.shape[1]
        nk = nc + n
        in_specs += [pl.BlockSpec((1, n, w), lambda b, i: (b, 0, Z_FULL["k"] // u)),
                     pl.BlockSpec((1, n, w), lambda b, i: (b, 0, Z_FULL["v"] // u))]
        args += [z, z]
    in_specs += [pl.BlockSpec((1, nc, w), lambda b, i: (b, 0, zc_off["k"] // u)),
                 pl.BlockSpec((1, nc, w), lambda b, i: (b, 0, zc_off["v"] // u)),
                 pl.BlockSpec((8, LANES), lambda b, i: (0, 0)),
                 pl.BlockSpec((1, LANES), lambda b, i: (0, 0))]
    args += [zc, zc, lam, subw]
    return pl.pallas_call(
        functools.partial(_attn_kernel, has_lat=has_lat, lam_init=lam_init),
        grid=(bsz, nq // tq),
        in_specs=in_specs,
        out_specs=pl.BlockSpec((1, tq, w), lambda b, i: (b, i, 0)),
        out_shape=jax.ShapeDtypeStruct((bsz, nq, DA_HEADS * DA_V_DIM), BF16),
        scratch_shapes=[pltpu.VMEM((nk, w), BF16), pltpu.VMEM((DA_HEADS, nk, 2 * LANES), BF16)],
        compiler_params=_cparams(("parallel", "arbitrary")),
        name="diff_attn",
    )(*args)


def _merge_kernel(*refs, route):
    if route:
        (ya_ref, yr_ref, yd_ref, ga_ref, gr_ref, gd_ref, x_ref, g1_ref, nw_ref, sh_ref, sc_ref,
         wb_ref, wo_ref, rw_ref, xo_ref, h_ref, rt_ref) = refs
    else:
        (ya_ref, yr_ref, yd_ref, ga_ref, gr_ref, gd_ref, x_ref, g1_ref, nw_ref, sh_ref, sc_ref,
         wb_ref, wo_ref, xo_ref, h_ref) = refs
    m = None
    for i, (y_ref, g_ref) in enumerate(((ya_ref, ga_ref), (yr_ref, gr_ref), (yd_ref, gd_ref))):
        t = _sigmoid(g_ref[0].astype(F32)) * jnp.dot(y_ref[0], wb_ref[i], preferred_element_type=F32)
        m = t if m is None else m + t
    out = jnp.dot(m.astype(BF16), wo_ref[...], preferred_element_type=F32)
    xn = x_ref[0] + g1_ref[0] * out
    xo_ref[0] = xn
    h = (_rms(xn) * nw_ref[...]) * (1.0 + sc_ref[0]) + sh_ref[0]
    h_ref[0] = h.astype(h_ref.dtype)
    if route:
        h_hi = h.astype(BF16)
        h_lo = (h - h_hi.astype(F32)).astype(BF16)
        logits = (jnp.dot(h_hi, rw_ref[0], preferred_element_type=F32)
                  + jnp.dot(h_lo, rw_ref[0], preferred_element_type=F32)
                  + jnp.dot(h_hi, rw_ref[1], preferred_element_type=F32))
        lane = lax.broadcasted_iota(jnp.int32, logits.shape, 1)
        neg = jnp.float32(-jnp.inf)
        l1 = jnp.where(lane < N_EXPERTS, logits, neg)
        m1 = jnp.max(l1, axis=-1, keepdims=True)
        i1 = jnp.min(jnp.where(l1 == m1, lane, LANES), axis=-1, keepdims=True)
        l2 = jnp.where(lane == i1, neg, l1)
        m2 = jnp.max(l2, axis=-1, keepdims=True)
        i2 = jnp.min(jnp.where(l2 == m2, lane, LANES), axis=-1, keepdims=True)
        ex = jnp.exp(m2 - m1)
        gt1 = 1.0 / (1.0 + ex)
        gt2 = ex * gt1
        rt = jnp.where(lane == 0, i1.astype(F32),
                       jnp.where(lane == 1, i2.astype(F32),
                                 jnp.where(lane == 2, gt1, jnp.where(lane == 3, gt2, 0.0))))
        rt_ref[0] = rt


def _merge_call(ya, yr, yd, z, x, mod3, mod_row, nw2, wb, wo, router_w=None):
    bsz, n, d = x.shape
    tm = min(512, n)
    route = router_w is not None
    tok = lambda b, i: (b, i, 0)
    gu = d // LANES
    in_specs = [
        pl.BlockSpec((1, tm, CONV_W), tok), pl.BlockSpec((1, tm, RG_W), tok),
        pl.BlockSpec((1, tm, DA_HEADS * DA_V_DIM), tok),
        pl.BlockSpec((1, tm, d), lambda b, i: (b, i, Z_FULL["g_a"] // gu)),
        pl.BlockSpec((1, tm, d), lambda b, i: (b, i, Z_FULL["g_r"] // gu)),
        pl.BlockSpec((1, tm, d), lambda b, i: (b, i, Z_FULL["g_d"] // gu)),
        pl.BlockSpec((1, tm, d), tok),
        pl.BlockSpec((1, 1, d), lambda b, i: (mod_row(b), 0, 2)),
        pl.BlockSpec((1, d), lambda b, i: (0, 0)),
        pl.BlockSpec((1, 1, d), lambda b, i: (mod_row(b), 0, 3)),
        pl.BlockSpec((1, 1, d), lambda b, i: (mod_row(b), 0, 4)),
        pl.BlockSpec(wb.shape, lambda b, i: (0, 0, 0)),
        pl.BlockSpec(wo.shape, lambda b, i: (0, 0)),
    ]
    args = [ya, yr, yd, z, z, z, x, mod3, nw2, mod3, mod3, wb, wo]
    out_specs = [pl.BlockSpec((1, tm, d), tok), pl.BlockSpec((1, tm, d), tok)]
    out_shape = [jax.ShapeDtypeStruct((bsz, n, d), F32),
                 jax.ShapeDtypeStruct((bsz, n, d), F32 if route else BF16)]
    if route:
        in_specs.append(pl.BlockSpec(router_w.shape, lambda b, i: (0, 0, 0)))
        args.append(router_w)
        out_specs.append(pl.BlockSpec((1, tm, LANES), tok))
        out_shape.append(jax.ShapeDtypeStruct((bsz, n, LANES), F32))
    return pl.pallas_call(
        functools.partial(_merge_kernel, route=route),
        grid=(bsz, n // tm),
        in_specs=in_specs,
        out_specs=out_specs,
        out_shape=out_shape,
        compiler_params=_cparams(("parallel", "parallel")),
        name="merge",
    )(*args)


def _ffn_kernel(h_ref, x_ref, g2_ref, wg_ref, wu_ref, w2_ref, o_ref, acc_ref):
    f = pl.program_id(2)
    h = h_ref[0]
    g = jnp.dot(h, wg_ref[...], preferred_element_type=F32)
    u = jnp.dot(h, wu_ref[...], preferred_element_type=F32)
    act = (_silu(g) * u).astype(BF16)
    part = jnp.dot(act, w2_ref[...], preferred_element_type=F32)

    @pl.when(f == 0)
    def _():
        acc_ref[...] = part

    @pl.when(f > 0)
    def _():
        acc_ref[...] += part

    @pl.when(f == pl.num_programs(2) - 1)
    def _():
        o_ref[0] = x_ref[0] + g2_ref[0] * acc_ref[...]


def _ffn_call(h, x, mod3, mod_row, w13, w2):
    bsz, n, d = x.shape
    ff = w2.shape[0]
    tm = min(SWIGLU_ROWS, n)
    tf = SWIGLU_FF_TILE
    nf = ff // tf
    tok = lambda b, i, f: (b, i, 0)
    return pl.pallas_call(
        _ffn_kernel,
        grid=(bsz, n // tm, nf),
        in_specs=[
            pl.BlockSpec((1, tm, d), tok),
            pl.BlockSpec((1, tm, d), tok),
            pl.BlockSpec((1, 1, d), lambda b, i, f: (mod_row(b), 0, 5)),
            pl.BlockSpec((d, tf), lambda b, i, f: (0, f)),
            pl.BlockSpec((d, tf), lambda b, i, f: (0, nf + f)),
            pl.BlockSpec((tf, d), lambda b, i, f: (f, 0)),
        ],
        out_specs=pl.BlockSpec((1, tm, d), tok),
        out_shape=jax.ShapeDtypeStruct((bsz, n, d), F32),
        scratch_shapes=[pltpu.VMEM((tm, d), F32)],
        compiler_params=_cparams(("parallel", "parallel", "arbitrary")),
        name="ffn",
    )(h, x, mod3, w13, w13, w2)


def _row_dma_params():
    return pltpu.CompilerParams(dimension_semantics=("arbitrary",), vmem_limit_bytes=VMEM_LIMIT,
                                disable_bounds_checks=True)


def _dispatch_kernel(dest_ref, pad_ref, h_ref, xs_hbm, scr, zrow, sem, *, tm, n_pad):
    i = pl.program_id(0)
    last = pl.num_programs(0) - 1
    slot = i % 2
    scr[slot] = h_ref[...]

    @pl.when(i == 0)
    def _():
        zrow[...] = jnp.zeros(zrow.shape, zrow.dtype)

    def body(r, carry):
        for k in range(2):
            pltpu.make_async_copy(scr.at[slot, pl.ds(r, 1)], xs_hbm.at[pl.ds(dest_ref[0, 0, 2 * r + k], 1)],
                                  sem.at[slot]).start()
        return carry

    lax.fori_loop(0, tm, body, 0, unroll=4)

    def pad_body(r, carry):
        pltpu.make_async_copy(zrow.at[pl.ds(0, 1)], xs_hbm.at[pl.ds(pad_ref[0, 0, r], 1)], sem.at[slot]).start()
        return carry

    lax.fori_loop(0, n_pad, pad_body, 0, unroll=4)

    def wait_all(s):
        left = 2 * tm + n_pad
        while left > 0:
            rows = min(left, tm)
            pltpu.make_async_copy(xs_hbm.at[pl.ds(0, rows)], scr.at[s, pl.ds(0, rows)], sem.at[s]).wait()
            left -= rows

    @pl.when(i > 0)
    def _():
        wait_all(1 - slot)

    @pl.when(i == last)
    def _():
        wait_all(slot)


def _dispatch_call(h2, dest3, pad3, n_rows, tm):
    t, d = h2.shape
    n_pad = pad3.shape[2]
    return pl.pallas_call(
        functools.partial(_dispatch_kernel, tm=tm, n_pad=n_pad),
        grid=(t // tm,),
        in_specs=[pl.BlockSpec((1, 1, 2 * tm), lambda i: (i, 0, 0), memory_space=pltpu.SMEM),
                  pl.BlockSpec((1, 1, n_pad), lambda i: (i, 0, 0), memory_space=pltpu.SMEM),
                  pl.BlockSpec((tm, d), lambda i: (i, 0))],
        out_specs=pl.BlockSpec(memory_space=pl.ANY),
        out_shape=jax.ShapeDtypeStruct((n_rows, d), F32),
        scratch_shapes=[pltpu.VMEM((2, tm, d), F32), pltpu.VMEM((8, d), F32), pltpu.SemaphoreType.DMA((2,))],
        compiler_params=_row_dma_params(),
        name="moe_dispatch",
    )(dest3, pad3, h2)


def _moe_kernel(be_ref, nu_ref, x_ref, wg_ref, wu_ref, w2_ref, o_ref, xb):
    i = pl.program_id(0)
    f = pl.program_id(1)
    used = i < nu_ref[0]

    @pl.when(used & (f == 0))
    def _():
        xb[...] = x_ref[...].astype(BF16)

    @pl.when(jnp.logical_not(used) & (f == 0))
    def _():
        o_ref[...] = jnp.zeros(o_ref.shape, o_ref.dtype)

    @pl.when(used)
    def _():
        x = xb[...]
        g = jnp.dot(x, wg_ref[0], preferred_element_type=F32)
        u = jnp.dot(x, wu_ref[0], preferred_element_type=F32)
        act = (_silu(g) * u).astype(BF16)
        part = jnp.dot(act, w2_ref[0], preferred_element_type=F32)

        @pl.when(f == 0)
        def _():
            o_ref[...] = part

        @pl.when(f > 0)
        def _():
            o_ref[...] += part


def _moe_call(xs, block_e, n_used, w13, w2, rows):
    n_rows, d = xs.shape
    n_blocks = n_rows // rows
    ff = w2.shape[1]
    tf = SWIGLU_FF_TILE
    nf = ff // tf

    def bi(i, nu):
        return jnp.minimum(i, nu[0] - 1)

    def fi(i, f, nu):
        return jnp.where(i < nu[0], f, nf - 1)

    grid_spec = pltpu.PrefetchScalarGridSpec(
        num_scalar_prefetch=2,
        grid=(n_blocks, nf),
        in_specs=[
            pl.BlockSpec((rows, d), lambda i, f, be, nu: (bi(i, nu), 0)),
            pl.BlockSpec((1, d, tf), lambda i, f, be, nu: (be[i], 0, fi(i, f, nu))),
            pl.BlockSpec((1, d, tf), lambda i, f, be, nu: (be[i], 0, nf + fi(i, f, nu))),
            pl.BlockSpec((1, tf, d), lambda i, f, be, nu: (be[i], fi(i, f, nu), 0)),
        ],
        out_specs=pl.BlockSpec((rows, d), lambda i, f, be, nu: (i, 0)),
        scratch_shapes=[pltpu.VMEM((rows, d), BF16)],
    )
    return pl.pallas_call(
        _moe_kernel,
        grid_spec=grid_spec,
        out_shape=jax.ShapeDtypeStruct((n_rows, d), F32),
        compiler_params=_cparams(("arbitrary", "arbitrary")),
        name="moe",
    )(block_e, n_used, xs, w13, w13, w2)


def _route_plan(route, rows):
    t = route.shape[0]
    flat_e = route[:, :2].astype(jnp.int32).reshape(-1)
    n_slots = 2 * t
    experts = jnp.arange(N_EXPERTS, dtype=jnp.int32)
    onehot = (flat_e[:, None] == experts[None, :]).astype(jnp.int32)
    csum = jnp.cumsum(onehot, axis=0)
    rank = jnp.sum(csum * onehot, axis=1) - 1
    counts = csum[-1]
    padded = (counts + rows - 1) // rows * rows
    pad_end = jnp.cumsum(padded)
    pad_start = pad_end - padded
    dest = jnp.sum(onehot * pad_start[None, :], axis=1) + rank
    n_blocks = -(-n_slots // rows) + N_EXPERTS
    block_start = jnp.arange(n_blocks, dtype=jnp.int32) * rows
    block_e = jnp.minimum(jnp.sum((block_start[:, None] >= pad_end[None, :]).astype(jnp.int32), axis=1),
                          N_EXPERTS - 1)
    n_used = (pad_end[-1] // rows).astype(jnp.int32).reshape(1)
    n_rows = n_blocks * rows
    gap_len = jnp.concatenate([padded - counts, (n_rows - pad_end[-1])[None]])
    gap_start = jnp.concatenate([pad_start + counts, pad_end[-1:]])
    gap_end = jnp.cumsum(gap_len)
    j = jnp.arange(n_rows - n_slots, dtype=jnp.int32)
    gap = jnp.sum((j[:, None] >= gap_end[None, :]).astype(jnp.int32), axis=1)
    gap_hot = (gap[:, None] == jnp.arange(N_EXPERTS + 1, dtype=jnp.int32)[None, :]).astype(jnp.int32)
    pad_rows = j + jnp.sum(gap_hot * (gap_start - (gap_end - gap_len))[None, :], axis=1)
    return dest.astype(jnp.int32), pad_rows.astype(jnp.int32), block_e.astype(jnp.int32), n_used, n_blocks


def _final_kernel(d0_ref, d1_ref, x_ref, rt_ref, g2_ref, fw_ref, ys_hbm, o_ref, buf, sem, *, tm):
    i = pl.program_id(0)
    n = pl.num_programs(0)
    slot = i % 2

    def issue(d_ref, s):
        def body(r, carry):
            for k in range(2):
                pltpu.make_async_copy(ys_hbm.at[pl.ds(d_ref[0, 0, 2 * r + k], 1)], buf.at[s, k, pl.ds(r, 1)],
                                      sem.at[s]).start()
            return carry

        lax.fori_loop(0, tm, body, 0, unroll=4)

    @pl.when(i == 0)
    def _():
        issue(d0_ref, 0)

    @pl.when(i + 1 < n)
    def _():
        issue(d1_ref, 1 - slot)

    for k in range(2):
        pltpu.make_async_copy(ys_hbm.at[pl.ds(0, tm)], buf.at[slot, k], sem.at[slot]).wait()
    rt = rt_ref[...]
    y = rt[:, 2:3] * buf[slot, 0] + rt[:, 3:4] * buf[slot, 1]
    xn = x_ref[...] + g2_ref[0] * y
    o_ref[...] = _rms(xn) * fw_ref[...]


def _final_call(x2, ys, dest3, route2, mod3, n_per_seq, fw, tm):
    t, d = x2.shape
    nt = t // tm
    per = n_per_seq // tm
    return pl.pallas_call(
        functools.partial(_final_kernel, tm=tm),
        grid=(nt,),
        in_specs=[
            pl.BlockSpec((1, 1, 2 * tm), lambda i: (i, 0, 0), memory_space=pltpu.SMEM),
            pl.BlockSpec((1, 1, 2 * tm), lambda i: (jnp.minimum(i + 1, nt - 1), 0, 0), memory_space=pltpu.SMEM),
            pl.BlockSpec((tm, d), lambda i: (i, 0)),
            pl.BlockSpec((tm, LANES), lambda i: (i, 0)),
            pl.BlockSpec((1, 1, d), lambda i: (i // per, 0, 5)),
            pl.BlockSpec((1, d), lambda i: (0, 0)),
            pl.BlockSpec(memory_space=pl.ANY),
        ],
        out_specs=pl.BlockSpec((tm, d), lambda i: (i, 0)),
        out_shape=jax.ShapeDtypeStruct((t, d), F32),
        scratch_shapes=[pltpu.VMEM((2, 2, tm, d), F32), pltpu.SemaphoreType.DMA((2,))],
        compiler_params=_row_dma_params(),
        name="combine_final",
    )(dest3, dest3, x2, route2, mod3, fw, ys)


def _norm_kernel(x_ref, fw_ref, o_ref):
    o_ref[0] = _rms(x_ref[0]) * fw_ref[...]


def _norm_call(x, fw):
    bsz, n, d = x.shape
    tm = min(512, n)
    tok = lambda b, i: (b, i, 0)
    return pl.pallas_call(
        _norm_kernel,
        grid=(bsz, n // tm),
        in_specs=[pl.BlockSpec((1, tm, d), tok), pl.BlockSpec((1, d), lambda b, i: (0, 0))],
        out_specs=pl.BlockSpec((1, tm, d), tok),
        out_shape=jax.ShapeDtypeStruct((bsz, n, d), F32),
        compiler_params=_cparams(("parallel", "parallel")),
        name="final_norm",
    )(x, fw)


def _rope_perm():
    perm = np.zeros((LANES,), np.int32)
    half = ROPE_AXIS_DIM // 2
    for i in range(LANES // 2):
        comp, axis, j = i // 32, (i % 32) // half, i % half
        perm[i] = comp * DA_HEAD_DIM + axis * ROPE_AXIS_DIM + j
        perm[LANES // 2 + i] = perm[i] + half
    return perm


def _rope_tables(n):
    rows = n // GRID_W
    pos_r = jnp.repeat(jnp.arange(rows, dtype=F32), GRID_W)
    pos_c = jnp.broadcast_to(jnp.arange(GRID_W, dtype=F32), (rows, GRID_W)).reshape(-1)
    inv_freq = ROPE_BASE ** (-jnp.arange(0, ROPE_AXIS_DIM, 2, dtype=F32) / ROPE_AXIS_DIM)
    ang = jnp.stack([pos_r[:, None] * inv_freq, pos_c[:, None] * inv_freq], axis=1)
    cos, sin = jnp.cos(ang), jnp.sin(ang)
    half = jnp.concatenate([cos.reshape(n, 32), cos.reshape(n, 32)], axis=1)
    cos_t = jnp.concatenate([half, half], axis=1)
    sh = jnp.concatenate([sin.reshape(n, 32), sin.reshape(n, 32)], axis=1)
    sin_t = jnp.concatenate([-sh, sh], axis=1)
    return cos_t, sin_t


def _blockdiag_gate(wa, wx):
    per = HALF // (RG_W // RG_BLOCKS)
    eye = jnp.eye(per, dtype=wa.dtype)

    def bd(w):
        w4 = w.reshape(2, RG_BLOCKS // per, per, w.shape[-2], w.shape[-1])
        return jnp.einsum("dhbij,bc->dhbicj", w4, eye).reshape(2, RG_BLOCKS // per, HALF, HALF)

    return jnp.concatenate([bd(wa), bd(wx)], axis=-1).astype(BF16)


def kernel(x, c, ctx, c_ctx, mod_w, mod_b, norm1_w, norm2_w, w_in, conv_a_w, rg_conv_w, rg_conv_b, rg_wa, rg_ba, rg_wx, rg_bx, rg_lambda, da_lambda, da_subln_w, w_branch, w_out, ffn_w13, ffn_w2, router_w, moe_w13, moe_w2, final_norm_w):
    bsz, n, d = x.shape
    nc = ctx.shape[1]
    depth = mod_w.shape[0]
    cos_t, sin_t = _rope_tables(n)
    perm = _rope_perm()
    ref_cols = {name: np.arange(start, start + width, dtype=np.int32) for name, start, width in REF_COLS}
    for name in ("q", "k"):
        for h in range(DA_HEADS):
            base = ref_cols[name][0] + h * LANES
            ref_cols[name][h * LANES:(h + 1) * LANES] = base + perm
    z_cols = np.concatenate([ref_cols[name] for name in Z_ORDER])

    bp = -(-(bsz + 1) // 8) * 8
    cc = jnp.zeros((bp, d), F32).at[:bsz].set(c).at[bsz].set(c_ctx)
    lat_row = lambda b: b
    ctx_row = lambda b: bsz
    cx = ctx
    moe_rows = SWIGLU_ROWS if bsz * n >= 8192 else 128

    for l in range(depth):
        last = l == depth - 1
        lam_init = 0.8 - 0.6 * math.exp(-0.3 * l)
        j = l // 2
        mod, lam = _mod_call(cc, mod_w[l], mod_b[l][None, :], da_lambda[l], lam_init)
        mod3 = mod.reshape(bp, 1, N_MOD * d)
        w_in_l = w_in[l][:, z_cols].astype(BF16)
        nw1 = norm1_w[l][None, :]
        nw2 = norm2_w[l][None, :]
        p = dict(
            conv_a_w=conv_a_w[l], rg_conv_w=rg_conv_w[l], rg_conv_b=rg_conv_b[l][:, None, :],
            rg_wg=_blockdiag_gate(rg_wa[l], rg_wx[l]),
            rg_ba=rg_ba[l].reshape(2, 1, RG_W), rg_bx=rg_bx[l].reshape(2, 1, RG_W),
            rg_lambda=rg_lambda[l][:, None, :])
        subw = da_subln_w[l][None, :]
        wb = w_branch[l].astype(BF16)
        wo = w_out[l].astype(BF16)

        ncol = w_in.shape[2] // COL
        z = _in_call(x, nw1, mod3, lat_row, w_in_l, range(ncol), True, cos_t, sin_t)
        if last:
            zc = _in_call(cx, nw1, mod3, ctx_row, w_in_l, CTX_LAST_BLOCKS, False)
            zc_off = Z_CTX_LAST
        else:
            zc = _in_call(cx, nw1, mod3, ctx_row, w_in_l, range(ncol), False)
            zc_off = Z_FULL

        seq_out = _seq_call(z, zc, zc_off, p, not last)
        yd = _attn_call(z, z, zc, zc_off, lam, subw, lam_init, True)
        moe_layer = l % 2 == 1
        if moe_layer:
            rw32 = jnp.zeros((d, LANES), F32).at[:, :N_EXPERTS].set(router_w[j])
            rw_hi = rw32.astype(BF16)
            rw = jnp.stack([rw_hi, (rw32 - rw_hi.astype(F32)).astype(BF16)])
            xn, h, route = _merge_call(seq_out[0], seq_out[1], yd, z, x, mod3, lat_row, nw2, wb, wo, rw)
            t = bsz * n
            tm = min(512, n)
            route2 = route.reshape(t, LANES)
            dest, pad_rows, block_e, n_used, n_blocks = _route_plan(route2, moe_rows)
            dest3 = dest.reshape(t // tm, 1, 2 * tm)
            pad3 = pad_rows.reshape(t // tm, 1, -1)
            xs = _dispatch_call(h.reshape(t, d), dest3, pad3, n_blocks * moe_rows, tm)
            ys = _moe_call(xs, block_e, n_used, moe_w13[j].astype(BF16), moe_w2[j].astype(BF16), moe_rows)
            if last:
                out = _final_call(xn.reshape(t, d), ys, dest3, route2, mod3, n, final_norm_w[None, :], tm)
                return out.reshape(bsz, n, d)
            raise NotImplementedError("expert layer followed by another layer")
        xn, h = _merge_call(seq_out[0], seq_out[1], yd, z, x, mod3, lat_row, nw2, wb, wo)
        w13 = ffn_w13[j].astype(BF16)
        w2 = ffn_w2[j].astype(BF16)
        x = _ffn_call(h, xn, mod3, lat_row, w13, w2)
        if not last:
            ycd = _attn_call(zc, None, zc, zc_off, lam, subw, lam_init, False)
            cxn, hc = _merge_call(seq_out[2], seq_out[3], ycd, zc, cx, mod3, ctx_row, nw2, wb, wo)
            cx = _ffn_call(hc, cxn, mod3, ctx_row, w13, w2)
    return _norm_call(x, final_norm_w[None, :])
```

```python
import functools
import math

import jax
import jax.numpy as jnp
import numpy as np
from jax import lax
from jax.experimental import pallas as pl
from jax.experimental.pallas import tpu as pltpu

F32 = jnp.float32
BF16 = jnp.bfloat16

EPS = 1e-6
N_MOD = 6
GRID_W = 64
CONV_W = 512
RG_W = 512
RG_BLOCKS = 8
RG_C = 8.0
DA_HEADS = 4
DA_HEAD_DIM = 64
DA_V_DIM = 128
ROPE_AXIS_DIM = 32
ROPE_BASE = 10000.0
N_EXPERTS = 8
LANES = 128
HALF = 256
VMEM_LIMIT = 56 * 1024 * 1024
IN_PROJ_ROWS = 2048
SWIGLU_ROWS = 512
SWIGLU_FF_TILE = 1792

COL = 1024
REF_COLS = (("a_b", 0, 512), ("a_c", 512, 512), ("a_x", 1024, 512), ("r_g", 1536, 512), ("r_x", 2048, 512),
            ("q", 2560, 512), ("k", 3072, 512), ("v", 3584, 512), ("g_a", 4096, 1024), ("g_r", 5120, 1024),
            ("g_d", 6144, 1024))
Z_ORDER = ("q", "k", "v", "a_b", "a_c", "a_x", "r_g", "r_x", "g_a", "g_r", "g_d")
Z_FULL = dict(q=0, k=4, v=8, a_b=12, a_c=16, a_x=20, r_g=24, r_x=28, g_a=32, g_r=40, g_d=48)
CTX_LAST_BLOCKS = (0, 1, 3)
Z_CTX_LAST = dict(k=4, v=8, r_x=20)
LOG2E = 1.4426950408889634


def _cparams(sem, vmem=VMEM_LIMIT):
    return pltpu.CompilerParams(dimension_semantics=sem, vmem_limit_bytes=vmem)


def _sigmoid(x):
    return 1.0 / (1.0 + jnp.exp(-x))


def _silu(x):
    return x * _sigmoid(x)


def _gelu_tanh(x):
    return 0.5 * x * (1.0 + jnp.tanh(math.sqrt(2.0 / math.pi) * (x + 0.044715 * (x * x * x))))


def _rms(x):
    return x * lax.rsqrt(jnp.mean(x * x, axis=-1, keepdims=True) + EPS)


def _mod_kernel(cc_ref, w_ref, b_ref, dl_ref, mod_ref, lam_ref, *, lam_init):
    cc = cc_ref[...]
    s = _silu(cc)
    mod_ref[...] = jnp.dot(s, w_ref[...], preferred_element_type=F32,
                           precision=lax.Precision.HIGHEST) + b_ref[...]
    dl = dl_ref[...]
    s1 = jnp.sum(dl[0:1] * dl[1:2], axis=-1, keepdims=True)
    s2 = jnp.sum(dl[2:3] * dl[3:4], axis=-1, keepdims=True)
    lam = jnp.exp(s1) - jnp.exp(s2) + lam_init
    lam_ref[...] = jnp.broadcast_to(lam, lam_ref.shape)


def _mod_call(cc, w, b, dl, lam_init):
    bp, d = cc.shape
    nd = w.shape[1]
    tn = 1024
    return pl.pallas_call(
        functools.partial(_mod_kernel, lam_init=lam_init),
        grid=(nd // tn,),
        in_specs=[
            pl.BlockSpec((bp, d), lambda j: (0, 0)),
            pl.BlockSpec((d, tn), lambda j: (0, j)),
            pl.BlockSpec((1, tn), lambda j: (0, j)),
            pl.BlockSpec(dl.shape, lambda j: (0, 0)),
        ],
        out_specs=[
            pl.BlockSpec((bp, tn), lambda j: (0, j)),
            pl.BlockSpec((8, LANES), lambda j: (0, 0)),
        ],
        out_shape=[jax.ShapeDtypeStruct((bp, nd), F32), jax.ShapeDtypeStruct((8, LANES), F32)],
        compiler_params=_cparams(("arbitrary",)),
        name="mod",
    )(cc, w, b, dl)


def _in_kernel(*refs, rope):
    if rope:
        x_ref, nw_ref, sh_ref, sc_ref, cos_ref, sin_ref, w_ref, o_ref, u_scr = refs
    else:
        x_ref, nw_ref, sh_ref, sc_ref, w_ref, o_ref, u_scr = refs
    j = pl.program_id(2)

    @pl.when(j == 0)
    def _():
        y = _rms(x_ref[0]) * nw_ref[...]
        u_scr[...] = (y * (1.0 + sc_ref[0]) + sh_ref[0]).astype(BF16)

    acc = jnp.dot(u_scr[...], w_ref[...], preferred_element_type=F32)
    q_scale = DA_HEAD_DIM ** -0.5 * LOG2E
    n_q = DA_HEADS * 2 * DA_HEAD_DIM // LANES

    @pl.when(j == 0)
    def _():
        for g in range(acc.shape[1] // LANES):
            r = acc[:, g * LANES:(g + 1) * LANES]
            if rope:
                r = r * cos_ref[...] + pltpu.roll(r, LANES // 2, 1) * sin_ref[...]
            if g < n_q:
                r = r * q_scale
            o_ref[0, :, g * LANES:(g + 1) * LANES] = r.astype(o_ref.dtype)

    @pl.when(j != 0)
    def _():
        o_ref[0] = acc.astype(o_ref.dtype)


def _in_call(x, nw, mod3, mod_row, w, col_blocks, rope, cos_t=None, sin_t=None):
    bsz, n, d = x.shape
    tm = min(IN_PROJ_ROWS, n)
    nj = len(col_blocks)
    cb = tuple(col_blocks)
    if cb == tuple(range(nj)):
        wmap = lambda b, i, j: (0, j)
    else:
        assert cb == CTX_LAST_BLOCKS
        wmap = lambda b, i, j: (0, j + j // 2)
    in_specs = [
        pl.BlockSpec((1, tm, d), lambda b, i, j: (b, i, 0)),
        pl.BlockSpec((1, d), lambda b, i, j: (0, 0)),
        pl.BlockSpec((1, 1, d), lambda b, i, j: (mod_row(b), 0, 0)),
        pl.BlockSpec((1, 1, d), lambda b, i, j: (mod_row(b), 0, 1)),
    ]
    args = [x, nw, mod3, mod3]
    if rope:
        in_specs += [pl.BlockSpec((tm, LANES), lambda b, i, j: (i, 0)),
                     pl.BlockSpec((tm, LANES), lambda b, i, j: (i, 0))]
        args += [cos_t, sin_t]
    in_specs.append(pl.BlockSpec((d, COL), wmap))
    args.append(w)
    return pl.pallas_call(
        functools.partial(_in_kernel, rope=rope),
        grid=(bsz, n // tm, nj),
        in_specs=in_specs,
        out_specs=pl.BlockSpec((1, tm, COL), lambda b, i, j: (b, i, j)),
        out_shape=jax.ShapeDtypeStruct((bsz, n, nj * COL), BF16),
        scratch_shapes=[pltpu.VMEM((tm, d), BF16)],
        compiler_params=_cparams(("parallel", "parallel", "arbitrary")),
        name="in_proj",
    )(*args)


SEQ_TILE = 64
PAD = 8
SCAN_ROWS = 64


def _seq_kernel(*refs, n, nc, need_ctx_out):
    it = iter(refs)
    ab_ref, ac_ref, ax_ref, rg_ref, rx_ref = (next(it) for _ in range(5))
    if need_ctx_out:
        cab_ref, cac_ref, cax_ref, crg_ref = (next(it) for _ in range(4))
    crx_ref = next(it)
    wa_ref, cw_ref, cbias_ref, wg_ref, ba_ref, bx_ref, lam_ref = (next(it) for _ in range(7))
    ya_ref, yr_ref = next(it), next(it)
    if need_ctx_out:
        yca_ref, ycr_ref = next(it), next(it)
    xp, a_scr, b_scr = next(it), next(it), next(it)
    T = SEQ_TILE
    W = HALF

    def stage(val, ns):
        xp[0:PAD, :] = jnp.zeros((PAD, W), F32)
        xp[PAD:PAD + ns, :] = val
        xp[PAD + ns:2 * PAD + ns, :] = jnp.zeros((PAD, W), F32)

    def conv_mixer(b_ref, c_ref, x_ref, out_ref, ns):
        stage(c_ref[0].astype(F32) * x_ref[0].astype(F32), ns)
        w0, w1, w2 = wa_ref[0:1, :], wa_ref[1:2, :], wa_ref[2:3, :]

        def body(ti, carry):
            t0 = pl.multiple_of(ti * T, T)
            ps = xp[pl.ds(t0, T + 2 * PAD), :]
            conv = w0 * ps[PAD - 1:PAD - 1 + T] + w1 * ps[PAD:PAD + T] + w2 * ps[PAD + 1:PAD + 1 + T]
            gate = b_ref[0, pl.ds(t0, T), :].astype(F32)
            out_ref[0, pl.ds(t0, T), :] = (gate * conv).astype(out_ref.dtype)
            return carry

        lax.fori_loop(0, ns // T, body, 0)

    nl = -lam_ref[...]
    sp = jnp.maximum(nl, 0.0) + jnp.log1p(jnp.exp(-jnp.abs(nl)))
    NG = W // LANES

    def rglru(x_ref, ns, h0):
        stage(x_ref[0].astype(F32), ns)

        def gates_body(ti, carry):
            t0 = pl.multiple_of(ti * T, T)
            xs = xp[pl.ds(t0, T + 2 * PAD), :]
            for d in range(2):
                off = PAD - 3 if d == 0 else PAD
                xc = cbias_ref[d]
                for k in range(4):
                    xc = xc + cw_ref[d, k:k + 1, :] * xs[off + k:off + k + T]
                g = jnp.dot(xc.astype(BF16), wg_ref[d, 0], preferred_element_type=F32)
                r = _sigmoid(g[:, :W] + ba_ref[d])
                i = _sigmoid(g[:, W:] + bx_ref[d])
                log_a = (-RG_C * r) * sp[d]
                a = jnp.exp(log_a)
                bb = jnp.sqrt(1.0 - a * a) * (i * xc)
                for g in range(NG):
                    a_scr[d, g, pl.ds(t0, T), :] = a[:, g * LANES:(g + 1) * LANES]
                    b_scr[d, g, pl.ds(t0, T), :] = bb[:, g * LANES:(g + 1) * LANES]
            return carry

        lax.fori_loop(0, ns // T, gates_body, 0, unroll=2)
        R = SCAN_ROWS
        nch = R // 8
        steps = ns // R
        sub = lax.broadcasted_iota(jnp.int32, (nch, 8, LANES), 1)

        def scan_body(ci, carry):
            out = []
            for d in range(2):
                r0 = pl.multiple_of((ci if d == 0 else steps - 1 - ci) * R, R)
                edge = 7 if d == 0 else 0
                for g in range(NG):
                    a = a_scr[d, g, pl.ds(r0, R), :].reshape(nch, 8, LANES)
                    b = b_scr[d, g, pl.ds(r0, R), :].reshape(nch, 8, LANES)
                    for s in (1, 2, 4):
                        sh = s if d == 0 else 8 - s
                        m = (sub >= s) if d == 0 else (sub < 8 - s)
                        b = b + a * jnp.where(m, pltpu.roll(b, sh, 1), 0.0)
                        a = a * jnp.where(m, pltpu.roll(a, sh, 1), 1.0)
                    c = carry[d * NG + g]
                    hs = [None] * nch
                    for j in (range(nch) if d == 0 else range(nch - 1, -1, -1)):
                        hs[j] = b[j] + a[j] * c
                        c = (jnp.broadcast_to(a[j, edge:edge + 1, :], (8, LANES)) * c
                             + jnp.broadcast_to(b[j, edge:edge + 1, :], (8, LANES)))
                    b_scr[d, g, pl.ds(r0, R), :] = jnp.concatenate(hs, axis=0)
                    out.append(c)
            return tuple(out)

        init = tuple(jnp.broadcast_to(h0[d][g], (8, LANES)) for d in range(2) for g in range(NG))
        ends = lax.fori_loop(0, steps, scan_body, init)
        return [[ends[d * NG + g][0:1, :] for g in range(NG)] for d in range(2)]

    def rg_out(g_ref, out_ref, ns):
        def body(ti, carry):
            t0 = pl.multiple_of(ti * T, T)
            for g in range(NG):
                hs = b_scr[0, g, pl.ds(t0, T), :] + b_scr[1, g, pl.ds(t0, T), :]
                gate = g_ref[0, pl.ds(t0, T), g * LANES:(g + 1) * LANES].astype(F32)
                out_ref[0, pl.ds(t0, T), g * LANES:(g + 1) * LANES] = (_gelu_tanh(gate) * hs).astype(out_ref.dtype)
            return carry

        lax.fori_loop(0, ns // T, body, 0)

    zero = [[jnp.zeros((1, LANES), F32)] * NG] * 2
    h_ctx = rglru(crx_ref, nc, zero)
    if need_ctx_out:
        rg_out(crg_ref, ycr_ref, nc)
        conv_mixer(cab_ref, cac_ref, cax_ref, yca_ref, nc)
    rglru(rx_ref, n, h_ctx)
    rg_out(rg_ref, yr_ref, n)
    conv_mixer(ab_ref, ac_ref, ax_ref, ya_ref, n)


def _seq_call(z, zc, zc_off, p, need_ctx_out):
    bsz, n, _ = z.shape
    nc = zc.shape[1]
    u = HALF // LANES

    def zspec(rows, off):
        return pl.BlockSpec((1, rows, HALF), lambda b, c: (b, 0, off // u + c))

    names = ("a_b", "a_c", "a_x", "r_g", "r_x")
    in_specs = [zspec(n, Z_FULL[k]) for k in names]
    args = [z] * 5
    if need_ctx_out:
        in_specs += [zspec(nc, zc_off[k]) for k in names[:4]]
        args += [zc] * 4
    in_specs.append(zspec(nc, zc_off["r_x"]))
    args.append(zc)
    in_specs += [
        pl.BlockSpec((3, HALF), lambda b, c: (0, c)),
        pl.BlockSpec((2, 4, HALF), lambda b, c: (0, 0, c)),
        pl.BlockSpec((2, 1, HALF), lambda b, c: (0, 0, c)),
        pl.BlockSpec((2, 1, HALF, 2 * HALF), lambda b, c: (0, c, 0, 0)),
        pl.BlockSpec((2, 1, HALF), lambda b, c: (0, 0, c)),
        pl.BlockSpec((2, 1, HALF), lambda b, c: (0, 0, c)),
        pl.BlockSpec((2, 1, HALF), lambda b, c: (0, 0, c)),
    ]
    args += [p["conv_a_w"], p["rg_conv_w"], p["rg_conv_b"], p["rg_wg"], p["rg_ba"], p["rg_bx"], p["rg_lambda"]]
    out_specs = [pl.BlockSpec((1, n, HALF), lambda b, c: (b, 0, c))] * 2
    out_shape = [jax.ShapeDtypeStruct((bsz, n, CONV_W), BF16), jax.ShapeDtypeStruct((bsz, n, RG_W), BF16)]
    if need_ctx_out:
        out_specs += [pl.BlockSpec((1, nc, HALF), lambda b, c: (b, 0, c))] * 2
        out_shape += [jax.ShapeDtypeStruct((bsz, nc, CONV_W), BF16), jax.ShapeDtypeStruct((bsz, nc, RG_W), BF16)]
    npad = max(n, nc) + 2 * PAD
    return pl.pallas_call(
        functools.partial(_seq_kernel, n=n, nc=nc, need_ctx_out=need_ctx_out),
        grid=(bsz, CONV_W // HALF),
        in_specs=in_specs,
        out_specs=out_specs,
        out_shape=out_shape,
        scratch_shapes=[pltpu.VMEM((npad, HALF), F32), pltpu.VMEM((2, u, max(n, nc), LANES), F32),
                        pltpu.VMEM((2, u, max(n, nc), LANES), F32)],
        compiler_params=_cparams(("parallel", "parallel")),
        name="seq_mixers",
    )(*args)


def _attn_kernel(*refs, has_lat, lam_init):
    if has_lat:
        q_ref, k_ref, v_ref, ck_ref, cv_ref, lam_ref, sw_ref, o_ref, k_all, v_ext = refs
    else:
        q_ref, ck_ref, cv_ref, lam_ref, sw_ref, o_ref, k_all, v_ext = refs
    nc = ck_ref.shape[1]
    nk = k_all.shape[0]

    @pl.when(pl.program_id(1) == 0)
    def _():
        k_all[0:nc, :] = ck_ref[0]
        if has_lat:
            k_all[nc:nk, :] = k_ref[0]
        for h in range(DA_HEADS):
            sl = slice(h * LANES, (h + 1) * LANES)
            v_ext[h, 0:nc, 0:LANES] = cv_ref[0, :, sl]
            if has_lat:
                v_ext[h, nc:nk, 0:LANES] = v_ref[0, :, sl]
            v_ext[h, :, LANES:2 * LANES] = jnp.ones((nk, LANES), v_ext.dtype)

    lane = lax.broadcasted_iota(jnp.int32, (1, LANES), 1)
    map1 = (lane % DA_HEAD_DIM) < (DA_HEAD_DIM // 2)
    dn = (((1,), (1,)), ((), ()))
    lam = lam_ref[0:1, 0:1]

    for h in range(DA_HEADS):
        sl = slice(h * LANES, (h + 1) * LANES)
        q = q_ref[0, :, sl]
        zero = jnp.zeros_like(q)

        def soft_v(qm):
            s = lax.dot_general(qm, k_all[:, sl], dn, preferred_element_type=F32)
            e = jnp.exp2(s - jnp.max(s, axis=-1, keepdims=True)).astype(BF16)
            r = jnp.dot(e, v_ext[h], preferred_element_type=F32)
            return r[:, :LANES] / r[:, LANES:LANES + 1]

        o = soft_v(jnp.where(map1, q, zero)) - lam * soft_v(jnp.where(map1, zero, q))
        y = _rms(o) * sw_ref[...] * (1.0 - lam_init)
        o_ref[0, :, sl] = y.astype(o_ref.dtype)


def _attn_call(zq, z, zc, zc_off, lam, subw, lam_init, has_lat):
    bsz, nq, _ = zq.shape
    nc = zc.shape[1]
    tq = min(256, nq)
    w = DA_HEADS * LANES
    u = DA_HEADS
    in_specs = [pl.BlockSpec((1, tq, w), lambda b, i: (b, i, Z_FULL["q"] // u))]
    args = [zq]
    nk = nc
    if has_lat:
        n = z.shape[1]
        nk = nc + n
        in_specs += [pl.BlockSpec((1, n, w), lambda b, i: (b, 0, Z_FULL["k"] // u)),
                     pl.BlockSpec((1, n, w), lambda b, i: (b, 0, Z_FULL["v"] // u))]
        args += [z, z]
    in_specs += [pl.BlockSpec((1, nc, w), lambda b, i: (b, 0, zc_off["k"] // u)),
                 pl.BlockSpec((1, nc, w), lambda b, i: (b, 0, zc_off["v"] // u)),
                 pl.BlockSpec((8, LANES), lambda b, i: (0, 0)),
                 pl.BlockSpec((1, LANES), lambda b, i: (0, 0))]
    args += [zc, zc, lam, subw]
    return pl.pallas_call(
        functools.partial(_attn_kernel, has_lat=has_lat, lam_init=lam_init),
        grid=(bsz, nq // tq),
        in_specs=in_specs,
        out_specs=pl.BlockSpec((1, tq, w), lambda b, i: (b, i, 0)),
        out_shape=jax.ShapeDtypeStruct((bsz, nq, DA_HEADS * DA_V_DIM), BF16),
        scratch_shapes=[pltpu.VMEM((nk, w), BF16), pltpu.VMEM((DA_HEADS, nk, 2 * LANES), BF16)],
        compiler_params=_cparams(("parallel", "arbitrary")),
        name="diff_attn",
    )(*args)


def _merge_kernel(*refs, route):
    if route:
        (ya_ref, yr_ref, yd_ref, ga_ref, gr_ref, gd_ref, x_ref, g1_ref, nw_ref, sh_ref, sc_ref,
         wb_ref, wo_ref, rw_ref, xo_ref, h_ref, rt_ref) = refs
    else:
        (ya_ref, yr_ref, yd_ref, ga_ref, gr_ref, gd_ref, x_ref, g1_ref, nw_ref, sh_ref, sc_ref,
         wb_ref, wo_ref, xo_ref, h_ref) = refs
    m = None
    for i, (y_ref, g_ref) in enumerate(((ya_ref, ga_ref), (yr_ref, gr_ref), (yd_ref, gd_ref))):
        t = _sigmoid(g_ref[0].astype(F32)) * jnp.dot(y_ref[0], wb_ref[i], preferred_element_type=F32)
        m = t if m is None else m + t
    out = jnp.dot(m.astype(BF16), wo_ref[...], preferred_element_type=F32)
    xn = x_ref[0] + g1_ref[0] * out
    xo_ref[0] = xn
    h = (_rms(xn) * nw_ref[...]) * (1.0 + sc_ref[0]) + sh_ref[0]
    h_ref[0] = h.astype(h_ref.dtype)
    if route:
        h_hi = h.astype(BF16)
        h_lo = (h - h_hi.astype(F32)).astype(BF16)
        logits = (jnp.dot(h_hi, rw_ref[0], preferred_element_type=F32)
                  + jnp.dot(h_lo, rw_ref[0], preferred_element_type=F32)
                  + jnp.dot(h_hi, rw_ref[1], preferred_element_type=F32))
        lane = lax.broadcasted_iota(jnp.int32, logits.shape, 1)
        neg = jnp.float32(-jnp.inf)
        l1 = jnp.where(lane < N_EXPERTS, logits, neg)
        m1 = jnp.max(l1, axis=-1, keepdims=True)
        i1 = jnp.min(jnp.where(l1 == m1, lane, LANES), axis=-1, keepdims=True)
        l2 = jnp.where(lane == i1, neg, l1)
        m2 = jnp.max(l2, axis=-1, keepdims=True)
        i2 = jnp.min(jnp.where(l2 == m2, lane, LANES), axis=-1, keepdims=True)
        ex = jnp.exp(m2 - m1)
        gt1 = 1.0 / (1.0 + ex)
        gt2 = ex * gt1
        rt = jnp.where(lane == 0, i1.astype(F32),
                       jnp.where(lane == 1, i2.astype(F32),
                                 jnp.where(lane == 2, gt1, jnp.where(lane == 3, gt2, 0.0))))
        rt_ref[0] = rt


def _merge_call(ya, yr, yd, z, x, mod3, mod_row, nw2, wb, wo, router_w=None):
    bsz, n, d = x.shape
    tm = min(512, n)
    route = router_w is not None
    tok = lambda b, i: (b, i, 0)
    gu = d // LANES
    in_specs = [
        pl.BlockSpec((1, tm, CONV_W), tok), pl.BlockSpec((1, tm, RG_W), tok),
        pl.BlockSpec((1, tm, DA_HEADS * DA_V_DIM), tok),
        pl.BlockSpec((1, tm, d), lambda b, i: (b, i, Z_FULL["g_a"] // gu)),
        pl.BlockSpec((1, tm, d), lambda b, i: (b, i, Z_FULL["g_r"] // gu)),
        pl.BlockSpec((1, tm, d), lambda b, i: (b, i, Z_FULL["g_d"] // gu)),
        pl.BlockSpec((1, tm, d), tok),
        pl.BlockSpec((1, 1, d), lambda b, i: (mod_row(b), 0, 2)),
        pl.BlockSpec((1, d), lambda b, i: (0, 0)),
        pl.BlockSpec((1, 1, d), lambda b, i: (mod_row(b), 0, 3)),
        pl.BlockSpec((1, 1, d), lambda b, i: (mod_row(b), 0, 4)),
        pl.BlockSpec(wb.shape, lambda b, i: (0, 0, 0)),
        pl.BlockSpec(wo.shape, lambda b, i: (0, 0)),
    ]
    args = [ya, yr, yd, z, z, z, x, mod3, nw2, mod3, mod3, wb, wo]
    out_specs = [pl.BlockSpec((1, tm, d), tok), pl.BlockSpec((1, tm, d), tok)]
    out_shape = [jax.ShapeDtypeStruct((bsz, n, d), F32),
                 jax.ShapeDtypeStruct((bsz, n, d), F32 if route else BF16)]
    if route:
        in_specs.append(pl.BlockSpec(router_w.shape, lambda b, i: (0, 0, 0)))
        args.append(router_w)
        out_specs.append(pl.BlockSpec((1, tm, LANES), tok))
        out_shape.append(jax.ShapeDtypeStruct((bsz, n, LANES), F32))
    return pl.pallas_call(
        functools.partial(_merge_kernel, route=route),
        grid=(bsz, n // tm),
        in_specs=in_specs,
        out_specs=out_specs,
        out_shape=out_shape,
        compiler_params=_cparams(("parallel", "parallel")),
        name="merge",
    )(*args)


def _ffn_kernel(h_ref, x_ref, g2_ref, wg_ref, wu_ref, w2_ref, o_ref, acc_ref):
    f = pl.program_id(2)
    h = h_ref[0]
    g = jnp.dot(h, wg_ref[...], preferred_element_type=F32)
    u = jnp.dot(h, wu_ref[...], preferred_element_type=F32)
    act = (_silu(g) * u).astype(BF16)
    part = jnp.dot(act, w2_ref[...], preferred_element_type=F32)

    @pl.when(f == 0)
    def _():
        acc_ref[...] = part

    @pl.when(f > 0)
    def _():
        acc_ref[...] += part

    @pl.when(f == pl.num_programs(2) - 1)
    def _():
        o_ref[0] = x_ref[0] + g2_ref[0] * acc_ref[...]


def _ffn_call(h, x, mod3, mod_row, w13, w2):
    bsz, n, d = x.shape
    ff = w2.shape[0]
    tm = min(SWIGLU_ROWS, n)
    tf = SWIGLU_FF_TILE
    nf = ff // tf
    tok = lambda b, i, f: (b, i, 0)
    return pl.pallas_call(
        _ffn_kernel,
        grid=(bsz, n // tm, nf),
        in_specs=[
            pl.BlockSpec((1, tm, d), tok),
            pl.BlockSpec((1, tm, d), tok),
            pl.BlockSpec((1, 1, d), lambda b, i, f: (mod_row(b), 0, 5)),
            pl.BlockSpec((d, tf), lambda b, i, f: (0, f)),
            pl.BlockSpec((d, tf), lambda b, i, f: (0, nf + f)),
            pl.BlockSpec((tf, d), lambda b, i, f: (f, 0)),
        ],
        out_specs=pl.BlockSpec((1, tm, d), tok),
        out_shape=jax.ShapeDtypeStruct((bsz, n, d), F32),
        scratch_shapes=[pltpu.VMEM((tm, d), F32)],
        compiler_params=_cparams(("parallel", "parallel", "arbitrary")),
        name="ffn",
    )(h, x, mod3, w13, w13, w2)


def _moe_kernel(be_ref, nu_ref, x_ref, wg_ref, wu_ref, w2_ref, o_ref, xb):
    i = pl.program_id(0)
    f = pl.program_id(1)
    used = i < nu_ref[0]

    @pl.when(used & (f == 0))
    def _():
        xb[...] = x_ref[...].astype(BF16)

    @pl.when(jnp.logical_not(used) & (f == 0))
    def _():
        o_ref[...] = jnp.zeros(o_ref.shape, o_ref.dtype)

    @pl.when(used)
    def _():
        x = xb[...]
        g = jnp.dot(x, wg_ref[0], preferred_element_type=F32)
        u = jnp.dot(x, wu_ref[0], preferred_element_type=F32)
        act = (_silu(g) * u).astype(BF16)
        part = jnp.dot(act, w2_ref[0], preferred_element_type=F32)

        @pl.when(f == 0)
        def _():
            o_ref[...] = part

        @pl.when(f > 0)
        def _():
            o_ref[...] += part


def _moe_call(xs, block_e, n_used, w13, w2, rows):
    n_rows, d = xs.shape
    n_blocks = n_rows // rows
    ff = w2.shape[1]
    tf = SWIGLU_FF_TILE
    nf = ff // tf

    def bi(i, nu):
        return jnp.minimum(i, nu[0] - 1)

    def fi(i, f, nu):
        return jnp.where(i < nu[0], f, nf - 1)

    grid_spec = pltpu.PrefetchScalarGridSpec(
        num_scalar_prefetch=2,
        grid=(n_blocks, nf),
        in_specs=[
            pl.BlockSpec((rows, d), lambda i, f, be, nu: (bi(i, nu), 0)),
            pl.BlockSpec((1, d, tf), lambda i, f, be, nu: (be[i], 0, fi(i, f, nu))),
            pl.BlockSpec((1, d, tf), lambda i, f, be, nu: (be[i], 0, nf + fi(i, f, nu))),
            pl.BlockSpec((1, tf, d), lambda i, f, be, nu: (be[i], fi(i, f, nu), 0)),
        ],
        out_specs=pl.BlockSpec((rows, d), lambda i, f, be, nu: (i, 0)),
        scratch_shapes=[pltpu.VMEM((rows, d), BF16)],
    )
    return pl.pallas_call(
        _moe_kernel,
        grid_spec=grid_spec,
        out_shape=jax.ShapeDtypeStruct((n_rows, d), F32),
        compiler_params=_cparams(("arbitrary", "arbitrary")),
        name="moe",
    )(block_e, n_used, xs, w13, w13, w2)


GROUP_PAD = 8
GROUP_SIZES = tuple(SWIGLU_ROWS >> s for s in range(SWIGLU_ROWS.bit_length()) if SWIGLU_ROWS >> s >= GROUP_PAD)


def _sorted_rows(tm):
    return 2 * tm + N_EXPERTS * GROUP_PAD


def _group_copies(step, src_of, dst_of, loc_ref, base_ref, pc_ref, sem, sizes):
    for e in range(N_EXPERTS):
        loc = loc_ref[step * N_EXPERTS + e]
        base = base_ref[step * N_EXPERTS + e]
        pc = pc_ref[step * N_EXPERTS + e]
        for size in sizes:
            before = pc & (-2 * size)
            src = src_of(pl.multiple_of(loc + before, GROUP_PAD), pl.multiple_of(base + before, GROUP_PAD), size)
            dst = dst_of(pl.multiple_of(loc + before, GROUP_PAD), pl.multiple_of(base + before, GROUP_PAD), size)
            yield (pc & size) != 0, pltpu.make_async_copy(src, dst, sem)


def _dispatch2_kernel(loc_ref, base_ref, pc_ref, tail_ref, nu_ref, pos_ref, h_ref, xs_hbm, srt, zeros, sem, zsem,
                      *, tm, nt, rows, sizes):
    i = pl.program_id(0)
    n = pl.num_programs(0)
    slot = i % 2
    sr = srt.shape[1]
    n_blocks = xs_hbm.shape[0] // rows
    min_used = 2 * tm * nt // rows

    def tail_copies():
        for e in range(N_EXPERTS):
            start = tail_ref[e]
            length = tail_ref[N_EXPERTS + e]
            for size in sizes[1:]:
                before = length & (-2 * size)
                dst = xs_hbm.at[pl.ds(pl.multiple_of(start + before, GROUP_PAD), size)]
                yield (length & size) != 0, pltpu.make_async_copy(zeros.at[pl.ds(0, size)], dst, zsem)
        for b in range(min_used, n_blocks):
            dst = xs_hbm.at[pl.ds(b * rows, rows)]
            yield b >= nu_ref[0], pltpu.make_async_copy(zeros, dst, zsem)

    @pl.when(i == 0)
    def _():
        zeros[...] = jnp.zeros(zeros.shape, zeros.dtype)
        for pred, cp in tail_copies():
            pl.when(pred)(cp.start)

    r = lax.broadcasted_iota(jnp.int32, (sr, tm), 0)
    hit = (r == pos_ref[0, 0:1, :]) | (r == pos_ref[0, 1:2, :])
    sel = jnp.where(hit, 1.0, 0.0).astype(BF16)
    srt[slot] = jnp.dot(sel, h_ref[...].astype(BF16), preferred_element_type=F32)

    def group_copies(step, s):
        return _group_copies(step, lambda lo, ba, sz: srt.at[s, pl.ds(lo, sz)],
                             lambda lo, ba, sz: xs_hbm.at[pl.ds(ba, sz)], loc_ref, base_ref, pc_ref, sem.at[s], sizes)

    for pred, cp in group_copies(i, slot):
        pl.when(pred)(cp.start)

    @pl.when(i > 0)
    def _():
        for pred, cp in group_copies(i - 1, 1 - slot):
            pl.when(pred)(cp.wait)

    @pl.when(i == n - 1)
    def _():
        for pred, cp in group_copies(i, slot):
            pl.when(pred)(cp.wait)
        for pred, cp in tail_copies():
            pl.when(pred)(cp.wait)


def _dispatch2_call(h2, plan, tm, rows):
    t, d = h2.shape
    nt = t // tm
    sr = _sorted_rows(tm)
    sizes = tuple(s for s in GROUP_SIZES if s <= tm)
    grid_spec = pltpu.PrefetchScalarGridSpec(
        num_scalar_prefetch=5,
        grid=(nt,),
        in_specs=[pl.BlockSpec((1, 2, tm), lambda i, *_: (i, 0, 0)),
                  pl.BlockSpec((tm, d), lambda i, *_: (i, 0))],
        out_specs=pl.BlockSpec(memory_space=pl.ANY),
        scratch_shapes=[pltpu.VMEM((2, sr, d), F32), pltpu.VMEM((rows, d), F32),
                        pltpu.SemaphoreType.DMA((2,)), pltpu.SemaphoreType.DMA(())],
    )
    return pl.pallas_call(
        functools.partial(_dispatch2_kernel, tm=tm, nt=nt, rows=rows, sizes=sizes),
        grid_spec=grid_spec,
        out_shape=jax.ShapeDtypeStruct((plan["n_blocks"] * rows, d), F32),
        compiler_params=pltpu.CompilerParams(dimension_semantics=("arbitrary",), vmem_limit_bytes=VMEM_LIMIT),
        name="moe_dispatch",
    )(plan["loc"], plan["base"], plan["pc"], plan["tail"], plan["n_used"], plan["pos_rows"], h2)


def _combine2_kernel(loc_ref, base_ref, pc_ref, x_ref, pos_ref, rt_ref, g2_ref, fw_ref, ys_hbm, o_ref, buf, sem,
                     *, tm, sizes):
    i = pl.program_id(0)
    n = pl.num_programs(0)
    slot = i % 2
    sr = buf.shape[1]

    def group_copies(step, s):
        return _group_copies(step, lambda lo, ba, sz: ys_hbm.at[pl.ds(ba, sz)],
                             lambda lo, ba, sz: buf.at[s, pl.ds(lo, sz)], loc_ref, base_ref, pc_ref, sem.at[s], sizes)

    @pl.when(i == 0)
    def _():
        buf[...] = jnp.zeros(buf.shape, buf.dtype)
        for pred, cp in group_copies(0, 0):
            pl.when(pred)(cp.start)

    @pl.when(i + 1 < n)
    def _():
        for pred, cp in group_copies(i + 1, 1 - slot):
            pl.when(pred)(cp.start)

    for pred, cp in group_copies(i, slot):
        pl.when(pred)(cp.wait)

    rt = rt_ref[...]
    col = lax.broadcasted_iota(jnp.int32, (tm, sr), 1)
    pos = pos_ref[...]
    w = jnp.where(col == pos[:, 0:1], rt[:, 2:3], jnp.where(col == pos[:, 1:2], rt[:, 3:4], 0.0))
    y = jnp.dot(w.astype(BF16), buf[slot].astype(BF16), preferred_element_type=F32)
    xn = x_ref[...] + g2_ref[0] * y
    o_ref[...] = _rms(xn) * fw_ref[...]


def _combine2_call(x2, ys, plan, route2, mod3, n_per_seq, fw, tm):
    t, d = x2.shape
    nt = t // tm
    per = n_per_seq // tm
    sr = _sorted_rows(tm)
    sizes = tuple(s for s in GROUP_SIZES if s <= tm)
    grid_spec = pltpu.PrefetchScalarGridSpec(
        num_scalar_prefetch=3,
        grid=(nt,),
        in_specs=[
            pl.BlockSpec((tm, d), lambda i, *_: (i, 0)),
            pl.BlockSpec((tm, 2), lambda i, *_: (i, 0)),
            pl.BlockSpec((tm, LANES), lambda i, *_: (i, 0)),
            pl.BlockSpec((1, 1, d), lambda i, *_: (i // per, 0, 5)),
            pl.BlockSpec((1, d), lambda i, *_: (0, 0)),
            pl.BlockSpec(memory_space=pl.ANY),
        ],
        out_specs=pl.BlockSpec((tm, d), lambda i, *_: (i, 0)),
        scratch_shapes=[pltpu.VMEM((2, sr, d), F32), pltpu.SemaphoreType.DMA((2,))],
    )
    return pl.pallas_call(
        functools.partial(_combine2_kernel, tm=tm, sizes=sizes),
        grid_spec=grid_spec,
        out_shape=jax.ShapeDtypeStruct((t, d), F32),
        compiler_params=pltpu.CompilerParams(dimension_semantics=("arbitrary",), vmem_limit_bytes=VMEM_LIMIT),
        name="combine_final",
    )(plan["loc"], plan["base"], plan["pc"], x2, plan["pos_cols"], route2, mod3, fw, ys)


def _sort_plan(route, tm, rows):
    t = route.shape[0]
    nt = t // tm
    experts = jnp.arange(N_EXPERTS, dtype=jnp.int32)
    e = route[:, :2].astype(jnp.int32).reshape(nt, 2 * tm)
    hot = (e[:, :, None] == experts[None, None, :]).astype(jnp.int32)
    cs = jnp.cumsum(hot, axis=1)
    rank = jnp.sum(cs * hot, axis=2) - 1
    cnt = cs[:, -1, :]
    pc = (cnt + GROUP_PAD - 1) // GROUP_PAD * GROUP_PAD
    loc = jnp.cumsum(pc, axis=1) - pc
    pos = jnp.sum(hot * loc[:, None, :], axis=2) + rank
    tot = jnp.sum(pc, axis=0)
    padded = (tot + rows - 1) // rows * rows
    e_end = jnp.cumsum(padded)
    e_start = e_end - padded
    base = e_start[None, :] + jnp.cumsum(pc, axis=0) - pc
    n_blocks = -(-(2 * t + nt * N_EXPERTS * (GROUP_PAD - 1)) // rows) + N_EXPERTS
    block_start = jnp.arange(n_blocks, dtype=jnp.int32) * rows
    block_e = jnp.minimum(jnp.sum((block_start[:, None] >= e_end[None, :]).astype(jnp.int32), axis=1),
                          N_EXPERTS - 1)
    pos3 = pos.reshape(nt, tm, 2)
    i32 = lambda a: a.astype(jnp.int32)
    return dict(
        loc=i32(loc.reshape(-1)), base=i32(base.reshape(-1)), pc=i32(pc.reshape(-1)),
        tail=i32(jnp.concatenate([e_start + tot, padded - tot])),
        n_used=i32(e_end[-1] // rows).reshape(1), block_e=i32(block_e), n_blocks=n_blocks,
        pos_rows=i32(jnp.swapaxes(pos3, 1, 2)), pos_cols=i32(pos3.reshape(t, 2)))


def _norm_kernel(x_ref, fw_ref, o_ref):
    o_ref[0] = _rms(x_ref[0]) * fw_ref[...]


def _norm_call(x, fw):
    bsz, n, d = x.shape
    tm = min(512, n)
    tok = lambda b, i: (b, i, 0)
    return pl.pallas_call(
        _norm_kernel,
        grid=(bsz, n // tm),
        in_specs=[pl.BlockSpec((1, tm, d), tok), pl.BlockSpec((1, d), lambda b, i: (0, 0))],
        out_specs=pl.BlockSpec((1, tm, d), tok),
        out_shape=jax.ShapeDtypeStruct((bsz, n, d), F32),
        compiler_params=_cparams(("parallel", "parallel")),
        name="final_norm",
    )(x, fw)


def _rope_perm():
    perm = np.zeros((LANES,), np.int32)
    half = ROPE_AXIS_DIM // 2
    for i in range(LANES // 2):
        comp, axis, j = i // 32, (i % 32) // half, i % half
        perm[i] = comp * DA_HEAD_DIM + axis * ROPE_AXIS_DIM + j
        perm[LANES // 2 + i] = perm[i] + half
    return perm


def _rope_tables(n):
    rows = n // GRID_W
    pos_r = jnp.repeat(jnp.arange(rows, dtype=F32), GRID_W)
    pos_c = jnp.broadcast_to(jnp.arange(GRID_W, dtype=F32), (rows, GRID_W)).reshape(-1)
    inv_freq = ROPE_BASE ** (-jnp.arange(0, ROPE_AXIS_DIM, 2, dtype=F32) / ROPE_AXIS_DIM)
    ang = jnp.stack([pos_r[:, None] * inv_freq, pos_c[:, None] * inv_freq], axis=1)
    cos, sin = jnp.cos(ang), jnp.sin(ang)
    half = jnp.concatenate([cos.reshape(n, 32), cos.reshape(n, 32)], axis=1)
    cos_t = jnp.concatenate([half, half], axis=1)
    sh = jnp.concatenate([sin.reshape(n, 32), sin.reshape(n, 32)], axis=1)
    sin_t = jnp.concatenate([-sh, sh], axis=1)
    return cos_t, sin_t


def _blockdiag_gate(wa, wx):
    per = HALF // (RG_W // RG_BLOCKS)
    eye = jnp.eye(per, dtype=wa.dtype)

    def bd(w):
        w4 = w.reshape(2, RG_BLOCKS // per, per, w.shape[-2], w.shape[-1])
        return jnp.einsum("dhbij,bc->dhbicj", w4, eye).reshape(2, RG_BLOCKS // per, HALF, HALF)

    return jnp.concatenate([bd(wa), bd(wx)], axis=-1).astype(BF16)


def kernel(x, c, ctx, c_ctx, mod_w, mod_b, norm1_w, norm2_w, w_in, conv_a_w, rg_conv_w, rg_conv_b, rg_wa, rg_ba, rg_wx, rg_bx, rg_lambda, da_lambda, da_subln_w, w_branch, w_out, ffn_w13, ffn_w2, router_w, moe_w13, moe_w2, final_norm_w):
    bsz, n, d = x.shape
    nc = ctx.shape[1]
    depth = mod_w.shape[0]
    cos_t, sin_t = _rope_tables(n)
    perm = _rope_perm()
    ref_cols = {name: np.arange(start, start + width, dtype=np.int32) for name, start, width in REF_COLS}
    for name in ("q", "k"):
        for h in range(DA_HEADS):
            base = ref_cols[name][0] + h * LANES
            ref_cols[name][h * LANES:(h + 1) * LANES] = base + perm
    z_cols = np.concatenate([ref_cols[name] for name in Z_ORDER])

    bp = -(-(bsz + 1) // 8) * 8
    cc = jnp.zeros((bp, d), F32).at[:bsz].set(c).at[bsz].set(c_ctx)
    lat_row = lambda b: b
    ctx_row = lambda b: bsz
    cx = ctx
    moe_rows = SWIGLU_ROWS if bsz * n >= 8192 else 128

    for l in range(depth):
        last = l == depth - 1
        lam_init = 0.8 - 0.6 * math.exp(-0.3 * l)
        j = l // 2
        mod, lam = _mod_call(cc, mod_w[l], mod_b[l][None, :], da_lambda[l], lam_init)
        mod3 = mod.reshape(bp, 1, N_MOD * d)
        w_in_l = w_in[l][:, z_cols].astype(BF16)
        nw1 = norm1_w[l][None, :]
        nw2 = norm2_w[l][None, :]
        p = dict(
            conv_a_w=conv_a_w[l], rg_conv_w=rg_conv_w[l], rg_conv_b=rg_conv_b[l][:, None, :],
            rg_wg=_blockdiag_gate(rg_wa[l], rg_wx[l]),
            rg_ba=rg_ba[l].reshape(2, 1, RG_W), rg_bx=rg_bx[l].reshape(2, 1, RG_W),
            rg_lambda=rg_lambda[l][:, None, :])
        subw = da_subln_w[l][None, :]
        wb = w_branch[l].astype(BF16)
        wo = w_out[l].astype(BF16)

        ncol = w_in.shape[2] // COL
        z = _in_call(x, nw1, mod3, lat_row, w_in_l, range(ncol), True, cos_t, sin_t)
        if last:
            zc = _in_call(cx, nw1, mod3, ctx_row, w_in_l, CTX_LAST_BLOCKS, False)
            zc_off = Z_CTX_LAST
        else:
            zc = _in_call(cx, nw1, mod3, ctx_row, w_in_l, range(ncol), False)
            zc_off = Z_FULL

        seq_out = _seq_call(z, zc, zc_off, p, not last)
        yd = _attn_call(z, z, zc, zc_off, lam, subw, lam_init, True)
        moe_layer = l % 2 == 1
        if moe_layer:
            rw32 = jnp.zeros((d, LANES), F32).at[:, :N_EXPERTS].set(router_w[j])
            rw_hi = rw32.astype(BF16)
            rw = jnp.stack([rw_hi, (rw32 - rw_hi.astype(F32)).astype(BF16)])
            xn, h, route = _merge_call(seq_out[0], seq_out[1], yd, z, x, mod3, lat_row, nw2, wb, wo, rw)
            t = bsz * n
            tm = min(512, n)
            route2 = route.reshape(t, LANES)
            plan = _sort_plan(route2, tm, moe_rows)
            xs = _dispatch2_call(h.reshape(t, d), plan, tm, moe_rows)
            ys = _moe_call(xs, plan["block_e"], plan["n_used"], moe_w13[j].astype(BF16),
                           moe_w2[j].astype(BF16), moe_rows)
            if last:
                out = _combine2_call(xn.reshape(t, d), ys, plan, route2, mod3, n, final_norm_w[None, :], tm)
                return out.reshape(bsz, n, d)
            raise NotImplementedError("expert layer followed by another layer")
        xn, h = _merge_call(seq_out[0], seq_out[1], yd, z, x, mod3, lat_row, nw2, wb, wo)
        w13 = ffn_w13[j].astype(BF16)
        w2 = ffn_w2[j].astype(BF16)
        x = _ffn_call(h, xn, mod3, lat_row, w13, w2)
        if not last:
            ycd = _attn_call(zc, None, zc, zc_off, lam, subw, lam_init, False)
            cxn, hc = _merge_call(seq_out[2], seq_out[3], ycd, zc, cx, mod3, ctx_row, nw2, wb, wo)
            cx = _ffn_call(hc, cxn, mod3, ctx_row, w13, w2)
    return _norm_call(x, final_norm_w[None, :])
```

```python
import functools
import math

import jax
import jax.numpy as jnp
import numpy as np
from jax import lax
from jax.experimental import pallas as pl
from jax.experimental.pallas import tpu as pltpu

F32 = jnp.float32
BF16 = jnp.bfloat16

EPS = 1e-6
N_MOD = 6
GRID_W = 64
CONV_W = 512
RG_W = 512
RG_BLOCKS = 8
RG_C = 8.0
DA_HEADS = 4
DA_HEAD_DIM = 64
DA_V_DIM = 128
ROPE_AXIS_DIM = 32
ROPE_BASE = 10000.0
N_EXPERTS = 8
LANES = 128
HALF = 256
VMEM_LIMIT = 56 * 1024 * 1024
IN_PROJ_ROWS = 2048
ATTN_Q_ROWS = 512
SWIGLU_ROWS = 512
SWIGLU_FF_TILE = 1792

COL = 1024
REF_COLS = (("a_b", 0, 512), ("a_c", 512, 512), ("a_x", 1024, 512), ("r_g", 1536, 512), ("r_x", 2048, 512),
            ("q", 2560, 512), ("k", 3072, 512), ("v", 3584, 512), ("g_a", 4096, 1024), ("g_r", 5120, 1024),
            ("g_d", 6144, 1024))
Z_ORDER = ("q", "k", "v", "a_b", "a_c", "a_x", "r_g", "r_x", "g_a", "g_r", "g_d")
Z_FULL = dict(q=0, k=4, v=8, a_b=12, a_c=16, a_x=20, r_g=24, r_x=28, g_a=32, g_r=40, g_d=48)
CTX_LAST_BLOCKS = (0, 1, 3)
Z_CTX_LAST = dict(k=4, v=8, r_x=20)
LOG2E = 1.4426950408889634


def _cparams(sem, vmem=VMEM_LIMIT):
    return pltpu.CompilerParams(dimension_semantics=sem, vmem_limit_bytes=vmem)


def _sigmoid(x):
    return 1.0 / (1.0 + jnp.exp(-x))


def _silu(x):
    return x * _sigmoid(x)


def _gelu_tanh(x):
    return 0.5 * x * (1.0 + jnp.tanh(math.sqrt(2.0 / math.pi) * (x + 0.044715 * (x * x * x))))


def _rms(x):
    return x * lax.rsqrt(jnp.mean(x * x, axis=-1, keepdims=True) + EPS)


def _mod_kernel(cc_ref, w_ref, b_ref, dl_ref, mod_ref, lam_ref, *, lam_init):
    cc = cc_ref[...]
    s = _silu(cc)
    mod_ref[...] = jnp.dot(s, w_ref[...], preferred_element_type=F32,
                           precision=lax.Precision.HIGHEST) + b_ref[...]
    dl = dl_ref[...]
    s1 = jnp.sum(dl[0:1] * dl[1:2], axis=-1, keepdims=True)
    s2 = jnp.sum(dl[2:3] * dl[3:4], axis=-1, keepdims=True)
    lam = jnp.exp(s1) - jnp.exp(s2) + lam_init
    lam_ref[...] = jnp.broadcast_to(lam, lam_ref.shape)


def _mod_call(cc, w, b, dl, lam_init):
    bp, d = cc.shape
    nd = w.shape[1]
    tn = 1024
    return pl.pallas_call(
        functools.partial(_mod_kernel, lam_init=lam_init),
        grid=(nd // tn,),
        in_specs=[
            pl.BlockSpec((bp, d), lambda j: (0, 0)),
            pl.BlockSpec((d, tn), lambda j: (0, j)),
            pl.BlockSpec((1, tn), lambda j: (0, j)),
            pl.BlockSpec(dl.shape, lambda j: (0, 0)),
        ],
        out_specs=[
            pl.BlockSpec((bp, tn), lambda j: (0, j)),
            pl.BlockSpec((8, LANES), lambda j: (0, 0)),
        ],
        out_shape=[jax.ShapeDtypeStruct((bp, nd), F32), jax.ShapeDtypeStruct((8, LANES), F32)],
        compiler_params=_cparams(("arbitrary",)),
        name="mod",
    )(cc, w, b, dl)


def _in_kernel(*refs, rope):
    if rope:
        x_ref, nw_ref, sh_ref, sc_ref, cos_ref, sin_ref, w_ref, o_ref, u_scr = refs
    else:
        x_ref, nw_ref, sh_ref, sc_ref, w_ref, o_ref, u_scr = refs
    j = pl.program_id(2)

    @pl.when(j == 0)
    def _():
        y = _rms(x_ref[0]) * nw_ref[...]
        u_scr[...] = (y * (1.0 + sc_ref[0]) + sh_ref[0]).astype(BF16)

    acc = jnp.dot(u_scr[...], w_ref[...], preferred_element_type=F32)
    q_scale = DA_HEAD_DIM ** -0.5 * LOG2E
    n_q = DA_HEADS * 2 * DA_HEAD_DIM // LANES

    @pl.when(j == 0)
    def _():
        for g in range(acc.shape[1] // LANES):
            r = acc[:, g * LANES:(g + 1) * LANES]
            if rope:
                r = r * cos_ref[...] + pltpu.roll(r, LANES // 2, 1) * sin_ref[...]
            if g < n_q:
                r = r * q_scale
            o_ref[0, :, g * LANES:(g + 1) * LANES] = r.astype(o_ref.dtype)

    @pl.when(j != 0)
    def _():
        o_ref[0] = acc.astype(o_ref.dtype)


def _in_call(x, nw, mod3, mod_row, w, col_blocks, rope, cos_t=None, sin_t=None):
    bsz, n, d = x.shape
    tm = min(IN_PROJ_ROWS, n)
    nj = len(col_blocks)
    cb = tuple(col_blocks)
    if cb == tuple(range(nj)):
        wmap = lambda b, i, j: (0, j)
    else:
        assert cb == CTX_LAST_BLOCKS
        wmap = lambda b, i, j: (0, j + j // 2)
    in_specs = [
        pl.BlockSpec((1, tm, d), lambda b, i, j: (b, i, 0)),
        pl.BlockSpec((1, d), lambda b, i, j: (0, 0)),
        pl.BlockSpec((1, 1, d), lambda b, i, j: (mod_row(b), 0, 0)),
        pl.BlockSpec((1, 1, d), lambda b, i, j: (mod_row(b), 0, 1)),
    ]
    args = [x, nw, mod3, mod3]
    if rope:
        in_specs += [pl.BlockSpec((tm, LANES), lambda b, i, j: (i, 0)),
                     pl.BlockSpec((tm, LANES), lambda b, i, j: (i, 0))]
        args += [cos_t, sin_t]
    in_specs.append(pl.BlockSpec((d, COL), wmap))
    args.append(w)
    return pl.pallas_call(
        functools.partial(_in_kernel, rope=rope),
        grid=(bsz, n // tm, nj),
        in_specs=in_specs,
        out_specs=pl.BlockSpec((1, tm, COL), lambda b, i, j: (b, i, j)),
        out_shape=jax.ShapeDtypeStruct((bsz, n, nj * COL), BF16),
        scratch_shapes=[pltpu.VMEM((tm, d), BF16)],
        compiler_params=_cparams(("parallel", "parallel", "arbitrary")),
        name="in_proj",
    )(*args)


SEQ_TILE = 64
PAD = 8
SCAN_ROWS = 64


def _seq_kernel(*refs, n, nc, need_ctx_out):
    it = iter(refs)
    ab_ref, ac_ref, ax_ref, rg_ref, rx_ref = (next(it) for _ in range(5))
    if need_ctx_out:
        cab_ref, cac_ref, cax_ref, crg_ref = (next(it) for _ in range(4))
    crx_ref = next(it)
    wa_ref, cw_ref, cbias_ref, wg_ref, ba_ref, bx_ref, lam_ref = (next(it) for _ in range(7))
    ya_ref, yr_ref = next(it), next(it)
    if need_ctx_out:
        yca_ref, ycr_ref = next(it), next(it)
    xp, a_scr, b_scr = next(it), next(it), next(it)
    T = SEQ_TILE
    W = HALF

    def stage(val, ns):
        xp[0:PAD, :] = jnp.zeros((PAD, W), F32)
        xp[PAD:PAD + ns, :] = val
        xp[PAD + ns:2 * PAD + ns, :] = jnp.zeros((PAD, W), F32)

    def conv_mixer(b_ref, c_ref, x_ref, out_ref, ns):
        stage(c_ref[0].astype(F32) * x_ref[0].astype(F32), ns)
        w0, w1, w2 = wa_ref[0:1, :], wa_ref[1:2, :], wa_ref[2:3, :]

        def body(ti, carry):
            t0 = pl.multiple_of(ti * T, T)
            ps = xp[pl.ds(t0, T + 2 * PAD), :]
            conv = w0 * ps[PAD - 1:PAD - 1 + T] + w1 * ps[PAD:PAD + T] + w2 * ps[PAD + 1:PAD + 1 + T]
            gate = b_ref[0, pl.ds(t0, T), :].astype(F32)
            out_ref[0, pl.ds(t0, T), :] = (gate * conv).astype(out_ref.dtype)
            return carry

        lax.fori_loop(0, ns // T, body, 0)

    nl = -lam_ref[...]
    sp = jnp.maximum(nl, 0.0) + jnp.log1p(jnp.exp(-jnp.abs(nl)))
    NG = W // LANES

    def rglru(x_ref, ns, h0):
        stage(x_ref[0].astype(F32), ns)

        def gates_body(ti, carry):
            t0 = pl.multiple_of(ti * T, T)
            xs = xp[pl.ds(t0, T + 2 * PAD), :]
            for d in range(2):
                off = PAD - 3 if d == 0 else PAD
                xc = cbias_ref[d]
                for k in range(4):
                    xc = xc + cw_ref[d, k:k + 1, :] * xs[off + k:off + k + T]
                g = jnp.dot(xc.astype(BF16), wg_ref[d, 0], preferred_element_type=F32)
                r = _sigmoid(g[:, :W] + ba_ref[d])
                i = _sigmoid(g[:, W:] + bx_ref[d])
                log_a = (-RG_C * r) * sp[d]
                a = jnp.exp(log_a)
                bb = jnp.sqrt(1.0 - a * a) * (i * xc)
                for g in range(NG):
                    a_scr[d, g, pl.ds(t0, T), :] = a[:, g * LANES:(g + 1) * LANES]
                    b_scr[d, g, pl.ds(t0, T), :] = bb[:, g * LANES:(g + 1) * LANES]
            return carry

        lax.fori_loop(0, ns // T, gates_body, 0, unroll=2)
        R = SCAN_ROWS
        nch = R // 8
        steps = ns // R
        sub = lax.broadcasted_iota(jnp.int32, (nch, 8, LANES), 1)

        def scan_body(ci, carry):
            out = []
            for d in range(2):
                r0 = pl.multiple_of((ci if d == 0 else steps - 1 - ci) * R, R)
                edge = 7 if d == 0 else 0
                for g in range(NG):
                    a = a_scr[d, g, pl.ds(r0, R), :].reshape(nch, 8, LANES)
                    b = b_scr[d, g, pl.ds(r0, R), :].reshape(nch, 8, LANES)
                    for s in (1, 2, 4):
                        sh = s if d == 0 else 8 - s
                        m = (sub >= s) if d == 0 else (sub < 8 - s)
                        b = b + a * jnp.where(m, pltpu.roll(b, sh, 1), 0.0)
                        a = a * jnp.where(m, pltpu.roll(a, sh, 1), 1.0)
                    c = carry[d * NG + g]
                    hs = [None] * nch
                    for j in (range(nch) if d == 0 else range(nch - 1, -1, -1)):
                        hs[j] = b[j] + a[j] * c
                        c = (jnp.broadcast_to(a[j, edge:edge + 1, :], (8, LANES)) * c
                             + jnp.broadcast_to(b[j, edge:edge + 1, :], (8, LANES)))
                    b_scr[d, g, pl.ds(r0, R), :] = jnp.concatenate(hs, axis=0)
                    out.append(c)
            return tuple(out)

        init = tuple(jnp.broadcast_to(h0[d][g], (8, LANES)) for d in range(2) for g in range(NG))
        ends = lax.fori_loop(0, steps, scan_body, init)
        return [[ends[d * NG + g][0:1, :] for g in range(NG)] for d in range(2)]

    def rg_out(g_ref, out_ref, ns):
        def body(ti, carry):
            t0 = pl.multiple_of(ti * T, T)
            for g in range(NG):
                hs = b_scr[0, g, pl.ds(t0, T), :] + b_scr[1, g, pl.ds(t0, T), :]
                gate = g_ref[0, pl.ds(t0, T), g * LANES:(g + 1) * LANES].astype(F32)
                out_ref[0, pl.ds(t0, T), g * LANES:(g + 1) * LANES] = (_gelu_tanh(gate) * hs).astype(out_ref.dtype)
            return carry

        lax.fori_loop(0, ns // T, body, 0)

    zero = [[jnp.zeros((1, LANES), F32)] * NG] * 2
    h_ctx = rglru(crx_ref, nc, zero)
    if need_ctx_out:
        rg_out(crg_ref, ycr_ref, nc)
        conv_mixer(cab_ref, cac_ref, cax_ref, yca_ref, nc)
    rglru(rx_ref, n, h_ctx)
    rg_out(rg_ref, yr_ref, n)
    conv_mixer(ab_ref, ac_ref, ax_ref, ya_ref, n)


def _seq_call(z, zc, zc_off, p, need_ctx_out):
    bsz, n, _ = z.shape
    nc = zc.shape[1]
    u = HALF // LANES

    def zspec(rows, off):
        return pl.BlockSpec((1, rows, HALF), lambda b, c: (b, 0, off // u + c))

    names = ("a_b", "a_c", "a_x", "r_g", "r_x")
    in_specs = [zspec(n, Z_FULL[k]) for k in names]
    args = [z] * 5
    if need_ctx_out:
        in_specs += [zspec(nc, zc_off[k]) for k in names[:4]]
        args += [zc] * 4
    in_specs.append(zspec(nc, zc_off["r_x"]))
    args.append(zc)
    in_specs += [
        pl.BlockSpec((3, HALF), lambda b, c: (0, c)),
        pl.BlockSpec((2, 4, HALF), lambda b, c: (0, 0, c)),
        pl.BlockSpec((2, 1, HALF), lambda b, c: (0, 0, c)),
        pl.BlockSpec((2, 1, HALF, 2 * HALF), lambda b, c: (0, c, 0, 0)),
        pl.BlockSpec((2, 1, HALF), lambda b, c: (0, 0, c)),
        pl.BlockSpec((2, 1, HALF), lambda b, c: (0, 0, c)),
        pl.BlockSpec((2, 1, HALF), lambda b, c: (0, 0, c)),
    ]
    args += [p["conv_a_w"], p["rg_conv_w"], p["rg_conv_b"], p["rg_wg"], p["rg_ba"], p["rg_bx"], p["rg_lambda"]]
    out_specs = [pl.BlockSpec((1, n, HALF), lambda b, c: (b, 0, c))] * 2
    out_shape = [jax.ShapeDtypeStruct((bsz, n, CONV_W), BF16), jax.ShapeDtypeStruct((bsz, n, RG_W), BF16)]
    if need_ctx_out:
        out_specs += [pl.BlockSpec((1, nc, HALF), lambda b, c: (b, 0, c))] * 2
        out_shape += [jax.ShapeDtypeStruct((bsz, nc, CONV_W), BF16), jax.ShapeDtypeStruct((bsz, nc, RG_W), BF16)]
    npad = max(n, nc) + 2 * PAD
    return pl.pallas_call(
        functools.partial(_seq_kernel, n=n, nc=nc, need_ctx_out=need_ctx_out),
        grid=(bsz, CONV_W // HALF),
        in_specs=in_specs,
        out_specs=out_specs,
        out_shape=out_shape,
        scratch_shapes=[pltpu.VMEM((npad, HALF), F32), pltpu.VMEM((2, u, max(n, nc), LANES), F32),
                        pltpu.VMEM((2, u, max(n, nc), LANES), F32)],
        compiler_params=_cparams(("parallel", "parallel")),
        name="seq_mixers",
    )(*args)


def _attn_kernel(*refs, has_lat, lam_init):
    if has_lat:
        q_ref, k_ref, v_ref, ck_ref, cv_ref, lam_ref, sw_ref, o_ref, k_all, v_ext = refs
    else:
        q_ref, ck_ref, cv_ref, lam_ref, sw_ref, o_ref, k_all, v_ext = refs
    nc = ck_ref.shape[1]
    nk = k_all.shape[0]

    @pl.when(pl.program_id(1) == 0)
    def _():
        k_all[0:nc, :] = ck_ref[0]
        if has_lat:
            k_all[nc:nk, :] = k_ref[0]
        for h in range(DA_HEADS):
            sl = slice(h * LANES, (h + 1) * LANES)
            v_ext[h, 0:nc, 0:LANES] = cv_ref[0, :, sl]
            if has_lat:
                v_ext[h, nc:nk, 0:LANES] = v_ref[0, :, sl]
            v_ext[h, :, LANES:2 * LANES] = jnp.ones((nk, LANES), v_ext.dtype)

    lane = lax.broadcasted_iota(jnp.int32, (1, LANES), 1)
    map1 = (lane % DA_HEAD_DIM) < (DA_HEAD_DIM // 2)
    dn = (((1,), (1,)), ((), ()))
    lam = lam_ref[0:1, 0:1]

    for h in range(DA_HEADS):
        sl = slice(h * LANES, (h + 1) * LANES)
        q = q_ref[0, :, sl]
        zero = jnp.zeros_like(q)

        def soft_v(qm):
            s = lax.dot_general(qm, k_all[:, sl], dn, preferred_element_type=F32)
            e = jnp.exp2(s - jnp.max(s, axis=-1, keepdims=True)).astype(BF16)
            r = jnp.dot(e, v_ext[h], preferred_element_type=F32)
            return r[:, :LANES] / r[:, LANES:LANES + 1]

        o = soft_v(jnp.where(map1, q, zero)) - lam * soft_v(jnp.where(map1, zero, q))
        y = _rms(o) * sw_ref[...] * (1.0 - lam_init)
        o_ref[0, :, sl] = y.astype(o_ref.dtype)


def _attn_call(zq, z, zc, zc_off, lam, subw, lam_init, has_lat):
    bsz, nq, _ = zq.shape
    nc = zc.shape[1]
    tq = min(ATTN_Q_ROWS, nq)
    w = DA_HEADS * LANES
    u = DA_HEADS
    in_specs = [pl.BlockSpec((1, tq, w), lambda b, i: (b, i, Z_FULL["q"] // u))]
    args = [zq]
    nk = nc
    if has_lat:
        n = z.shape[1]
        nk = nc + n
        in_specs += [pl.BlockSpec((1, n, w), lambda b, i: (b, 0, Z_FULL["k"] // u)),
                     pl.BlockSpec((1, n, w), lambda b, i: (b, 0, Z_FULL["v"] // u))]
        args += [z, z]
    in_specs += [pl.BlockSpec((1, nc, w), lambda b, i: (b, 0, zc_off["k"] // u)),
                 pl.BlockSpec((1, nc, w), lambda b, i: (b, 0, zc_off["v"] // u)),
                 pl.BlockSpec((8, LANES), lambda b, i: (0, 0)),
                 pl.BlockSpec((1, LANES), lambda b, i: (0, 0))]
    args += [zc, zc, lam, subw]
    return pl.pallas_call(
        functools.partial(_attn_kernel, has_lat=has_lat, lam_init=lam_init),
        grid=(bsz, nq // tq),
        in_specs=in_specs,
        out_specs=pl.BlockSpec((1, tq, w), lambda b, i: (b, i, 0)),
        out_shape=jax.ShapeDtypeStruct((bsz, nq, DA_HEADS * DA_V_DIM), BF16),
        scratch_shapes=[pltpu.VMEM((nk, w), BF16), pltpu.VMEM((DA_HEADS, nk, 2 * LANES), BF16)],
        compiler_params=_cparams(("parallel", "arbitrary")),
        name="diff_attn",
    )(*args)


def _merge_kernel(*refs, route):
    if route:
        (ya_ref, yr_ref, yd_ref, ga_ref, gr_ref, gd_ref, x_ref, g1_ref, nw_ref, sh_ref, sc_ref,
         wb_ref, wo_ref, rw_ref, xo_ref, h_ref, rt_ref) = refs
    else:
        (ya_ref, yr_ref, yd_ref, ga_ref, gr_ref, gd_ref, x_ref, g1_ref, nw_ref, sh_ref, sc_ref,
         wb_ref, wo_ref, xo_ref, h_ref) = refs
    m = None
    for i, (y_ref, g_ref) in enumerate(((ya_ref, ga_ref), (yr_ref, gr_ref), (yd_ref, gd_ref))):
        t = _sigmoid(g_ref[0].astype(F32)) * jnp.dot(y_ref[0], wb_ref[i], preferred_element_type=F32)
        m = t if m is None else m + t
    out = jnp.dot(m.astype(BF16), wo_ref[...], preferred_element_type=F32)
    xn = x_ref[0] + g1_ref[0] * out
    xo_ref[0] = xn
    h = (_rms(xn) * nw_ref[...]) * (1.0 + sc_ref[0]) + sh_ref[0]
    h_ref[0] = h.astype(h_ref.dtype)
    if route:
        h_hi = h.astype(BF16)
        h_lo = (h - h_hi.astype(F32)).astype(BF16)
        logits = (jnp.dot(h_hi, rw_ref[0], preferred_element_type=F32)
                  + jnp.dot(h_lo, rw_ref[0], preferred_element_type=F32)
                  + jnp.dot(h_hi, rw_ref[1], preferred_element_type=F32))
        lane = lax.broadcasted_iota(jnp.int32, logits.shape, 1)
        neg = jnp.float32(-jnp.inf)
        l1 = jnp.where(lane < N_EXPERTS, logits, neg)
        m1 = jnp.max(l1, axis=-1, keepdims=True)
        i1 = jnp.min(jnp.where(l1 == m1, lane, LANES), axis=-1, keepdims=True)
        l2 = jnp.where(lane == i1, neg, l1)
        m2 = jnp.max(l2, axis=-1, keepdims=True)
        i2 = jnp.min(jnp.where(l2 == m2, lane, LANES), axis=-1, keepdims=True)
        ex = jnp.exp(m2 - m1)
        gt1 = 1.0 / (1.0 + ex)
        gt2 = ex * gt1
        rt = jnp.where(lane == 0, i1.astype(F32),
                       jnp.where(lane == 1, i2.astype(F32),
                                 jnp.where(lane == 2, gt1, jnp.where(lane == 3, gt2, 0.0))))
        rt_ref[0] = rt


def _merge_call(ya, yr, yd, z, x, mod3, mod_row, nw2, wb, wo, router_w=None):
    bsz, n, d = x.shape
    tm = min(512, n)
    route = router_w is not None
    tok = lambda b, i: (b, i, 0)
    gu = d // LANES
    in_specs = [
        pl.BlockSpec((1, tm, CONV_W), tok), pl.BlockSpec((1, tm, RG_W), tok),
        pl.BlockSpec((1, tm, DA_HEADS * DA_V_DIM), tok),
        pl.BlockSpec((1, tm, d), lambda b, i: (b, i, Z_FULL["g_a"] // gu)),
        pl.BlockSpec((1, tm, d), lambda b, i: (b, i, Z_FULL["g_r"] // gu)),
        pl.BlockSpec((1, tm, d), lambda b, i: (b, i, Z_FULL["g_d"] // gu)),
        pl.BlockSpec((1, tm, d), tok),
        pl.BlockSpec((1, 1, d), lambda b, i: (mod_row(b), 0, 2)),
        pl.BlockSpec((1, d), lambda b, i: (0, 0)),
        pl.BlockSpec((1, 1, d), lambda b, i: (mod_row(b), 0, 3)),
        pl.BlockSpec((1, 1, d), lambda b, i: (mod_row(b), 0, 4)),
        pl.BlockSpec(wb.shape, lambda b, i: (0, 0, 0)),
        pl.BlockSpec(wo.shape, lambda b, i: (0, 0)),
    ]
    args = [ya, yr, yd, z, z, z, x, mod3, nw2, mod3, mod3, wb, wo]
    out_specs = [pl.BlockSpec((1, tm, d), tok), pl.BlockSpec((1, tm, d), tok)]
    out_shape = [jax.ShapeDtypeStruct((bsz, n, d), F32),
                 jax.ShapeDtypeStruct((bsz, n, d), F32 if route else BF16)]
    if route:
        in_specs.append(pl.BlockSpec(router_w.shape, lambda b, i: (0, 0, 0)))
        args.append(router_w)
        out_specs.append(pl.BlockSpec((1, tm, LANES), tok))
        out_shape.append(jax.ShapeDtypeStruct((bsz, n, LANES), F32))
    return pl.pallas_call(
        functools.partial(_merge_kernel, route=route),
        grid=(bsz, n // tm),
        in_specs=in_specs,
        out_specs=out_specs,
        out_shape=out_shape,
        compiler_params=_cparams(("parallel", "parallel")),
        name="merge",
    )(*args)


def _ffn_kernel(h_ref, x_ref, g2_ref, wg_ref, wu_ref, w2_ref, o_ref, acc_ref):
    f = pl.program_id(2)
    h = h_ref[0]
    g = jnp.dot(h, wg_ref[...], preferred_element_type=F32)
    u = jnp.dot(h, wu_ref[...], preferred_element_type=F32)
    act = (_silu(g) * u).astype(BF16)
    part = jnp.dot(act, w2_ref[...], preferred_element_type=F32)

    @pl.when(f == 0)
    def _():
        acc_ref[...] = part

    @pl.when(f > 0)
    def _():
        acc_ref[...] += part

    @pl.when(f == pl.num_programs(2) - 1)
    def _():
        o_ref[0] = x_ref[0] + g2_ref[0] * acc_ref[...]


def _ffn_call(h, x, mod3, mod_row, w13, w2):
    bsz, n, d = x.shape
    ff = w2.shape[0]
    tm = min(SWIGLU_ROWS, n)
    tf = SWIGLU_FF_TILE
    nf = ff // tf
    tok = lambda b, i, f: (b, i, 0)
    return pl.pallas_call(
        _ffn_kernel,
        grid=(bsz, n // tm, nf),
        in_specs=[
            pl.BlockSpec((1, tm, d), tok),
            pl.BlockSpec((1, tm, d), tok),
            pl.BlockSpec((1, 1, d), lambda b, i, f: (mod_row(b), 0, 5)),
            pl.BlockSpec((d, tf), lambda b, i, f: (0, f)),
            pl.BlockSpec((d, tf), lambda b, i, f: (0, nf + f)),
            pl.BlockSpec((tf, d), lambda b, i, f: (f, 0)),
        ],
        out_specs=pl.BlockSpec((1, tm, d), tok),
        out_shape=jax.ShapeDtypeStruct((bsz, n, d), F32),
        scratch_shapes=[pltpu.VMEM((tm, d), F32)],
        compiler_params=_cparams(("parallel", "parallel", "arbitrary")),
        name="ffn",
    )(h, x, mod3, w13, w13, w2)


def _moe_kernel(be_ref, nu_ref, x_ref, wg_ref, wu_ref, w2_ref, o_ref, xb):
    i = pl.program_id(0)
    f = pl.program_id(1)
    used = i < nu_ref[0]

    @pl.when(used & (f == 0))
    def _():
        xb[...] = x_ref[...].astype(BF16)

    @pl.when(jnp.logical_not(used) & (f == 0))
    def _():
        o_ref[...] = jnp.zeros(o_ref.shape, o_ref.dtype)

    @pl.when(used)
    def _():
        x = xb[...]
        g = jnp.dot(x, wg_ref[0], preferred_element_type=F32)
        u = jnp.dot(x, wu_ref[0], preferred_element_type=F32)
        act = (_silu(g) * u).astype(BF16)
        part = jnp.dot(act, w2_ref[0], preferred_element_type=F32)

        @pl.when(f == 0)
        def _():
            o_ref[...] = part

        @pl.when(f > 0)
        def _():
            o_ref[...] += part


def _moe_call(xs, block_e, n_used, w13, w2, rows):
    n_rows, d = xs.shape
    n_blocks = n_rows // rows
    ff = w2.shape[1]
    tf = SWIGLU_FF_TILE
    nf = ff // tf

    def bi(i, nu):
        return jnp.minimum(i, nu[0] - 1)

    def fi(i, f, nu):
        return jnp.where(i < nu[0], f, nf - 1)

    grid_spec = pltpu.PrefetchScalarGridSpec(
        num_scalar_prefetch=2,
        grid=(n_blocks, nf),
        in_specs=[
            pl.BlockSpec((rows, d), lambda i, f, be, nu: (bi(i, nu), 0)),
            pl.BlockSpec((1, d, tf), lambda i, f, be, nu: (be[i], 0, fi(i, f, nu))),
            pl.BlockSpec((1, d, tf), lambda i, f, be, nu: (be[i], 0, nf + fi(i, f, nu))),
            pl.BlockSpec((1, tf, d), lambda i, f, be, nu: (be[i], fi(i, f, nu), 0)),
        ],
        out_specs=pl.BlockSpec((rows, d), lambda i, f, be, nu: (i, 0)),
        scratch_shapes=[pltpu.VMEM((rows, d), BF16)],
    )
    return pl.pallas_call(
        _moe_kernel,
        grid_spec=grid_spec,
        out_shape=jax.ShapeDtypeStruct((n_rows, d), F32),
        compiler_params=_cparams(("arbitrary", "arbitrary")),
        name="moe",
    )(block_e, n_used, xs, w13, w13, w2)


GROUP_PAD = 8
GROUP_SIZES = tuple(SWIGLU_ROWS >> s for s in range(SWIGLU_ROWS.bit_length()) if SWIGLU_ROWS >> s >= GROUP_PAD)


def _sorted_rows(tm):
    return 2 * tm + N_EXPERTS * GROUP_PAD


def _group_copies(step, src_of, dst_of, loc_ref, base_ref, pc_ref, sem, sizes):
    for e in range(N_EXPERTS):
        loc = loc_ref[step * N_EXPERTS + e]
        base = base_ref[step * N_EXPERTS + e]
        pc = pc_ref[step * N_EXPERTS + e]
        for size in sizes:
            before = pc & (-2 * size)
            src = src_of(pl.multiple_of(loc + before, GROUP_PAD), pl.multiple_of(base + before, GROUP_PAD), size)
            dst = dst_of(pl.multiple_of(loc + before, GROUP_PAD), pl.multiple_of(base + before, GROUP_PAD), size)
            yield (pc & size) != 0, pltpu.make_async_copy(src, dst, sem)


def _dispatch2_kernel(loc_ref, base_ref, pc_ref, tail_ref, nu_ref, pos_ref, h_ref, xs_hbm, srt, zeros, sem, zsem,
                      *, tm, nt, rows, sizes):
    i = pl.program_id(0)
    n = pl.num_programs(0)
    slot = i % 2
    sr = srt.shape[1]
    n_blocks = xs_hbm.shape[0] // rows
    min_used = 2 * tm * nt // rows

    def tail_copies():
        for e in range(N_EXPERTS):
            start = tail_ref[e]
            length = tail_ref[N_EXPERTS + e]
            for size in sizes[1:]:
                before = length & (-2 * size)
                dst = xs_hbm.at[pl.ds(pl.multiple_of(start + before, GROUP_PAD), size)]
                yield (length & size) != 0, pltpu.make_async_copy(zeros.at[pl.ds(0, size)], dst, zsem)
        for b in range(min_used, n_blocks):
            dst = xs_hbm.at[pl.ds(b * rows, rows)]
            yield b >= nu_ref[0], pltpu.make_async_copy(zeros, dst, zsem)

    @pl.when(i == 0)
    def _():
        zeros[...] = jnp.zeros(zeros.shape, zeros.dtype)
        for pred, cp in tail_copies():
            pl.when(pred)(cp.start)

    r = lax.broadcasted_iota(jnp.int32, (sr, tm), 0)
    hit = (r == pos_ref[0, 0:1, :]) | (r == pos_ref[0, 1:2, :])
    sel = jnp.where(hit, 1.0, 0.0).astype(BF16)
    srt[slot] = jnp.dot(sel, h_ref[...].astype(BF16), preferred_element_type=F32)

    def group_copies(step, s):
        return _group_copies(step, lambda lo, ba, sz: srt.at[s, pl.ds(lo, sz)],
                             lambda lo, ba, sz: xs_hbm.at[pl.ds(ba, sz)], loc_ref, base_ref, pc_ref, sem.at[s], sizes)

    for pred, cp in group_copies(i, slot):
        pl.when(pred)(cp.start)

    @pl.when(i > 0)
    def _():
        for pred, cp in group_copies(i - 1, 1 - slot):
            pl.when(pred)(cp.wait)

    @pl.when(i == n - 1)
    def _():
        for pred, cp in group_copies(i, slot):
            pl.when(pred)(cp.wait)
        for pred, cp in tail_copies():
            pl.when(pred)(cp.wait)


def _dispatch2_call(h2, plan, tm, rows):
    t, d = h2.shape
    nt = t // tm
    sr = _sorted_rows(tm)
    sizes = tuple(s for s in GROUP_SIZES if s <= tm)
    grid_spec = pltpu.PrefetchScalarGridSpec(
        num_scalar_prefetch=5,
        grid=(nt,),
        in_specs=[pl.BlockSpec((1, 2, tm), lambda i, *_: (i, 0, 0)),
                  pl.BlockSpec((tm, d), lambda i, *_: (i, 0))],
        out_specs=pl.BlockSpec(memory_space=pl.ANY),
        scratch_shapes=[pltpu.VMEM((2, sr, d), F32), pltpu.VMEM((rows, d), F32),
                        pltpu.SemaphoreType.DMA((2,)), pltpu.SemaphoreType.DMA(())],
    )
    return pl.pallas_call(
        functools.partial(_dispatch2_kernel, tm=tm, nt=nt, rows=rows, sizes=sizes),
        grid_spec=grid_spec,
        out_shape=jax.ShapeDtypeStruct((plan["n_blocks"] * rows, d), F32),
        compiler_params=pltpu.CompilerParams(dimension_semantics=("arbitrary",), vmem_limit_bytes=VMEM_LIMIT),
        name="moe_dispatch",
    )(plan["loc"], plan["base"], plan["pc"], plan["tail"], plan["n_used"], plan["pos_rows"], h2)


def _combine2_kernel(loc_ref, base_ref, pc_ref, x_ref, pos_ref, rt_ref, g2_ref, fw_ref, ys_hbm, o_ref, buf, sem,
                     *, tm, sizes):
    i = pl.program_id(0)
    n = pl.num_programs(0)
    slot = i % 2
    sr = buf.shape[1]

    def group_copies(step, s):
        return _group_copies(step, lambda lo, ba, sz: ys_hbm.at[pl.ds(ba, sz)],
                             lambda lo, ba, sz: buf.at[s, pl.ds(lo, sz)], loc_ref, base_ref, pc_ref, sem.at[s], sizes)

    @pl.when(i == 0)
    def _():
        buf[...] = jnp.zeros(buf.shape, buf.dtype)
        for pred, cp in group_copies(0, 0):
            pl.when(pred)(cp.start)

    @pl.when(i + 1 < n)
    def _():
        for pred, cp in group_copies(i + 1, 1 - slot):
            pl.when(pred)(cp.start)

    for pred, cp in group_copies(i, slot):
        pl.when(pred)(cp.wait)

    rt = rt_ref[...]
    col = lax.broadcasted_iota(jnp.int32, (tm, sr), 1)
    pos = pos_ref[...]
    w = jnp.where(col == pos[:, 0:1], rt[:, 2:3], jnp.where(col == pos[:, 1:2], rt[:, 3:4], 0.0))
    y = jnp.dot(w.astype(BF16), buf[slot].astype(BF16), preferred_element_type=F32)
    xn = x_ref[...] + g2_ref[0] * y
    o_ref[...] = _rms(xn) * fw_ref[...]


def _combine2_call(x2, ys, plan, route2, mod3, n_per_seq, fw, tm):
    t, d = x2.shape
    nt = t // tm
    per = n_per_seq // tm
    sr = _sorted_rows(tm)
    sizes = tuple(s for s in GROUP_SIZES if s <= tm)
    grid_spec = pltpu.PrefetchScalarGridSpec(
        num_scalar_prefetch=3,
        grid=(nt,),
        in_specs=[
            pl.BlockSpec((tm, d), lambda i, *_: (i, 0)),
            pl.BlockSpec((tm, 2), lambda i, *_: (i, 0)),
            pl.BlockSpec((tm, LANES), lambda i, *_: (i, 0)),
            pl.BlockSpec((1, 1, d), lambda i, *_: (i // per, 0, 5)),
            pl.BlockSpec((1, d), lambda i, *_: (0, 0)),
            pl.BlockSpec(memory_space=pl.ANY),
        ],
        out_specs=pl.BlockSpec((tm, d), lambda i, *_: (i, 0)),
        scratch_shapes=[pltpu.VMEM((2, sr, d), F32), pltpu.SemaphoreType.DMA((2,))],
    )
    return pl.pallas_call(
        functools.partial(_combine2_kernel, tm=tm, sizes=sizes),
        grid_spec=grid_spec,
        out_shape=jax.ShapeDtypeStruct((t, d), F32),
        compiler_params=pltpu.CompilerParams(dimension_semantics=("arbitrary",), vmem_limit_bytes=VMEM_LIMIT),
        name="combine_final",
    )(plan["loc"], plan["base"], plan["pc"], x2, plan["pos_cols"], route2, mod3, fw, ys)


def _sort_plan(route, tm, rows):
    t = route.shape[0]
    nt = t // tm
    experts = jnp.arange(N_EXPERTS, dtype=jnp.int32)
    e = route[:, :2].astype(jnp.int32).reshape(nt, 2 * tm)
    hot = (e[:, :, None] == experts[None, None, :]).astype(jnp.int32)
    cs = jnp.cumsum(hot, axis=1)
    rank = jnp.sum(cs * hot, axis=2) - 1
    cnt = cs[:, -1, :]
    pc = (cnt + GROUP_PAD - 1) // GROUP_PAD * GROUP_PAD
    loc = jnp.cumsum(pc, axis=1) - pc
    pos = jnp.sum(hot * loc[:, None, :], axis=2) + rank
    tot = jnp.sum(pc, axis=0)
    padded = (tot + rows - 1) // rows * rows
    e_end = jnp.cumsum(padded)
    e_start = e_end - padded
    base = e_start[None, :] + jnp.cumsum(pc, axis=0) - pc
    n_blocks = -(-(2 * t + nt * N_EXPERTS * (GROUP_PAD - 1)) // rows) + N_EXPERTS
    block_start = jnp.arange(n_blocks, dtype=jnp.int32) * rows
    block_e = jnp.minimum(jnp.sum((block_start[:, None] >= e_end[None, :]).astype(jnp.int32), axis=1),
                          N_EXPERTS - 1)
    pos3 = pos.reshape(nt, tm, 2)
    i32 = lambda a: a.astype(jnp.int32)
    return dict(
        loc=i32(loc.reshape(-1)), base=i32(base.reshape(-1)), pc=i32(pc.reshape(-1)),
        tail=i32(jnp.concatenate([e_start + tot, padded - tot])),
        n_used=i32(e_end[-1] // rows).reshape(1), block_e=i32(block_e), n_blocks=n_blocks,
        pos_rows=i32(jnp.swapaxes(pos3, 1, 2)), pos_cols=i32(pos3.reshape(t, 2)))


def _norm_kernel(x_ref, fw_ref, o_ref):
    o_ref[0] = _rms(x_ref[0]) * fw_ref[...]


def _norm_call(x, fw):
    bsz, n, d = x.shape
    tm = min(512, n)
    tok = lambda b, i: (b, i, 0)
    return pl.pallas_call(
        _norm_kernel,
        grid=(bsz, n // tm),
        in_specs=[pl.BlockSpec((1, tm, d), tok), pl.BlockSpec((1, d), lambda b, i: (0, 0))],
        out_specs=pl.BlockSpec((1, tm, d), tok),
        out_shape=jax.ShapeDtypeStruct((bsz, n, d), F32),
        compiler_params=_cparams(("parallel", "parallel")),
        name="final_norm",
    )(x, fw)


def _rope_perm():
    perm = np.zeros((LANES,), np.int32)
    half = ROPE_AXIS_DIM // 2
    for i in range(LANES // 2):
        comp, axis, j = i // 32, (i % 32) // half, i % half
        perm[i] = comp * DA_HEAD_DIM + axis * ROPE_AXIS_DIM + j
        perm[LANES // 2 + i] = perm[i] + half
    return perm


def _rope_tables(n):
    rows = n // GRID_W
    pos_r = jnp.repeat(jnp.arange(rows, dtype=F32), GRID_W)
    pos_c = jnp.broadcast_to(jnp.arange(GRID_W, dtype=F32), (rows, GRID_W)).reshape(-1)
    inv_freq = ROPE_BASE ** (-jnp.arange(0, ROPE_AXIS_DIM, 2, dtype=F32) / ROPE_AXIS_DIM)
    ang = jnp.stack([pos_r[:, None] * inv_freq, pos_c[:, None] * inv_freq], axis=1)
    cos, sin = jnp.cos(ang), jnp.sin(ang)
    half = jnp.concatenate([cos.reshape(n, 32), cos.reshape(n, 32)], axis=1)
    cos_t = jnp.concatenate([half, half], axis=1)
    sh = jnp.concatenate([sin.reshape(n, 32), sin.reshape(n, 32)], axis=1)
    sin_t = jnp.concatenate([-sh, sh], axis=1)
    return cos_t, sin_t


def _blockdiag_gate(wa, wx):
    per = HALF // (RG_W // RG_BLOCKS)
    eye = jnp.eye(per, dtype=wa.dtype)

    def bd(w):
        w4 = w.reshape(2, RG_BLOCKS // per, per, w.shape[-2], w.shape[-1])
        return jnp.einsum("dhbij,bc->dhbicj", w4, eye).reshape(2, RG_BLOCKS // per, HALF, HALF)

    return jnp.concatenate([bd(wa), bd(wx)], axis=-1).astype(BF16)


def kernel(x, c, ctx, c_ctx, mod_w, mod_b, norm1_w, norm2_w, w_in, conv_a_w, rg_conv_w, rg_conv_b, rg_wa, rg_ba, rg_wx, rg_bx, rg_lambda, da_lambda, da_subln_w, w_branch, w_out, ffn_w13, ffn_w2, router_w, moe_w13, moe_w2, final_norm_w):
    bsz, n, d = x.shape
    nc = ctx.shape[1]
    depth = mod_w.shape[0]
    cos_t, sin_t = _rope_tables(n)
    perm = _rope_perm()
    ref_cols = {name: np.arange(start, start + width, dtype=np.int32) for name, start, width in REF_COLS}
    for name in ("q", "k"):
        for h in range(DA_HEADS):
            base = ref_cols[name][0] + h * LANES
            ref_cols[name][h * LANES:(h + 1) * LANES] = base + perm
    z_cols = np.concatenate([ref_cols[name] for name in Z_ORDER])

    bp = -(-(bsz + 1) // 8) * 8
    cc = jnp.zeros((bp, d), F32).at[:bsz].set(c).at[bsz].set(c_ctx)
    lat_row = lambda b: b
    ctx_row = lambda b: bsz
    cx = ctx
    moe_rows = SWIGLU_ROWS if bsz * n >= 8192 else 128

    for l in range(depth):
        last = l == depth - 1
        lam_init = 0.8 - 0.6 * math.exp(-0.3 * l)
        j = l // 2
        mod, lam = _mod_call(cc, mod_w[l], mod_b[l][None, :], da_lambda[l], lam_init)
        mod3 = mod.reshape(bp, 1, N_MOD * d)
        w_in_l = w_in[l][:, z_cols].astype(BF16)
        nw1 = norm1_w[l][None, :]
        nw2 = norm2_w[l][None, :]
        p = dict(
            conv_a_w=conv_a_w[l], rg_conv_w=rg_conv_w[l], rg_conv_b=rg_conv_b[l][:, None, :],
            rg_wg=_blockdiag_gate(rg_wa[l], rg_wx[l]),
            rg_ba=rg_ba[l].reshape(2, 1, RG_W), rg_bx=rg_bx[l].reshape(2, 1, RG_W),
            rg_lambda=rg_lambda[l][:, None, :])
        subw = da_subln_w[l][None, :]
        wb = w_branch[l].astype(BF16)
        wo = w_out[l].astype(BF16)

        ncol = w_in.shape[2] // COL
        z = _in_call(x, nw1, mod3, lat_row, w_in_l, range(ncol), True, cos_t, sin_t)
        flat = lambda a: a.reshape((1, bsz * nc) + a.shape[2:])
        unflat = lambda a: a.reshape((bsz, nc) + a.shape[2:])
        if last:
            zc = unflat(_in_call(flat(cx), nw1, mod3, ctx_row, w_in_l, CTX_LAST_BLOCKS, False))
            zc_off = Z_CTX_LAST
        else:
            zc = unflat(_in_call(flat(cx), nw1, mod3, ctx_row, w_in_l, range(ncol), False))
            zc_off = Z_FULL

        seq_out = _seq_call(z, zc, zc_off, p, not last)
        yd = _attn_call(z, z, zc, zc_off, lam, subw, lam_init, True)
        moe_layer = l % 2 == 1
        if moe_layer:
            rw32 = jnp.zeros((d, LANES), F32).at[:, :N_EXPERTS].set(router_w[j])
            rw_hi = rw32.astype(BF16)
            rw = jnp.stack([rw_hi, (rw32 - rw_hi.astype(F32)).astype(BF16)])
            xn, h, route = _merge_call(seq_out[0], seq_out[1], yd, z, x, mod3, lat_row, nw2, wb, wo, rw)
            t = bsz * n
            tm = min(512, n)
            route2 = route.reshape(t, LANES)
            plan = _sort_plan(route2, tm, moe_rows)
            xs = _dispatch2_call(h.reshape(t, d), plan, tm, moe_rows)
            ys = _moe_call(xs, plan["block_e"], plan["n_used"], moe_w13[j].astype(BF16),
                           moe_w2[j].astype(BF16), moe_rows)
            if last:
                out = _combine2_call(xn.reshape(t, d), ys, plan, route2, mod3, n, final_norm_w[None, :], tm)
                return out.reshape(bsz, n, d)
            raise NotImplementedError("expert layer followed by another layer")
        xn, h = _merge_call(seq_out[0], seq_out[1], yd, z, x, mod3, lat_row, nw2, wb, wo)
        w13 = ffn_w13[j].astype(BF16)
        w2 = ffn_w2[j].astype(BF16)
        x = _ffn_call(h, xn, mod3, lat_row, w13, w2)
        if not last:
            ycd = _attn_call(zc, None, zc, zc_off, lam, subw, lam_init, False)
            cxn, hc = _merge_call(flat(seq_out[2]), flat(seq_out[3]), flat(ycd), flat(zc), flat(cx), mod3,
                                  ctx_row, nw2, wb, wo)
            cx = unflat(_ffn_call(hc, cxn, mod3, ctx_row, w13, w2))
    return _norm_call(x, final_norm_w[None, :])
```

```python
import functools
import math

import jax
import jax.numpy as jnp
import numpy as np
from jax import lax
from jax.experimental import pallas as pl
from jax.experimental.pallas import tpu as pltpu

F32 = jnp.float32
BF16 = jnp.bfloat16

EPS = 1e-6
N_MOD = 6
GRID_W = 64
CONV_W = 512
RG_W = 512
RG_BLOCKS = 8
RG_C = 8.0
DA_HEADS = 4
DA_HEAD_DIM = 64
DA_V_DIM = 128
ROPE_AXIS_DIM = 32
ROPE_BASE = 10000.0
N_EXPERTS = 8
LANES = 128
HALF = 256
VMEM_LIMIT = 56 * 1024 * 1024
IN_PROJ_ROWS = 2048
ATTN_Q_ROWS = 512
SWIGLU_ROWS = 512
SWIGLU_FF_TILE = 1792

COL = 1024
REF_COLS = (("a_b", 0, 512), ("a_c", 512, 512), ("a_x", 1024, 512), ("r_g", 1536, 512), ("r_x", 2048, 512),
            ("q", 2560, 512), ("k", 3072, 512), ("v", 3584, 512), ("g_a", 4096, 1024), ("g_r", 5120, 1024),
            ("g_d", 6144, 1024))
Z_ORDER = ("q", "k", "v", "a_b", "a_c", "a_x", "r_g", "r_x", "g_a", "g_r", "g_d")
Z_FULL = dict(q=0, k=4, v=8, a_b=12, a_c=16, a_x=20, r_g=24, r_x=28, g_a=32, g_r=40, g_d=48)
CTX_LAST_BLOCKS = (0, 1, 3)
Z_CTX_LAST = dict(k=4, v=8, r_x=20)
LOG2E = 1.4426950408889634


def _cparams(sem, vmem=VMEM_LIMIT):
    return pltpu.CompilerParams(dimension_semantics=sem, vmem_limit_bytes=vmem)


def _sigmoid(x):
    return 1.0 / (1.0 + jnp.exp2(x * (-LOG2E)))


def _silu(x):
    return x * _sigmoid(x)


def _gelu_tanh(x):
    return 0.5 * x * (1.0 + jnp.tanh(math.sqrt(2.0 / math.pi) * (x + 0.044715 * (x * x * x))))


def _rms(x):
    return x * lax.rsqrt(jnp.mean(x * x, axis=-1, keepdims=True) + EPS)


def _mod_kernel(cc_ref, w_ref, b_ref, dl_ref, mod_ref, lam_ref, *, lam_init):
    cc = cc_ref[...]
    s = _silu(cc)
    mod_ref[...] = jnp.dot(s, w_ref[...], preferred_element_type=F32,
                           precision=lax.Precision.HIGHEST) + b_ref[...]
    dl = dl_ref[...]
    s1 = jnp.sum(dl[0:1] * dl[1:2], axis=-1, keepdims=True)
    s2 = jnp.sum(dl[2:3] * dl[3:4], axis=-1, keepdims=True)
    lam = jnp.exp(s1) - jnp.exp(s2) + lam_init
    lam_ref[...] = jnp.broadcast_to(lam, lam_ref.shape)


def _mod_call(cc, w, b, dl, lam_init):
    bp, d = cc.shape
    nd = w.shape[1]
    tn = 1024
    return pl.pallas_call(
        functools.partial(_mod_kernel, lam_init=lam_init),
        grid=(nd // tn,),
        in_specs=[
            pl.BlockSpec((bp, d), lambda j: (0, 0)),
            pl.BlockSpec((d, tn), lambda j: (0, j)),
            pl.BlockSpec((1, tn), lambda j: (0, j)),
            pl.BlockSpec(dl.shape, lambda j: (0, 0)),
        ],
        out_specs=[
            pl.BlockSpec((bp, tn), lambda j: (0, j)),
            pl.BlockSpec((8, LANES), lambda j: (0, 0)),
        ],
        out_shape=[jax.ShapeDtypeStruct((bp, nd), F32), jax.ShapeDtypeStruct((8, LANES), F32)],
        compiler_params=_cparams(("arbitrary",)),
        name="mod",
    )(cc, w, b, dl)


def _in_kernel(*refs, rope):
    if rope:
        x_ref, nw_ref, sh_ref, sc_ref, cos_ref, sin_ref, w_ref, o_ref, u_scr = refs
    else:
        x_ref, nw_ref, sh_ref, sc_ref, w_ref, o_ref, u_scr = refs
    j = pl.program_id(2)

    @pl.when(j == 0)
    def _():
        y = _rms(x_ref[0]) * nw_ref[...]
        u_scr[...] = (y * (1.0 + sc_ref[0]) + sh_ref[0]).astype(BF16)

    acc = jnp.dot(u_scr[...], w_ref[...], preferred_element_type=F32)
    q_scale = DA_HEAD_DIM ** -0.5 * LOG2E
    n_q = DA_HEADS * 2 * DA_HEAD_DIM // LANES

    @pl.when(j == 0)
    def _():
        for g in range(acc.shape[1] // LANES):
            r = acc[:, g * LANES:(g + 1) * LANES]
            if rope:
                r = r * cos_ref[...] + pltpu.roll(r, LANES // 2, 1) * sin_ref[...]
            if g < n_q:
                r = r * q_scale
            o_ref[0, :, g * LANES:(g + 1) * LANES] = r.astype(o_ref.dtype)

    @pl.when(j != 0)
    def _():
        o_ref[0] = acc.astype(o_ref.dtype)


def _in_call(x, nw, mod3, mod_row, w, col_blocks, rope, cos_t=None, sin_t=None):
    bsz, n, d = x.shape
    tm = min(IN_PROJ_ROWS, n)
    nj = len(col_blocks)
    cb = tuple(col_blocks)
    if cb == tuple(range(nj)):
        wmap = lambda b, i, j: (0, j)
    else:
        assert cb == CTX_LAST_BLOCKS
        wmap = lambda b, i, j: (0, j + j // 2)
    in_specs = [
        pl.BlockSpec((1, tm, d), lambda b, i, j: (b, i, 0)),
        pl.BlockSpec((1, d), lambda b, i, j: (0, 0)),
        pl.BlockSpec((1, 1, d), lambda b, i, j: (mod_row(b), 0, 0)),
        pl.BlockSpec((1, 1, d), lambda b, i, j: (mod_row(b), 0, 1)),
    ]
    args = [x, nw, mod3, mod3]
    if rope:
        in_specs += [pl.BlockSpec((tm, LANES), lambda b, i, j: (i, 0)),
                     pl.BlockSpec((tm, LANES), lambda b, i, j: (i, 0))]
        args += [cos_t, sin_t]
    in_specs.append(pl.BlockSpec((d, COL), wmap))
    args.append(w)
    return pl.pallas_call(
        functools.partial(_in_kernel, rope=rope),
        grid=(bsz, n // tm, nj),
        in_specs=in_specs,
        out_specs=pl.BlockSpec((1, tm, COL), lambda b, i, j: (b, i, j)),
        out_shape=jax.ShapeDtypeStruct((bsz, n, nj * COL), BF16),
        scratch_shapes=[pltpu.VMEM((tm, d), BF16)],
        compiler_params=_cparams(("parallel", "parallel", "arbitrary")),
        name="in_proj",
    )(*args)


SEQ_TILE = 64
PAD = 8
SCAN_ROWS = 64


def _seq_kernel(*refs, n, nc, need_ctx_out):
    it = iter(refs)
    ab_ref, ac_ref, ax_ref, rg_ref, rx_ref = (next(it) for _ in range(5))
    if need_ctx_out:
        cab_ref, cac_ref, cax_ref, crg_ref = (next(it) for _ in range(4))
    crx_ref = next(it)
    wa_ref, cw_ref, cbias_ref, wg_ref, ba_ref, bx_ref, lam_ref = (next(it) for _ in range(7))
    ya_ref, yr_ref = next(it), next(it)
    if need_ctx_out:
        yca_ref, ycr_ref = next(it), next(it)
    xp, a_scr, b_scr = next(it), next(it), next(it)
    T = SEQ_TILE
    W = HALF

    def stage(val, ns):
        xp[0:PAD, :] = jnp.zeros((PAD, W), F32)
        xp[PAD:PAD + ns, :] = val
        xp[PAD + ns:2 * PAD + ns, :] = jnp.zeros((PAD, W), F32)

    def conv_mixer(b_ref, c_ref, x_ref, out_ref, ns):
        stage(c_ref[0].astype(F32) * x_ref[0].astype(F32), ns)
        w0, w1, w2 = wa_ref[0:1, :], wa_ref[1:2, :], wa_ref[2:3, :]

        def body(ti, carry):
            t0 = pl.multiple_of(ti * T, T)
            ps = xp[pl.ds(t0, T + 2 * PAD), :]
            conv = w0 * ps[PAD - 1:PAD - 1 + T] + w1 * ps[PAD:PAD + T] + w2 * ps[PAD + 1:PAD + 1 + T]
            gate = b_ref[0, pl.ds(t0, T), :].astype(F32)
            out_ref[0, pl.ds(t0, T), :] = (gate * conv).astype(out_ref.dtype)
            return carry

        lax.fori_loop(0, ns // T, body, 0)

    nl = -lam_ref[...]
    sp = jnp.maximum(nl, 0.0) + jnp.log1p(jnp.exp(-jnp.abs(nl)))
    decay = sp * (-RG_C * LOG2E)
    NG = W // LANES

    def rglru(x_ref, ns, h0):
        stage(x_ref[0].astype(F32), ns)

        def gates_body(ti, carry):
            t0 = pl.multiple_of(ti * T, T)
            xs = xp[pl.ds(t0, T + 2 * PAD), :]
            for d in range(2):
                off = PAD - 3 if d == 0 else PAD
                xc = cbias_ref[d]
                for k in range(4):
                    xc = xc + cw_ref[d, k:k + 1, :] * xs[off + k:off + k + T]
                g = jnp.dot(xc.astype(BF16), wg_ref[d, 0], preferred_element_type=F32)
                r = _sigmoid(g[:, :W] + ba_ref[d])
                i = _sigmoid(g[:, W:] + bx_ref[d])
                a = jnp.exp2(r * decay[d])
                y = 1.0 - a * a
                bb = jnp.where(y > 0.0, y * lax.rsqrt(y), 0.0) * (i * xc)
                for g in range(NG):
                    a_scr[d, g, pl.ds(t0, T), :] = a[:, g * LANES:(g + 1) * LANES]
                    b_scr[d, g, pl.ds(t0, T), :] = bb[:, g * LANES:(g + 1) * LANES]
            return carry

        lax.fori_loop(0, ns // T, gates_body, 0, unroll=2)
        R = SCAN_ROWS
        nch = R // 8
        steps = ns // R
        sub = lax.broadcasted_iota(jnp.int32, (nch, 8, LANES), 1)

        def scan_body(ci, carry):
            out = []
            for d in range(2):
                r0 = pl.multiple_of((ci if d == 0 else steps - 1 - ci) * R, R)
                edge = 7 if d == 0 else 0
                for g in range(NG):
                    a = a_scr[d, g, pl.ds(r0, R), :].reshape(nch, 8, LANES)
                    b = b_scr[d, g, pl.ds(r0, R), :].reshape(nch, 8, LANES)
                    for s in (1, 2, 4):
                        sh = s if d == 0 else 8 - s
                        m = (sub >= s) if d == 0 else (sub < 8 - s)
                        b = b + a * jnp.where(m, pltpu.roll(b, sh, 1), 0.0)
                        a = a * jnp.where(m, pltpu.roll(a, sh, 1), 1.0)
                    c = carry[d * NG + g]
                    hs = [None] * nch
                    for j in (range(nch) if d == 0 else range(nch - 1, -1, -1)):
                        hs[j] = b[j] + a[j] * c
                        c = (jnp.broadcast_to(a[j, edge:edge + 1, :], (8, LANES)) * c
                             + jnp.broadcast_to(b[j, edge:edge + 1, :], (8, LANES)))
                    b_scr[d, g, pl.ds(r0, R), :] = jnp.concatenate(hs, axis=0)
                    out.append(c)
            return tuple(out)

        init = tuple(jnp.broadcast_to(h0[d][g], (8, LANES)) for d in range(2) for g in range(NG))
        ends = lax.fori_loop(0, steps, scan_body, init)
        return [[ends[d * NG + g][0:1, :] for g in range(NG)] for d in range(2)]

    def rg_out(g_ref, out_ref, ns):
        def body(ti, carry):
            t0 = pl.multiple_of(ti * T, T)
            for g in range(NG):
                hs = b_scr[0, g, pl.ds(t0, T), :] + b_scr[1, g, pl.ds(t0, T), :]
                gate = g_ref[0, pl.ds(t0, T), g * LANES:(g + 1) * LANES].astype(F32)
                out_ref[0, pl.ds(t0, T), g * LANES:(g + 1) * LANES] = (_gelu_tanh(gate) * hs).astype(out_ref.dtype)
            return carry

        lax.fori_loop(0, ns // T, body, 0)

    zero = [[jnp.zeros((1, LANES), F32)] * NG] * 2
    h_ctx = rglru(crx_ref, nc, zero)
    if need_ctx_out:
        rg_out(crg_ref, ycr_ref, nc)
        conv_mixer(cab_ref, cac_ref, cax_ref, yca_ref, nc)
    rglru(rx_ref, n, h_ctx)
    rg_out(rg_ref, yr_ref, n)
    conv_mixer(ab_ref, ac_ref, ax_ref, ya_ref, n)


def _seq_call(z, zc, zc_off, p, need_ctx_out):
    bsz, n, _ = z.shape
    nc = zc.shape[1]
    u = HALF // LANES

    def zspec(rows, off):
        return pl.BlockSpec((1, rows, HALF), lambda b, c: (b, 0, off // u + c))

    names = ("a_b", "a_c", "a_x", "r_g", "r_x")
    in_specs = [zspec(n, Z_FULL[k]) for k in names]
    args = [z] * 5
    if need_ctx_out:
        in_specs += [zspec(nc, zc_off[k]) for k in names[:4]]
        args += [zc] * 4
    in_specs.append(zspec(nc, zc_off["r_x"]))
    args.append(zc)
    in_specs += [
        pl.BlockSpec((3, HALF), lambda b, c: (0, c)),
        pl.BlockSpec((2, 4, HALF), lambda b, c: (0, 0, c)),
        pl.BlockSpec((2, 1, HALF), lambda b, c: (0, 0, c)),
        pl.BlockSpec((2, 1, HALF, 2 * HALF), lambda b, c: (0, c, 0, 0)),
        pl.BlockSpec((2, 1, HALF), lambda b, c: (0, 0, c)),
        pl.BlockSpec((2, 1, HALF), lambda b, c: (0, 0, c)),
        pl.BlockSpec((2, 1, HALF), lambda b, c: (0, 0, c)),
    ]
    args += [p["conv_a_w"], p["rg_conv_w"], p["rg_conv_b"], p["rg_wg"], p["rg_ba"], p["rg_bx"], p["rg_lambda"]]
    out_specs = [pl.BlockSpec((1, n, HALF), lambda b, c: (b, 0, c))] * 2
    out_shape = [jax.ShapeDtypeStruct((bsz, n, CONV_W), BF16), jax.ShapeDtypeStruct((bsz, n, RG_W), BF16)]
    if need_ctx_out:
        out_specs += [pl.BlockSpec((1, nc, HALF), lambda b, c: (b, 0, c))] * 2
        out_shape += [jax.ShapeDtypeStruct((bsz, nc, CONV_W), BF16), jax.ShapeDtypeStruct((bsz, nc, RG_W), BF16)]
    npad = max(n, nc) + 2 * PAD
    return pl.pallas_call(
        functools.partial(_seq_kernel, n=n, nc=nc, need_ctx_out=need_ctx_out),
        grid=(bsz, CONV_W // HALF),
        in_specs=in_specs,
        out_specs=out_specs,
        out_shape=out_shape,
        scratch_shapes=[pltpu.VMEM((npad, HALF), F32), pltpu.VMEM((2, u, max(n, nc), LANES), F32),
                        pltpu.VMEM((2, u, max(n, nc), LANES), F32)],
        compiler_params=_cparams(("parallel", "parallel")),
        name="seq_mixers",
    )(*args)


def _attn_kernel(*refs, has_lat, lam_init):
    if has_lat:
        q_ref, k_ref, v_ref, ck_ref, cv_ref, lam_ref, sw_ref, o_ref, k_all, v_ext = refs
    else:
        q_ref, ck_ref, cv_ref, lam_ref, sw_ref, o_ref, k_all, v_ext = refs
    nc = ck_ref.shape[1]
    nk = k_all.shape[0]

    @pl.when(pl.program_id(1) == 0)
    def _():
        k_all[0:nc, :] = ck_ref[0]
        if has_lat:
            k_all[nc:nk, :] = k_ref[0]
        for h in range(DA_HEADS):
            sl = slice(h * LANES, (h + 1) * LANES)
            v_ext[h, 0:nc, 0:LANES] = cv_ref[0, :, sl]
            if has_lat:
                v_ext[h, nc:nk, 0:LANES] = v_ref[0, :, sl]
            v_ext[h, :, LANES:2 * LANES] = jnp.ones((nk, LANES), v_ext.dtype)

    lane = lax.broadcasted_iota(jnp.int32, (1, LANES), 1)
    map1 = (lane % DA_HEAD_DIM) < (DA_HEAD_DIM // 2)
    dn = (((1,), (1,)), ((), ()))
    lam = lam_ref[0:1, 0:1]

    for h in range(DA_HEADS):
        sl = slice(h * LANES, (h + 1) * LANES)
        q = q_ref[0, :, sl]
        zero = jnp.zeros_like(q)

        def soft_v(qm):
            s = lax.dot_general(qm, k_all[:, sl], dn, preferred_element_type=F32)
            e = jnp.exp2(s - jnp.max(s, axis=-1, keepdims=True)).astype(BF16)
            r = jnp.dot(e, v_ext[h], preferred_element_type=F32)
            return r[:, :LANES] / r[:, LANES:LANES + 1]

        o = soft_v(jnp.where(map1, q, zero)) - lam * soft_v(jnp.where(map1, zero, q))
        y = _rms(o) * sw_ref[...] * (1.0 - lam_init)
        o_ref[0, :, sl] = y.astype(o_ref.dtype)


def _attn_call(zq, z, zc, zc_off, lam, subw, lam_init, has_lat):
    bsz, nq, _ = zq.shape
    nc = zc.shape[1]
    tq = min(ATTN_Q_ROWS, nq)
    w = DA_HEADS * LANES
    u = DA_HEADS
    in_specs = [pl.BlockSpec((1, tq, w), lambda b, i: (b, i, Z_FULL["q"] // u))]
    args = [zq]
    nk = nc
    if has_lat:
        n = z.shape[1]
        nk = nc + n
        in_specs += [pl.BlockSpec((1, n, w), lambda b, i: (b, 0, Z_FULL["k"] // u)),
                     pl.BlockSpec((1, n, w), lambda b, i: (b, 0, Z_FULL["v"] // u))]
        args += [z, z]
    in_specs += [pl.BlockSpec((1, nc, w), lambda b, i: (b, 0, zc_off["k"] // u)),
                 pl.BlockSpec((1, nc, w), lambda b, i: (b, 0, zc_off["v"] // u)),
                 pl.BlockSpec((8, LANES), lambda b, i: (0, 0)),
                 pl.BlockSpec((1, LANES), lambda b, i: (0, 0))]
    args += [zc, zc, lam, subw]
    return pl.pallas_call(
        functools.partial(_attn_kernel, has_lat=has_lat, lam_init=lam_init),
        grid=(bsz, nq // tq),
        in_specs=in_specs,
        out_specs=pl.BlockSpec((1, tq, w), lambda b, i: (b, i, 0)),
        out_shape=jax.ShapeDtypeStruct((bsz, nq, DA_HEADS * DA_V_DIM), BF16),
        scratch_shapes=[pltpu.VMEM((nk, w), BF16), pltpu.VMEM((DA_HEADS, nk, 2 * LANES), BF16)],
        compiler_params=_cparams(("parallel", "arbitrary")),
        name="diff_attn",
    )(*args)


def _merge_kernel(*refs, route):
    if route:
        (ya_ref, yr_ref, yd_ref, ga_ref, gr_ref, gd_ref, x_ref, g1_ref, nw_ref, sh_ref, sc_ref,
         wb_ref, wo_ref, rw_ref, xo_ref, h_ref, rt_ref) = refs
    else:
        (ya_ref, yr_ref, yd_ref, ga_ref, gr_ref, gd_ref, x_ref, g1_ref, nw_ref, sh_ref, sc_ref,
         wb_ref, wo_ref, xo_ref, h_ref) = refs
    m = None
    for i, (y_ref, g_ref) in enumerate(((ya_ref, ga_ref), (yr_ref, gr_ref), (yd_ref, gd_ref))):
        t = _sigmoid(g_ref[0].astype(F32)) * jnp.dot(y_ref[0], wb_ref[i], preferred_element_type=F32)
        m = t if m is None else m + t
    out = jnp.dot(m.astype(BF16), wo_ref[...], preferred_element_type=F32)
    xn = x_ref[0] + g1_ref[0] * out
    xo_ref[0] = xn
    h = (_rms(xn) * nw_ref[...]) * (1.0 + sc_ref[0]) + sh_ref[0]
    hb = h.astype(BF16)
    h_ref[0] = hb
    if route:
        logits = jnp.dot(hb, rw_ref[...], preferred_element_type=F32)
        lane = lax.broadcasted_iota(jnp.int32, logits.shape, 1)
        neg = jnp.float32(-jnp.inf)
        l1 = jnp.where(lane < N_EXPERTS, logits, neg)
        m1 = jnp.max(l1, axis=-1, keepdims=True)
        i1 = jnp.min(jnp.where(l1 == m1, lane, LANES), axis=-1, keepdims=True)
        l2 = jnp.where(lane == i1, neg, l1)
        m2 = jnp.max(l2, axis=-1, keepdims=True)
        i2 = jnp.min(jnp.where(l2 == m2, lane, LANES), axis=-1, keepdims=True)
        ex = jnp.exp(m2 - m1)
        gt1 = 1.0 / (1.0 + ex)
        gt2 = ex * gt1
        rt = jnp.where(lane == 0, i1.astype(F32),
                       jnp.where(lane == 1, i2.astype(F32),
                                 jnp.where(lane == 2, gt1, jnp.where(lane == 3, gt2, 0.0))))
        rt_ref[0] = rt


def _merge_call(ya, yr, yd, z, x, mod3, mod_row, nw2, wb, wo, router_w=None):
    bsz, n, d = x.shape
    tm = min(512, n)
    route = router_w is not None
    tok = lambda b, i: (b, i, 0)
    gu = d // LANES
    in_specs = [
        pl.BlockSpec((1, tm, CONV_W), tok), pl.BlockSpec((1, tm, RG_W), tok),
        pl.BlockSpec((1, tm, DA_HEADS * DA_V_DIM), tok),
        pl.BlockSpec((1, tm, d), lambda b, i: (b, i, Z_FULL["g_a"] // gu)),
        pl.BlockSpec((1, tm, d), lambda b, i: (b, i, Z_FULL["g_r"] // gu)),
        pl.BlockSpec((1, tm, d), lambda b, i: (b, i, Z_FULL["g_d"] // gu)),
        pl.BlockSpec((1, tm, d), tok),
        pl.BlockSpec((1, 1, d), lambda b, i: (mod_row(b), 0, 2)),
        pl.BlockSpec((1, d), lambda b, i: (0, 0)),
        pl.BlockSpec((1, 1, d), lambda b, i: (mod_row(b), 0, 3)),
        pl.BlockSpec((1, 1, d), lambda b, i: (mod_row(b), 0, 4)),
        pl.BlockSpec(wb.shape, lambda b, i: (0, 0, 0)),
        pl.BlockSpec(wo.shape, lambda b, i: (0, 0)),
    ]
    args = [ya, yr, yd, z, z, z, x, mod3, nw2, mod3, mod3, wb, wo]
    out_specs = [pl.BlockSpec((1, tm, d), tok), pl.BlockSpec((1, tm, d), tok)]
    out_shape = [jax.ShapeDtypeStruct((bsz, n, d), F32),
                 jax.ShapeDtypeStruct((bsz, n, d), BF16)]
    if route:
        in_specs.append(pl.BlockSpec(router_w.shape, lambda b, i: (0, 0)))
        args.append(router_w)
        out_specs.append(pl.BlockSpec((1, tm, LANES), tok))
        out_shape.append(jax.ShapeDtypeStruct((bsz, n, LANES), F32))
    return pl.pallas_call(
        functools.partial(_merge_kernel, route=route),
        grid=(bsz, n // tm),
        in_specs=in_specs,
        out_specs=out_specs,
        out_shape=out_shape,
        compiler_params=_cparams(("parallel", "parallel")),
        name="merge",
    )(*args)


def _ffn_kernel(h_ref, x_ref, g2_ref, wg_ref, wu_ref, w2_ref, o_ref, acc_ref):
    f = pl.program_id(2)
    h = h_ref[0]
    g = jnp.dot(h, wg_ref[...], preferred_element_type=F32)
    u = jnp.dot(h, wu_ref[...], preferred_element_type=F32)
    act = (_silu(g) * u).astype(BF16)
    part = jnp.dot(act, w2_ref[...], preferred_element_type=F32)

    @pl.when(f == 0)
    def _():
        acc_ref[...] = part

    @pl.when(f > 0)
    def _():
        acc_ref[...] += part

    @pl.when(f == pl.num_programs(2) - 1)
    def _():
        o_ref[0] = x_ref[0] + g2_ref[0] * acc_ref[...]


def _ffn_call(h, x, mod3, mod_row, w13, w2):
    bsz, n, d = x.shape
    ff = w2.shape[0]
    tm = min(SWIGLU_ROWS, n)
    tf = SWIGLU_FF_TILE
    nf = ff // tf
    tok = lambda b, i, f: (b, i, 0)
    return pl.pallas_call(
        _ffn_kernel,
        grid=(bsz, n // tm, nf),
        in_specs=[
            pl.BlockSpec((1, tm, d), tok),
            pl.BlockSpec((1, tm, d), tok),
            pl.BlockSpec((1, 1, d), lambda b, i, f: (mod_row(b), 0, 5)),
            pl.BlockSpec((d, tf), lambda b, i, f: (0, f)),
            pl.BlockSpec((d, tf), lambda b, i, f: (0, nf + f)),
            pl.BlockSpec((tf, d), lambda b, i, f: (f, 0)),
        ],
        out_specs=pl.BlockSpec((1, tm, d), tok),
        out_shape=jax.ShapeDtypeStruct((bsz, n, d), F32),
        scratch_shapes=[pltpu.VMEM((tm, d), F32)],
        compiler_params=_cparams(("parallel", "parallel", "arbitrary")),
        name="ffn",
    )(h, x, mod3, w13, w13, w2)


def _moe_kernel(be_ref, nu_ref, x_ref, wg_ref, wu_ref, w2_ref, o_ref, xb):
    i = pl.program_id(0)
    f = pl.program_id(1)
    used = i < nu_ref[0]

    @pl.when(used & (f == 0))
    def _():
        xb[...] = x_ref[...].astype(BF16)

    @pl.when(jnp.logical_not(used) & (f == 0))
    def _():
        o_ref[...] = jnp.zeros(o_ref.shape, o_ref.dtype)

    @pl.when(used)
    def _():
        x = xb[...]
        g = jnp.dot(x, wg_ref[0], preferred_element_type=F32)
        u = jnp.dot(x, wu_ref[0], preferred_element_type=F32)
        act = (_silu(g) * u).astype(BF16)
        part = jnp.dot(act, w2_ref[0], preferred_element_type=F32)

        @pl.when(f == 0)
        def _():
            o_ref[...] = part

        @pl.when(f > 0)
        def _():
            o_ref[...] += part


def _moe_call(xs, block_e, n_used, w13, w2, rows):
    n_rows, d = xs.shape
    n_blocks = n_rows // rows
    ff = w2.shape[1]
    tf = SWIGLU_FF_TILE
    nf = ff // tf

    def bi(i, nu):
        return jnp.minimum(i, nu[0] - 1)

    def fi(i, f, nu):
        return jnp.where(i < nu[0], f, nf - 1)

    grid_spec = pltpu.PrefetchScalarGridSpec(
        num_scalar_prefetch=2,
        grid=(n_blocks, nf),
        in_specs=[
            pl.BlockSpec((rows, d), lambda i, f, be, nu: (bi(i, nu), 0)),
            pl.BlockSpec((1, d, tf), lambda i, f, be, nu: (be[i], 0, fi(i, f, nu))),
            pl.BlockSpec((1, d, tf), lambda i, f, be, nu: (be[i], 0, nf + fi(i, f, nu))),
            pl.BlockSpec((1, tf, d), lambda i, f, be, nu: (be[i], fi(i, f, nu), 0)),
        ],
        out_specs=pl.BlockSpec((rows, d), lambda i, f, be, nu: (i, 0)),
        scratch_shapes=[pltpu.VMEM((rows, d), BF16)],
    )
    return pl.pallas_call(
        _moe_kernel,
        grid_spec=grid_spec,
        out_shape=jax.ShapeDtypeStruct((n_rows, d), F32),
        compiler_params=_cparams(("arbitrary", "arbitrary")),
        name="moe",
    )(block_e, n_used, xs, w13, w13, w2)


GROUP_PAD = 8
GROUP_SIZES = tuple(SWIGLU_ROWS >> s for s in range(SWIGLU_ROWS.bit_length()) if SWIGLU_ROWS >> s >= GROUP_PAD)


def _sorted_rows(tm):
    return 2 * tm + N_EXPERTS * GROUP_PAD


def _group_copies(step, src_of, dst_of, loc_ref, base_ref, pc_ref, sem, sizes):
    for e in range(N_EXPERTS):
        loc = loc_ref[step * N_EXPERTS + e]
        base = base_ref[step * N_EXPERTS + e]
        pc = pc_ref[step * N_EXPERTS + e]
        for size in sizes:
            before = pc & (-2 * size)
            src = src_of(pl.multiple_of(loc + before, GROUP_PAD), pl.multiple_of(base + before, GROUP_PAD), size)
            dst = dst_of(pl.multiple_of(loc + before, GROUP_PAD), pl.multiple_of(base + before, GROUP_PAD), size)
            yield (pc & size) != 0, pltpu.make_async_copy(src, dst, sem)


def _dispatch2_kernel(loc_ref, base_ref, pc_ref, tail_ref, nu_ref, pos_ref, h_ref, xs_hbm, srt, zeros, sem, zsem,
                      *, tm, nt, rows, sizes):
    i = pl.program_id(0)
    n = pl.num_programs(0)
    slot = i % 2
    sr = srt.shape[1]
    n_blocks = xs_hbm.shape[0] // rows
    min_used = 2 * tm * nt // rows

    def tail_copies():
        for e in range(N_EXPERTS):
            start = tail_ref[e]
            length = tail_ref[N_EXPERTS + e]
            for size in sizes[1:]:
                before = length & (-2 * size)
                dst = xs_hbm.at[pl.ds(pl.multiple_of(start + before, GROUP_PAD), size)]
                yield (length & size) != 0, pltpu.make_async_copy(zeros.at[pl.ds(0, size)], dst, zsem)
        for b in range(min_used, n_blocks):
            dst = xs_hbm.at[pl.ds(b * rows, rows)]
            yield b >= nu_ref[0], pltpu.make_async_copy(zeros, dst, zsem)

    @pl.when(i == 0)
    def _():
        zeros[...] = jnp.zeros(zeros.shape, zeros.dtype)
        for pred, cp in tail_copies():
            pl.when(pred)(cp.start)

    r = lax.broadcasted_iota(jnp.int32, (sr, tm), 0)
    hit = (r == pos_ref[0, 0:1, :]) | (r == pos_ref[0, 1:2, :])
    sel = jnp.where(hit, 1.0, 0.0).astype(BF16)
    srt[slot] = jnp.dot(sel, h_ref[...], preferred_element_type=F32)

    def group_copies(step, s):
        return _group_copies(step, lambda lo, ba, sz: srt.at[s, pl.ds(lo, sz)],
                             lambda lo, ba, sz: xs_hbm.at[pl.ds(ba, sz)], loc_ref, base_ref, pc_ref, sem.at[s], sizes)

    for pred, cp in group_copies(i, slot):
        pl.when(pred)(cp.start)

    @pl.when(i > 0)
    def _():
        for pred, cp in group_copies(i - 1, 1 - slot):
            pl.when(pred)(cp.wait)

    @pl.when(i == n - 1)
    def _():
        for pred, cp in group_copies(i, slot):
            pl.when(pred)(cp.wait)
        for pred, cp in tail_copies():
            pl.when(pred)(cp.wait)


def _dispatch2_call(h2, plan, tm, rows):
    t, d = h2.shape
    nt = t // tm
    sr = _sorted_rows(tm)
    sizes = tuple(s for s in GROUP_SIZES if s <= tm)
    grid_spec = pltpu.PrefetchScalarGridSpec(
        num_scalar_prefetch=5,
        grid=(nt,),
        in_specs=[pl.BlockSpec((1, 2, tm), lambda i, *_: (i, 0, 0)),
                  pl.BlockSpec((tm, d), lambda i, *_: (i, 0))],
        out_specs=pl.BlockSpec(memory_space=pl.ANY),
        scratch_shapes=[pltpu.VMEM((2, sr, d), F32), pltpu.VMEM((rows, d), F32),
                        pltpu.SemaphoreType.DMA((2,)), pltpu.SemaphoreType.DMA(())],
    )
    return pl.pallas_call(
        functools.partial(_dispatch2_kernel, tm=tm, nt=nt, rows=rows, sizes=sizes),
        grid_spec=grid_spec,
        out_shape=jax.ShapeDtypeStruct((plan["n_blocks"] * rows, d), F32),
        compiler_params=pltpu.CompilerParams(dimension_semantics=("arbitrary",), vmem_limit_bytes=VMEM_LIMIT),
        name="moe_dispatch",
    )(plan["loc"], plan["base"], plan["pc"], plan["tail"], plan["n_used"], plan["pos_rows"], h2)


def _combine2_kernel(loc_ref, base_ref, pc_ref, x_ref, pos_ref, rt_ref, g2_ref, fw_ref, ys_hbm, o_ref, buf, sem,
                     *, tm, sizes):
    i = pl.program_id(0)
    n = pl.num_programs(0)
    slot = i % 2
    sr = buf.shape[1]

    def group_copies(step, s):
        return _group_copies(step, lambda lo, ba, sz: ys_hbm.at[pl.ds(ba, sz)],
                             lambda lo, ba, sz: buf.at[s, pl.ds(lo, sz)], loc_ref, base_ref, pc_ref, sem.at[s], sizes)

    @pl.when(i == 0)
    def _():
        buf[...] = jnp.zeros(buf.shape, buf.dtype)
        for pred, cp in group_copies(0, 0):
            pl.when(pred)(cp.start)

    @pl.when(i + 1 < n)
    def _():
        for pred, cp in group_copies(i + 1, 1 - slot):
            pl.when(pred)(cp.start)

    for pred, cp in group_copies(i, slot):
        pl.when(pred)(cp.wait)

    rt = rt_ref[...]
    col = lax.broadcasted_iota(jnp.int32, (tm, sr), 1)
    pos = pos_ref[...]
    w = jnp.where(col == pos[:, 0:1], rt[:, 2:3], jnp.where(col == pos[:, 1:2], rt[:, 3:4], 0.0))
    y = jnp.dot(w.astype(BF16), buf[slot].astype(BF16), preferred_element_type=F32)
    xn = x_ref[...] + g2_ref[0] * y
    o_ref[...] = _rms(xn) * fw_ref[...]


def _combine2_call(x2, ys, plan, route2, mod3, n_per_seq, fw, tm):
    t, d = x2.shape
    nt = t // tm
    per = n_per_seq // tm
    sr = _sorted_rows(tm)
    sizes = tuple(s for s in GROUP_SIZES if s <= tm)
    grid_spec = pltpu.PrefetchScalarGridSpec(
        num_scalar_prefetch=3,
        grid=(nt,),
        in_specs=[
            pl.BlockSpec((tm, d), lambda i, *_: (i, 0)),
            pl.BlockSpec((tm, 2), lambda i, *_: (i, 0)),
            pl.BlockSpec((tm, LANES), lambda i, *_: (i, 0)),
            pl.BlockSpec((1, 1, d), lambda i, *_: (i // per, 0, 5)),
            pl.BlockSpec((1, d), lambda i, *_: (0, 0)),
            pl.BlockSpec(memory_space=pl.ANY),
        ],
        out_specs=pl.BlockSpec((tm, d), lambda i, *_: (i, 0)),
        scratch_shapes=[pltpu.VMEM((2, sr, d), F32), pltpu.SemaphoreType.DMA((2,))],
    )
    return pl.pallas_call(
        functools.partial(_combine2_kernel, tm=tm, sizes=sizes),
        grid_spec=grid_spec,
        out_shape=jax.ShapeDtypeStruct((t, d), F32),
        compiler_params=pltpu.CompilerParams(dimension_semantics=("arbitrary",), vmem_limit_bytes=VMEM_LIMIT),
        name="combine_final",
    )(plan["loc"], plan["base"], plan["pc"], x2, plan["pos_cols"], route2, mod3, fw, ys)


def _sort_plan(route, tm, rows):
    t = route.shape[0]
    nt = t // tm
    experts = jnp.arange(N_EXPERTS, dtype=jnp.int32)
    e = route[:, :2].astype(jnp.int32).reshape(nt, 2 * tm)
    hot = (e[:, :, None] == experts[None, None, :]).astype(jnp.int32)
    cs = jnp.cumsum(hot, axis=1)
    rank = jnp.sum(cs * hot, axis=2) - 1
    cnt = cs[:, -1, :]
    pc = (cnt + GROUP_PAD - 1) // GROUP_PAD * GROUP_PAD
    loc = jnp.cumsum(pc, axis=1) - pc
    pos = jnp.sum(hot * loc[:, None, :], axis=2) + rank
    tot = jnp.sum(pc, axis=0)
    padded = (tot + rows - 1) // rows * rows
    e_end = jnp.cumsum(padded)
    e_start = e_end - padded
    base = e_start[None, :] + jnp.cumsum(pc, axis=0) - pc
    n_blocks = -(-(2 * t + nt * N_EXPERTS * (GROUP_PAD - 1)) // rows) + N_EXPERTS
    block_start = jnp.arange(n_blocks, dtype=jnp.int32) * rows
    block_e = jnp.minimum(jnp.sum((block_start[:, None] >= e_end[None, :]).astype(jnp.int32), axis=1),
                          N_EXPERTS - 1)
    pos3 = pos.reshape(nt, tm, 2)
    i32 = lambda a: a.astype(jnp.int32)
    return dict(
        loc=i32(loc.reshape(-1)), base=i32(base.reshape(-1)), pc=i32(pc.reshape(-1)),
        tail=i32(jnp.concatenate([e_start + tot, padded - tot])),
        n_used=i32(e_end[-1] // rows).reshape(1), block_e=i32(block_e), n_blocks=n_blocks,
        pos_rows=i32(jnp.swapaxes(pos3, 1, 2)), pos_cols=i32(pos3.reshape(t, 2)))


def _norm_kernel(x_ref, fw_ref, o_ref):
    o_ref[0] = _rms(x_ref[0]) * fw_ref[...]


def _norm_call(x, fw):
    bsz, n, d = x.shape
    tm = min(512, n)
    tok = lambda b, i: (b, i, 0)
    return pl.pallas_call(
        _norm_kernel,
        grid=(bsz, n // tm),
        in_specs=[pl.BlockSpec((1, tm, d), tok), pl.BlockSpec((1, d), lambda b, i: (0, 0))],
        out_specs=pl.BlockSpec((1, tm, d), tok),
        out_shape=jax.ShapeDtypeStruct((bsz, n, d), F32),
        compiler_params=_cparams(("parallel", "parallel")),
        name="final_norm",
    )(x, fw)


def _rope_perm():
    perm = np.zeros((LANES,), np.int32)
    half = ROPE_AXIS_DIM // 2
    for i in range(LANES // 2):
        comp, axis, j = i // 32, (i % 32) // half, i % half
        perm[i] = comp * DA_HEAD_DIM + axis * ROPE_AXIS_DIM + j
        perm[LANES // 2 + i] = perm[i] + half
    return perm


def _rope_tables(n):
    rows = n // GRID_W
    pos_r = jnp.repeat(jnp.arange(rows, dtype=F32), GRID_W)
    pos_c = jnp.broadcast_to(jnp.arange(GRID_W, dtype=F32), (rows, GRID_W)).reshape(-1)
    inv_freq = ROPE_BASE ** (-jnp.arange(0, ROPE_AXIS_DIM, 2, dtype=F32) / ROPE_AXIS_DIM)
    ang = jnp.stack([pos_r[:, None] * inv_freq, pos_c[:, None] * inv_freq], axis=1)
    cos, sin = jnp.cos(ang), jnp.sin(ang)
    half = jnp.concatenate([cos.reshape(n, 32), cos.reshape(n, 32)], axis=1)
    cos_t = jnp.concatenate([half, half], axis=1)
    sh = jnp.concatenate([sin.reshape(n, 32), sin.reshape(n, 32)], axis=1)
    sin_t = jnp.concatenate([-sh, sh], axis=1)
    return cos_t, sin_t


def _blockdiag_gate(wa, wx):
    per = HALF // (RG_W // RG_BLOCKS)
    eye = jnp.eye(per, dtype=wa.dtype)

    def bd(w):
        w4 = w.reshape(2, RG_BLOCKS // per, per, w.shape[-2], w.shape[-1])
        return jnp.einsum("dhbij,bc->dhbicj", w4, eye).reshape(2, RG_BLOCKS // per, HALF, HALF)

    return jnp.concatenate([bd(wa), bd(wx)], axis=-1).astype(BF16)


def kernel(x, c, ctx, c_ctx, mod_w, mod_b, norm1_w, norm2_w, w_in, conv_a_w, rg_conv_w, rg_conv_b, rg_wa, rg_ba, rg_wx, rg_bx, rg_lambda, da_lambda, da_subln_w, w_branch, w_out, ffn_w13, ffn_w2, router_w, moe_w13, moe_w2, final_norm_w):
    bsz, n, d = x.shape
    nc = ctx.shape[1]
    depth = mod_w.shape[0]
    cos_t, sin_t = _rope_tables(n)
    perm = _rope_perm()
    ref_cols = {name: np.arange(start, start + width, dtype=np.int32) for name, start, width in REF_COLS}
    for name in ("q", "k"):
        for h in range(DA_HEADS):
            base = ref_cols[name][0] + h * LANES
            ref_cols[name][h * LANES:(h + 1) * LANES] = base + perm
    z_cols = np.concatenate([ref_cols[name] for name in Z_ORDER])

    bp = -(-(bsz + 1) // 8) * 8
    cc = jnp.zeros((bp, d), F32).at[:bsz].set(c).at[bsz].set(c_ctx)
    lat_row = lambda b: b
    ctx_row = lambda b: bsz
    cx = ctx
    moe_rows = SWIGLU_ROWS if bsz * n >= 8192 else 128

    for l in range(depth):
        last = l == depth - 1
        lam_init = 0.8 - 0.6 * math.exp(-0.3 * l)
        j = l // 2
        mod, lam = _mod_call(cc, mod_w[l], mod_b[l][None, :], da_lambda[l], lam_init)
        mod3 = mod.reshape(bp, 1, N_MOD * d)
        w_in_l = w_in[l][:, z_cols].astype(BF16)
        nw1 = norm1_w[l][None, :]
        nw2 = norm2_w[l][None, :]
        p = dict(
            conv_a_w=conv_a_w[l], rg_conv_w=rg_conv_w[l], rg_conv_b=rg_conv_b[l][:, None, :],
            rg_wg=_blockdiag_gate(rg_wa[l], rg_wx[l]),
            rg_ba=rg_ba[l].reshape(2, 1, RG_W), rg_bx=rg_bx[l].reshape(2, 1, RG_W),
            rg_lambda=rg_lambda[l][:, None, :])
        subw = da_subln_w[l][None, :]
        wb = w_branch[l].astype(BF16)
        wo = w_out[l].astype(BF16)

        ncol = w_in.shape[2] // COL
        z = _in_call(x, nw1, mod3, lat_row, w_in_l, range(ncol), True, cos_t, sin_t)
        flat = lambda a: a.reshape((1, bsz * nc) + a.shape[2:])
        unflat = lambda a: a.reshape((bsz, nc) + a.shape[2:])
        if last:
            zc = unflat(_in_call(flat(cx), nw1, mod3, ctx_row, w_in_l, CTX_LAST_BLOCKS, False))
            zc_off = Z_CTX_LAST
        else:
            zc = unflat(_in_call(flat(cx), nw1, mod3, ctx_row, w_in_l, range(ncol), False))
            zc_off = Z_FULL

        seq_out = _seq_call(z, zc, zc_off, p, not last)
        yd = _attn_call(z, z, zc, zc_off, lam, subw, lam_init, True)
        moe_layer = l % 2 == 1
        if moe_layer:
            rw = jnp.zeros((d, LANES), BF16).at[:, :N_EXPERTS].set(router_w[j].astype(BF16))
            xn, h, route = _merge_call(seq_out[0], seq_out[1], yd, z, x, mod3, lat_row, nw2, wb, wo, rw)
            t = bsz * n
            tm = min(512, n)
            route2 = route.reshape(t, LANES)
            plan = _sort_plan(route2, tm, moe_rows)
            xs = _dispatch2_call(h.reshape(t, d), plan, tm, moe_rows)
            ys = _moe_call(xs, plan["block_e"], plan["n_used"], moe_w13[j].astype(BF16),
                           moe_w2[j].astype(BF16), moe_rows)
            if last:
                out = _combine2_call(xn.reshape(t, d), ys, plan, route2, mod3, n, final_norm_w[None, :], tm)
                return out.reshape(bsz, n, d)
            raise NotImplementedError("expert layer followed by another layer")
        xn, h = _merge_call(seq_out[0], seq_out[1], yd, z, x, mod3, lat_row, nw2, wb, wo)
        w13 = ffn_w13[j].astype(BF16)
        w2 = ffn_w2[j].astype(BF16)
        x = _ffn_call(h, xn, mod3, lat_row, w13, w2)
        if not last:
            ycd = _attn_call(zc, None, zc, zc_off, lam, subw, lam_init, False)
            cxn, hc = _merge_call(flat(seq_out[2]), flat(seq_out[3]), flat(ycd), flat(zc), flat(cx), mod3,
                                  ctx_row, nw2, wb, wo)
            cx = unflat(_ffn_call(hc, cxn, mod3, ctx_row, w13, w2))
    return _norm_call(x, final_norm_w[None, :])
```

```python
import functools
import math

import jax
import jax.numpy as jnp
import numpy as np
from jax import lax
from jax.experimental import pallas as pl
from jax.experimental.pallas import tpu as pltpu

F32 = jnp.float32
BF16 = jnp.bfloat16

EPS = 1e-6
N_MOD = 6
GRID_W = 64
CONV_W = 512
RG_W = 512
RG_BLOCKS = 8
RG_C = 8.0
DA_HEADS = 4
DA_HEAD_DIM = 64
DA_V_DIM = 128
ROPE_AXIS_DIM = 32
ROPE_BASE = 10000.0
N_EXPERTS = 8
LANES = 128
HALF = 256
VMEM_LIMIT = 56 * 1024 * 1024
IN_PROJ_ROWS = 2048
IN_PROJ_CHUNK = 256
ATTN_Q_ROWS = 512
SWIGLU_ROWS = 512
SWIGLU_FF_TILE = 1792

COL = 1024
REF_COLS = (("a_b", 0, 512), ("a_c", 512, 512), ("a_x", 1024, 512), ("r_g", 1536, 512), ("r_x", 2048, 512),
            ("q", 2560, 512), ("k", 3072, 512), ("v", 3584, 512), ("g_a", 4096, 1024), ("g_r", 5120, 1024),
            ("g_d", 6144, 1024))
Z_ORDER = ("q", "k", "v", "a_b", "a_c", "a_x", "r_g", "r_x", "g_a", "g_r", "g_d")
Z_FULL = dict(q=0, k=4, v=8, a_b=12, a_c=16, a_x=20, r_g=24, r_x=28, g_a=32, g_r=40, g_d=48)
CTX_LAST_BLOCKS = (0, 1, 3)
Z_CTX_LAST = dict(k=4, v=8, r_x=20)
LOG2E = 1.4426950408889634


def _cparams(sem, vmem=VMEM_LIMIT):
    return pltpu.CompilerParams(dimension_semantics=sem, vmem_limit_bytes=vmem)


def _sigmoid(x):
    return 1.0 / (1.0 + jnp.exp2(x * (-LOG2E)))


def _silu(x):
    return x * _sigmoid(x)


def _gelu_tanh(x):
    return 0.5 * x * (1.0 + jnp.tanh(math.sqrt(2.0 / math.pi) * (x + 0.044715 * (x * x * x))))


def _rms(x):
    return x * lax.rsqrt(jnp.mean(x * x, axis=-1, keepdims=True) + EPS)


def _mod_kernel(cc_ref, w_ref, b_ref, dl_ref, mod_ref, lam_ref, *, lam_init):
    cc = cc_ref[...]
    s = _silu(cc)
    mod_ref[...] = jnp.dot(s, w_ref[...], preferred_element_type=F32,
                           precision=lax.Precision.HIGHEST) + b_ref[...]
    dl = dl_ref[...]
    s1 = jnp.sum(dl[0:1] * dl[1:2], axis=-1, keepdims=True)
    s2 = jnp.sum(dl[2:3] * dl[3:4], axis=-1, keepdims=True)
    lam = jnp.exp(s1) - jnp.exp(s2) + lam_init
    lam_ref[...] = jnp.broadcast_to(lam, lam_ref.shape)


def _mod_call(cc, w, b, dl, lam_init):
    bp, d = cc.shape
    nd = w.shape[1]
    tn = 1024
    return pl.pallas_call(
        functools.partial(_mod_kernel, lam_init=lam_init),
        grid=(nd // tn,),
        in_specs=[
            pl.BlockSpec((bp, d), lambda j: (0, 0)),
            pl.BlockSpec((d, tn), lambda j: (0, j)),
            pl.BlockSpec((1, tn), lambda j: (0, j)),
            pl.BlockSpec(dl.shape, lambda j: (0, 0)),
        ],
        out_specs=[
            pl.BlockSpec((bp, tn), lambda j: (0, j)),
            pl.BlockSpec((8, LANES), lambda j: (0, 0)),
        ],
        out_shape=[jax.ShapeDtypeStruct((bp, nd), F32), jax.ShapeDtypeStruct((8, LANES), F32)],
        compiler_params=_cparams(("arbitrary",)),
        name="mod",
    )(cc, w, b, dl)


def _in_kernel(*refs, rope):
    if rope:
        x_ref, nw_ref, sh_ref, sc_ref, cos_ref, sin_ref, w_ref, o_ref, u_scr = refs
    else:
        x_ref, nw_ref, sh_ref, sc_ref, w_ref, o_ref, u_scr = refs
    j = pl.program_id(2)
    q_scale = DA_HEAD_DIM ** -0.5 * LOG2E
    n_q = DA_HEADS * 2 * DA_HEAD_DIM // LANES
    tm = x_ref.shape[1]
    chunk = min(IN_PROJ_CHUNK, tm)

    @pl.when(j == 0)
    def _():
        for c in range(tm // chunk):
            rows = slice(c * chunk, (c + 1) * chunk)
            y = _rms(x_ref[0, rows, :]) * nw_ref[...]
            u = (y * (1.0 + sc_ref[0]) + sh_ref[0]).astype(BF16)
            u_scr[rows, :] = u
            acc = jnp.dot(u, w_ref[...], preferred_element_type=F32)
            for g in range(acc.shape[1] // LANES):
                r = acc[:, g * LANES:(g + 1) * LANES]
                if rope:
                    r = r * cos_ref[rows, :] + pltpu.roll(r, LANES // 2, 1) * sin_ref[rows, :]
                if g < n_q:
                    r = r * q_scale
                o_ref[0, rows, g * LANES:(g + 1) * LANES] = r.astype(o_ref.dtype)

    @pl.when(j != 0)
    def _():
        o_ref[0] = jnp.dot(u_scr[...], w_ref[...], preferred_element_type=F32).astype(o_ref.dtype)


def _in_call(x, nw, mod3, mod_row, w, col_blocks, rope, cos_t=None, sin_t=None):
    bsz, n, d = x.shape
    tm = min(IN_PROJ_ROWS, n)
    nj = len(col_blocks)
    cb = tuple(col_blocks)
    if cb == tuple(range(nj)):
        wmap = lambda b, i, j: (0, j)
    else:
        assert cb == CTX_LAST_BLOCKS
        wmap = lambda b, i, j: (0, j + j // 2)
    in_specs = [
        pl.BlockSpec((1, tm, d), lambda b, i, j: (b, i, 0)),
        pl.BlockSpec((1, d), lambda b, i, j: (0, 0)),
        pl.BlockSpec((1, 1, d), lambda b, i, j: (mod_row(b), 0, 0)),
        pl.BlockSpec((1, 1, d), lambda b, i, j: (mod_row(b), 0, 1)),
    ]
    args = [x, nw, mod3, mod3]
    if rope:
        in_specs += [pl.BlockSpec((tm, LANES), lambda b, i, j: (i, 0)),
                     pl.BlockSpec((tm, LANES), lambda b, i, j: (i, 0))]
        args += [cos_t, sin_t]
    in_specs.append(pl.BlockSpec((d, COL), wmap))
    args.append(w)
    return pl.pallas_call(
        functools.partial(_in_kernel, rope=rope),
        grid=(bsz, n // tm, nj),
        in_specs=in_specs,
        out_specs=pl.BlockSpec((1, tm, COL), lambda b, i, j: (b, i, j)),
        out_shape=jax.ShapeDtypeStruct((bsz, n, nj * COL), BF16),
        scratch_shapes=[pltpu.VMEM((tm, d), BF16)],
        compiler_params=_cparams(("parallel", "parallel", "arbitrary")),
        name="in_proj",
    )(*args)


SEQ_TILE = 64
PAD = 8
SCAN_ROWS = 64


def _seq_kernel(*refs, n, nc, need_ctx_out):
    it = iter(refs)
    ab_ref, ac_ref, ax_ref, rg_ref, rx_ref = (next(it) for _ in range(5))
    if need_ctx_out:
        cab_ref, cac_ref, cax_ref, crg_ref = (next(it) for _ in range(4))
    crx_ref = next(it)
    wa_ref, cw_ref, cbias_ref, wg_ref, ba_ref, bx_ref, lam_ref = (next(it) for _ in range(7))
    ya_ref, yr_ref = next(it), next(it)
    if need_ctx_out:
        yca_ref, ycr_ref = next(it), next(it)
    xp, a_scr, b_scr = next(it), next(it), next(it)
    T = SEQ_TILE
    W = HALF

    def stage(val, ns):
        xp[0:PAD, :] = jnp.zeros((PAD, W), F32)
        xp[PAD:PAD + ns, :] = val
        xp[PAD + ns:2 * PAD + ns, :] = jnp.zeros((PAD, W), F32)

    def conv_mixer(b_ref, c_ref, x_ref, out_ref, ns):
        stage(c_ref[0].astype(F32) * x_ref[0].astype(F32), ns)
        w0, w1, w2 = wa_ref[0:1, :], wa_ref[1:2, :], wa_ref[2:3, :]

        def body(ti, carry):
            t0 = pl.multiple_of(ti * T, T)
            ps = xp[pl.ds(t0, T + 2 * PAD), :]
            conv = w0 * ps[PAD - 1:PAD - 1 + T] + w1 * ps[PAD:PAD + T] + w2 * ps[PAD + 1:PAD + 1 + T]
            gate = b_ref[0, pl.ds(t0, T), :].astype(F32)
            out_ref[0, pl.ds(t0, T), :] = (gate * conv).astype(out_ref.dtype)
            return carry

        lax.fori_loop(0, ns // T, body, 0)

    nl = -lam_ref[...]
    sp = jnp.maximum(nl, 0.0) + jnp.log1p(jnp.exp(-jnp.abs(nl)))
    decay = sp * (-RG_C * LOG2E)
    NG = W // LANES

    def rglru(x_ref, ns, h0):
        stage(x_ref[0].astype(F32), ns)

        def gates_body(ti, carry):
            t0 = pl.multiple_of(ti * T, T)
            xs = xp[pl.ds(t0, T + 2 * PAD), :]
            for d in range(2):
                off = PAD - 3 if d == 0 else PAD
                xc = cbias_ref[d]
                for k in range(4):
                    xc = xc + cw_ref[d, k:k + 1, :] * xs[off + k:off + k + T]
                g = jnp.dot(xc.astype(BF16), wg_ref[d, 0], preferred_element_type=F32)
                r = _sigmoid(g[:, :W] + ba_ref[d])
                i = _sigmoid(g[:, W:] + bx_ref[d])
                a = jnp.exp2(r * decay[d])
                y = 1.0 - a * a
                bb = jnp.where(y > 0.0, y * lax.rsqrt(y), 0.0) * (i * xc)
                for g in range(NG):
                    a_scr[d, g, pl.ds(t0, T), :] = a[:, g * LANES:(g + 1) * LANES]
                    b_scr[d, g, pl.ds(t0, T), :] = bb[:, g * LANES:(g + 1) * LANES]
            return carry

        lax.fori_loop(0, ns // T, gates_body, 0, unroll=2)
        R = SCAN_ROWS
        nch = R // 8
        steps = ns // R
        sub = lax.broadcasted_iota(jnp.int32, (nch, 8, LANES), 1)

        def scan_body(ci, carry):
            out = []
            for d in range(2):
                r0 = pl.multiple_of((ci if d == 0 else steps - 1 - ci) * R, R)
                edge = 7 if d == 0 else 0
                for g in range(NG):
                    a = a_scr[d, g, pl.ds(r0, R), :].reshape(nch, 8, LANES)
                    b = b_scr[d, g, pl.ds(r0, R), :].reshape(nch, 8, LANES)
                    for s in (1, 2, 4):
                        sh = s if d == 0 else 8 - s
                        m = (sub >= s) if d == 0 else (sub < 8 - s)
                        b = b + a * jnp.where(m, pltpu.roll(b, sh, 1), 0.0)
                        a = a * jnp.where(m, pltpu.roll(a, sh, 1), 1.0)
                    c = carry[d * NG + g]
                    hs = [None] * nch
                    for j in (range(nch) if d == 0 else range(nch - 1, -1, -1)):
                        hs[j] = b[j] + a[j] * c
                        c = (jnp.broadcast_to(a[j, edge:edge + 1, :], (8, LANES)) * c
                             + jnp.broadcast_to(b[j, edge:edge + 1, :], (8, LANES)))
                    b_scr[d, g, pl.ds(r0, R), :] = jnp.concatenate(hs, axis=0)
                    out.append(c)
            return tuple(out)

        init = tuple(jnp.broadcast_to(h0[d][g], (8, LANES)) for d in range(2) for g in range(NG))
        ends = lax.fori_loop(0, steps, scan_body, init)
        return [[ends[d * NG + g][0:1, :] for g in range(NG)] for d in range(2)]

    def rg_out(g_ref, out_ref, ns):
        def body(ti, carry):
            t0 = pl.multiple_of(ti * T, T)
            for g in range(NG):
                hs = b_scr[0, g, pl.ds(t0, T), :] + b_scr[1, g, pl.ds(t0, T), :]
                gate = g_ref[0, pl.ds(t0, T), g * LANES:(g + 1) * LANES].astype(F32)
                out_ref[0, pl.ds(t0, T), g * LANES:(g + 1) * LANES] = (_gelu_tanh(gate) * hs).astype(out_ref.dtype)
            return carry

        lax.fori_loop(0, ns // T, body, 0)

    zero = [[jnp.zeros((1, LANES), F32)] * NG] * 2
    h_ctx = rglru(crx_ref, nc, zero)
    if need_ctx_out:
        rg_out(crg_ref, ycr_ref, nc)
        conv_mixer(cab_ref, cac_ref, cax_ref, yca_ref, nc)
    rglru(rx_ref, n, h_ctx)
    rg_out(rg_ref, yr_ref, n)
    conv_mixer(ab_ref, ac_ref, ax_ref, ya_ref, n)


def _seq_call(z, zc, zc_off, p, need_ctx_out):
    bsz, n, _ = z.shape
    nc = zc.shape[1]
    u = HALF // LANES

    def zspec(rows, off):
        return pl.BlockSpec((1, rows, HALF), lambda b, c: (b, 0, off // u + c))

    names = ("a_b", "a_c", "a_x", "r_g", "r_x")
    in_specs = [zspec(n, Z_FULL[k]) for k in names]
    args = [z] * 5
    if need_ctx_out:
        in_specs += [zspec(nc, zc_off[k]) for k in names[:4]]
        args += [zc] * 4
    in_specs.append(zspec(nc, zc_off["r_x"]))
    args.append(zc)
    in_specs += [
        pl.BlockSpec((3, HALF), lambda b, c: (0, c)),
        pl.BlockSpec((2, 4, HALF), lambda b, c: (0, 0, c)),
        pl.BlockSpec((2, 1, HALF), lambda b, c: (0, 0, c)),
        pl.BlockSpec((2, 1, HALF, 2 * HALF), lambda b, c: (0, c, 0, 0)),
        pl.BlockSpec((2, 1, HALF), lambda b, c: (0, 0, c)),
        pl.BlockSpec((2, 1, HALF), lambda b, c: (0, 0, c)),
        pl.BlockSpec((2, 1, HALF), lambda b, c: (0, 0, c)),
    ]
    args += [p["conv_a_w"], p["rg_conv_w"], p["rg_conv_b"], p["rg_wg"], p["rg_ba"], p["rg_bx"], p["rg_lambda"]]
    out_specs = [pl.BlockSpec((1, n, HALF), lambda b, c: (b, 0, c))] * 2
    out_shape = [jax.ShapeDtypeStruct((bsz, n, CONV_W), BF16), jax.ShapeDtypeStruct((bsz, n, RG_W), BF16)]
    if need_ctx_out:
        out_specs += [pl.BlockSpec((1, nc, HALF), lambda b, c: (b, 0, c))] * 2
        out_shape += [jax.ShapeDtypeStruct((bsz, nc, CONV_W), BF16), jax.ShapeDtypeStruct((bsz, nc, RG_W), BF16)]
    npad = max(n, nc) + 2 * PAD
    return pl.pallas_call(
        functools.partial(_seq_kernel, n=n, nc=nc, need_ctx_out=need_ctx_out),
        grid=(bsz, CONV_W // HALF),
        in_specs=in_specs,
        out_specs=out_specs,
        out_shape=out_shape,
        scratch_shapes=[pltpu.VMEM((npad, HALF), F32), pltpu.VMEM((2, u, max(n, nc), LANES), F32),
                        pltpu.VMEM((2, u, max(n, nc), LANES), F32)],
        compiler_params=_cparams(("parallel", "parallel")),
        name="seq_mixers",
    )(*args)


def _attn_kernel(*refs, has_lat, lam_init):
    if has_lat:
        q_ref, k_ref, v_ref, ck_ref, cv_ref, lam_ref, sw_ref, o_ref, k_all, v_ext = refs
    else:
        q_ref, ck_ref, cv_ref, lam_ref, sw_ref, o_ref, k_all, v_ext = refs
    nc = ck_ref.shape[1]
    nk = k_all.shape[0]

    @pl.when(pl.program_id(1) == 0)
    def _():
        k_all[0:nc, :] = ck_ref[0]
        if has_lat:
            k_all[nc:nk, :] = k_ref[0]
        for h in range(DA_HEADS):
            sl = slice(h * LANES, (h + 1) * LANES)
            v_ext[h, 0:nc, 0:LANES] = cv_ref[0, :, sl]
            if has_lat:
                v_ext[h, nc:nk, 0:LANES] = v_ref[0, :, sl]
            v_ext[h, :, LANES:2 * LANES] = jnp.ones((nk, LANES), v_ext.dtype)

    lane = lax.broadcasted_iota(jnp.int32, (1, LANES), 1)
    map1 = (lane % DA_HEAD_DIM) < (DA_HEAD_DIM // 2)
    dn = (((1,), (1,)), ((), ()))
    lam = lam_ref[0:1, 0:1]

    for h in range(DA_HEADS):
        sl = slice(h * LANES, (h + 1) * LANES)
        q = q_ref[0, :, sl]
        zero = jnp.zeros_like(q)

        def soft_v(qm):
            s = lax.dot_general(qm, k_all[:, sl], dn, preferred_element_type=F32)
            e = jnp.exp2(s - jnp.max(s, axis=-1, keepdims=True)).astype(BF16)
            r = jnp.dot(e, v_ext[h], preferred_element_type=F32)
            return r[:, :LANES] / r[:, LANES:LANES + 1]

        o = soft_v(jnp.where(map1, q, zero)) - lam * soft_v(jnp.where(map1, zero, q))
        y = _rms(o) * sw_ref[...] * (1.0 - lam_init)
        o_ref[0, :, sl] = y.astype(o_ref.dtype)


def _attn_call(zq, z, zc, zc_off, lam, subw, lam_init, has_lat):
    bsz, nq, _ = zq.shape
    nc = zc.shape[1]
    tq = min(ATTN_Q_ROWS, nq)
    w = DA_HEADS * LANES
    u = DA_HEADS
    in_specs = [pl.BlockSpec((1, tq, w), lambda b, i: (b, i, Z_FULL["q"] // u))]
    args = [zq]
    nk = nc
    if has_lat:
        n = z.shape[1]
        nk = nc + n
        in_specs += [pl.BlockSpec((1, n, w), lambda b, i: (b, 0, Z_FULL["k"] // u)),
                     pl.BlockSpec((1, n, w), lambda b, i: (b, 0, Z_FULL["v"] // u))]
        args += [z, z]
    in_specs += [pl.BlockSpec((1, nc, w), lambda b, i: (b, 0, zc_off["k"] // u)),
                 pl.BlockSpec((1, nc, w), lambda b, i: (b, 0, zc_off["v"] // u)),
                 pl.BlockSpec((8, LANES), lambda b, i: (0, 0)),
                 pl.BlockSpec((1, LANES), lambda b, i: (0, 0))]
    args += [zc, zc, lam, subw]
    return pl.pallas_call(
        functools.partial(_attn_kernel, has_lat=has_lat, lam_init=lam_init),
        grid=(bsz, nq // tq),
        in_specs=in_specs,
        out_specs=pl.BlockSpec((1, tq, w), lambda b, i: (b, i, 0)),
        out_shape=jax.ShapeDtypeStruct((bsz, nq, DA_HEADS * DA_V_DIM), BF16),
        scratch_shapes=[pltpu.VMEM((nk, w), BF16), pltpu.VMEM((DA_HEADS, nk, 2 * LANES), BF16)],
        compiler_params=_cparams(("parallel", "arbitrary")),
        name="diff_attn",
    )(*args)


def _merge_kernel(*refs, route):
    if route:
        (ya_ref, yr_ref, yd_ref, ga_ref, gr_ref, gd_ref, x_ref, g1_ref, nw_ref, sh_ref, sc_ref,
         wb_ref, wo_ref, rw_ref, xo_ref, h_ref, rt_ref) = refs
    else:
        (ya_ref, yr_ref, yd_ref, ga_ref, gr_ref, gd_ref, x_ref, g1_ref, nw_ref, sh_ref, sc_ref,
         wb_ref, wo_ref, xo_ref, h_ref) = refs
    m = None
    for i, (y_ref, g_ref) in enumerate(((ya_ref, ga_ref), (yr_ref, gr_ref), (yd_ref, gd_ref))):
        t = _sigmoid(g_ref[0].astype(F32)) * jnp.dot(y_ref[0], wb_ref[i], preferred_element_type=F32)
        m = t if m is None else m + t
    out = jnp.dot(m.astype(BF16), wo_ref[...], preferred_element_type=F32)
    xn = x_ref[0] + g1_ref[0] * out
    xo_ref[0] = xn
    h = (_rms(xn) * nw_ref[...]) * (1.0 + sc_ref[0]) + sh_ref[0]
    hb = h.astype(BF16)
    h_ref[0] = hb
    if route:
        logits = jnp.dot(hb, rw_ref[...], preferred_element_type=F32)
        lane = lax.broadcasted_iota(jnp.int32, logits.shape, 1)
        neg = jnp.float32(-jnp.inf)
        l1 = jnp.where(lane < N_EXPERTS, logits, neg)
        m1 = jnp.max(l1, axis=-1, keepdims=True)
        i1 = jnp.min(jnp.where(l1 == m1, lane, LANES), axis=-1, keepdims=True)
        l2 = jnp.where(lane == i1, neg, l1)
        m2 = jnp.max(l2, axis=-1, keepdims=True)
        i2 = jnp.min(jnp.where(l2 == m2, lane, LANES), axis=-1, keepdims=True)
        ex = jnp.exp(m2 - m1)
        gt1 = 1.0 / (1.0 + ex)
        gt2 = ex * gt1
        rt = jnp.where(lane == 0, i1.astype(F32),
                       jnp.where(lane == 1, i2.astype(F32),
                                 jnp.where(lane == 2, gt1, jnp.where(lane == 3, gt2, 0.0))))
        rt_ref[0] = rt


def _merge_call(ya, yr, yd, z, x, mod3, mod_row, nw2, wb, wo, router_w=None):
    bsz, n, d = x.shape
    tm = min(512, n)
    route = router_w is not None
    tok = lambda b, i: (b, i, 0)
    gu = d // LANES
    in_specs = [
        pl.BlockSpec((1, tm, CONV_W), tok), pl.BlockSpec((1, tm, RG_W), tok),
        pl.BlockSpec((1, tm, DA_HEADS * DA_V_DIM), tok),
        pl.BlockSpec((1, tm, d), lambda b, i: (b, i, Z_FULL["g_a"] // gu)),
        pl.BlockSpec((1, tm, d), lambda b, i: (b, i, Z_FULL["g_r"] // gu)),
        pl.BlockSpec((1, tm, d), lambda b, i: (b, i, Z_FULL["g_d"] // gu)),
        pl.BlockSpec((1, tm, d), tok),
        pl.BlockSpec((1, 1, d), lambda b, i: (mod_row(b), 0, 2)),
        pl.BlockSpec((1, d), lambda b, i: (0, 0)),
        pl.BlockSpec((1, 1, d), lambda b, i: (mod_row(b), 0, 3)),
        pl.BlockSpec((1, 1, d), lambda b, i: (mod_row(b), 0, 4)),
        pl.BlockSpec(wb.shape, lambda b, i: (0, 0, 0)),
        pl.BlockSpec(wo.shape, lambda b, i: (0, 0)),
    ]
    args = [ya, yr, yd, z, z, z, x, mod3, nw2, mod3, mod3, wb, wo]
    out_specs = [pl.BlockSpec((1, tm, d), tok), pl.BlockSpec((1, tm, d), tok)]
    out_shape = [jax.ShapeDtypeStruct((bsz, n, d), F32),
                 jax.ShapeDtypeStruct((bsz, n, d), BF16)]
    if route:
        in_specs.append(pl.BlockSpec(router_w.shape, lambda b, i: (0, 0)))
        args.append(router_w)
        out_specs.append(pl.BlockSpec((1, tm, LANES), tok))
        out_shape.append(jax.ShapeDtypeStruct((bsz, n, LANES), F32))
    return pl.pallas_call(
        functools.partial(_merge_kernel, route=route),
        grid=(bsz, n // tm),
        in_specs=in_specs,
        out_specs=out_specs,
        out_shape=out_shape,
        compiler_params=_cparams(("parallel", "parallel")),
        name="merge",
    )(*args)


def _ffn_kernel(h_ref, x_ref, g2_ref, wg_ref, wu_ref, w2_ref, o_ref, acc_ref, *, nf):
    f = pl.program_id(2)
    h = h_ref[0]
    g = jnp.dot(h, wg_ref[...], preferred_element_type=F32)
    u = jnp.dot(h, wu_ref[...], preferred_element_type=F32)
    act = (_silu(g) * u).astype(BF16)

    def down():
        return jnp.dot(act, w2_ref[...], preferred_element_type=F32)

    if nf == 1:
        o_ref[0] = x_ref[0] + g2_ref[0] * down()
        return

    @pl.when(f == 0)
    def _():
        acc_ref[...] = down()

    if nf > 2:
        @pl.when((f > 0) & (f < nf - 1))
        def _():
            acc_ref[...] += down()

    @pl.when(f == nf - 1)
    def _():
        o_ref[0] = x_ref[0] + g2_ref[0] * (acc_ref[...] + down())


def _ffn_call(h, x, mod3, mod_row, w13, w2):
    bsz, n, d = x.shape
    ff = w2.shape[0]
    tm = min(SWIGLU_ROWS, n)
    tf = SWIGLU_FF_TILE
    nf = ff // tf
    tok = lambda b, i, f: (b, i, 0)
    return pl.pallas_call(
        functools.partial(_ffn_kernel, nf=nf),
        grid=(bsz, n // tm, nf),
        in_specs=[
            pl.BlockSpec((1, tm, d), tok),
            pl.BlockSpec((1, tm, d), tok),
            pl.BlockSpec((1, 1, d), lambda b, i, f: (mod_row(b), 0, 5)),
            pl.BlockSpec((d, tf), lambda b, i, f: (0, f)),
            pl.BlockSpec((d, tf), lambda b, i, f: (0, nf + f)),
            pl.BlockSpec((tf, d), lambda b, i, f: (f, 0)),
        ],
        out_specs=pl.BlockSpec((1, tm, d), tok),
        out_shape=jax.ShapeDtypeStruct((bsz, n, d), F32),
        scratch_shapes=[pltpu.VMEM((tm, d), F32)],
        compiler_params=_cparams(("parallel", "parallel", "arbitrary")),
        name="ffn",
    )(h, x, mod3, w13, w13, w2)


def _moe_kernel(be_ref, nu_ref, x_ref, wg_ref, wu_ref, w2_ref, o_ref, xb):
    i = pl.program_id(0)
    f = pl.program_id(1)
    used = i < nu_ref[0]

    def swiglu(x):
        g = jnp.dot(x, wg_ref[0], preferred_element_type=F32)
        u = jnp.dot(x, wu_ref[0], preferred_element_type=F32)
        act = (_silu(g) * u).astype(BF16)
        return jnp.dot(act, w2_ref[0], preferred_element_type=F32)

    @pl.when(used & (f == 0))
    def _():
        x = x_ref[...].astype(BF16)
        xb[...] = x
        o_ref[...] = swiglu(x)

    @pl.when(used & (f > 0))
    def _():
        o_ref[...] += swiglu(xb[...])

    @pl.when(jnp.logical_not(used) & (f == 0))
    def _():
        o_ref[...] = jnp.zeros(o_ref.shape, o_ref.dtype)


def _moe_call(xs, block_e, n_used, w13, w2, rows):
    n_rows, d = xs.shape
    n_blocks = n_rows // rows
    ff = w2.shape[1]
    tf = SWIGLU_FF_TILE
    nf = ff // tf

    def bi(i, nu):
        return jnp.minimum(i, nu[0] - 1)

    def fi(i, f, nu):
        return jnp.where(i < nu[0], f, nf - 1)

    grid_spec = pltpu.PrefetchScalarGridSpec(
        num_scalar_prefetch=2,
        grid=(n_blocks, nf),
        in_specs=[
            pl.BlockSpec((rows, d), lambda i, f, be, nu: (bi(i, nu), 0)),
            pl.BlockSpec((1, d, tf), lambda i, f, be, nu: (be[i], 0, fi(i, f, nu))),
            pl.BlockSpec((1, d, tf), lambda i, f, be, nu: (be[i], 0, nf + fi(i, f, nu))),
            pl.BlockSpec((1, tf, d), lambda i, f, be, nu: (be[i], fi(i, f, nu), 0)),
        ],
        out_specs=pl.BlockSpec((rows, d), lambda i, f, be, nu: (i, 0)),
        scratch_shapes=[pltpu.VMEM((rows, d), BF16)],
    )
    return pl.pallas_call(
        _moe_kernel,
        grid_spec=grid_spec,
        out_shape=jax.ShapeDtypeStruct((n_rows, d), F32),
        compiler_params=_cparams(("arbitrary", "arbitrary")),
        name="moe",
    )(block_e, n_used, xs, w13, w13, w2)


GROUP_PAD = 8
GROUP_SIZES = tuple(SWIGLU_ROWS >> s for s in range(SWIGLU_ROWS.bit_length()) if SWIGLU_ROWS >> s >= GROUP_PAD)


def _sorted_rows(tm):
    return 2 * tm + N_EXPERTS * GROUP_PAD


def _group_copies(step, src_of, dst_of, loc_ref, base_ref, pc_ref, sem, sizes):
    for e in range(N_EXPERTS):
        loc = loc_ref[step * N_EXPERTS + e]
        base = base_ref[step * N_EXPERTS + e]
        pc = pc_ref[step * N_EXPERTS + e]
        for size in sizes:
            before = pc & (-2 * size)
            src = src_of(pl.multiple_of(loc + before, GROUP_PAD), pl.multiple_of(base + before, GROUP_PAD), size)
            dst = dst_of(pl.multiple_of(loc + before, GROUP_PAD), pl.multiple_of(base + before, GROUP_PAD), size)
            yield (pc & size) != 0, pltpu.make_async_copy(src, dst, sem)


def _dispatch2_kernel(loc_ref, base_ref, pc_ref, tail_ref, nu_ref, pos_ref, h_ref, xs_hbm, srt, zeros, sem, zsem,
                      *, tm, nt, rows, sizes):
    i = pl.program_id(0)
    n = pl.num_programs(0)
    slot = i % 2
    sr = srt.shape[1]
    n_blocks = xs_hbm.shape[0] // rows
    min_used = 2 * tm * nt // rows

    def tail_copies():
        for e in range(N_EXPERTS):
            start = tail_ref[e]
            length = tail_ref[N_EXPERTS + e]
            for size in sizes[1:]:
                before = length & (-2 * size)
                dst = xs_hbm.at[pl.ds(pl.multiple_of(start + before, GROUP_PAD), size)]
                yield (length & size) != 0, pltpu.make_async_copy(zeros.at[pl.ds(0, size)], dst, zsem)
        for b in range(min_used, n_blocks):
            dst = xs_hbm.at[pl.ds(b * rows, rows)]
            yield b >= nu_ref[0], pltpu.make_async_copy(zeros, dst, zsem)

    @pl.when(i == 0)
    def _():
        zeros[...] = jnp.zeros(zeros.shape, zeros.dtype)
        for pred, cp in tail_copies():
            pl.when(pred)(cp.start)

    r = lax.broadcasted_iota(jnp.int32, (sr, tm), 0)
    hit = (r == pos_ref[0, 0:1, :]) | (r == pos_ref[0, 1:2, :])
    sel = jnp.where(hit, 1.0, 0.0).astype(BF16)
    srt[slot] = jnp.dot(sel, h_ref[...], preferred_element_type=F32)

    def group_copies(step, s):
        return _group_copies(step, lambda lo, ba, sz: srt.at[s, pl.ds(lo, sz)],
                             lambda lo, ba, sz: xs_hbm.at[pl.ds(ba, sz)], loc_ref, base_ref, pc_ref, sem.at[s], sizes)

    for pred, cp in group_copies(i, slot):
        pl.when(pred)(cp.start)

    @pl.when(i > 0)
    def _():
        for pred, cp in group_copies(i - 1, 1 - slot):
            pl.when(pred)(cp.wait)

    @pl.when(i == n - 1)
    def _():
        for pred, cp in group_copies(i, slot):
            pl.when(pred)(cp.wait)
        for pred, cp in tail_copies():
            pl.when(pred)(cp.wait)


def _dispatch2_call(h2, plan, tm, rows):
    t, d = h2.shape
    nt = t // tm
    sr = _sorted_rows(tm)
    sizes = tuple(s for s in GROUP_SIZES if s <= tm)
    grid_spec = pltpu.PrefetchScalarGridSpec(
        num_scalar_prefetch=5,
        grid=(nt,),
        in_specs=[pl.BlockSpec((1, 2, tm), lambda i, *_: (i, 0, 0)),
                  pl.BlockSpec((tm, d), lambda i, *_: (i, 0))],
        out_specs=pl.BlockSpec(memory_space=pl.ANY),
        scratch_shapes=[pltpu.VMEM((2, sr, d), F32), pltpu.VMEM((rows, d), F32),
                        pltpu.SemaphoreType.DMA((2,)), pltpu.SemaphoreType.DMA(())],
    )
    return pl.pallas_call(
        functools.partial(_dispatch2_kernel, tm=tm, nt=nt, rows=rows, sizes=sizes),
        grid_spec=grid_spec,
        out_shape=jax.ShapeDtypeStruct((plan["n_blocks"] * rows, d), F32),
        compiler_params=pltpu.CompilerParams(dimension_semantics=("arbitrary",), vmem_limit_bytes=VMEM_LIMIT),
        name="moe_dispatch",
    )(plan["loc"], plan["base"], plan["pc"], plan["tail"], plan["n_used"], plan["pos_rows"], h2)


def _combine2_kernel(loc_ref, base_ref, pc_ref, x_ref, pos_ref, rt_ref, g2_ref, fw_ref, ys_hbm, o_ref, buf, sem,
                     *, tm, sizes):
    i = pl.program_id(0)
    n = pl.num_programs(0)
    slot = i % 2
    sr = buf.shape[1]

    def group_copies(step, s):
        return _group_copies(step, lambda lo, ba, sz: ys_hbm.at[pl.ds(ba, sz)],
                             lambda lo, ba, sz: buf.at[s, pl.ds(lo, sz)], loc_ref, base_ref, pc_ref, sem.at[s], sizes)

    @pl.when(i == 0)
    def _():
        buf[...] = jnp.zeros(buf.shape, buf.dtype)
        for pred, cp in group_copies(0, 0):
            pl.when(pred)(cp.start)

    @pl.when(i + 1 < n)
    def _():
        for pred, cp in group_copies(i + 1, 1 - slot):
            pl.when(pred)(cp.start)

    for pred, cp in group_copies(i, slot):
        pl.when(pred)(cp.wait)

    rt = rt_ref[...]
    col = lax.broadcasted_iota(jnp.int32, (tm, sr), 1)
    pos = pos_ref[...]
    w = jnp.where(col == pos[:, 0:1], rt[:, 2:3], jnp.where(col == pos[:, 1:2], rt[:, 3:4], 0.0))
    y = jnp.dot(w.astype(BF16), buf[slot].astype(BF16), preferred_element_type=F32)
    xn = x_ref[...] + g2_ref[0] * y
    o_ref[...] = _rms(xn) * fw_ref[...]


def _combine2_call(x2, ys, plan, route2, mod3, n_per_seq, fw, tm):
    t, d = x2.shape
    nt = t // tm
    per = n_per_seq // tm
    sr = _sorted_rows(tm)
    sizes = tuple(s for s in GROUP_SIZES if s <= tm)
    grid_spec = pltpu.PrefetchScalarGridSpec(
        num_scalar_prefetch=3,
        grid=(nt,),
        in_specs=[
            pl.BlockSpec((tm, d), lambda i, *_: (i, 0)),
            pl.BlockSpec((tm, 2), lambda i, *_: (i, 0)),
            pl.BlockSpec((tm, LANES), lambda i, *_: (i, 0)),
            pl.BlockSpec((1, 1, d), lambda i, *_: (i // per, 0, 5)),
            pl.BlockSpec((1, d), lambda i, *_: (0, 0)),
            pl.BlockSpec(memory_space=pl.ANY),
        ],
        out_specs=pl.BlockSpec((tm, d), lambda i, *_: (i, 0)),
        scratch_shapes=[pltpu.VMEM((2, sr, d), F32), pltpu.SemaphoreType.DMA((2,))],
    )
    return pl.pallas_call(
        functools.partial(_combine2_kernel, tm=tm, sizes=sizes),
        grid_spec=grid_spec,
        out_shape=jax.ShapeDtypeStruct((t, d), F32),
        compiler_params=pltpu.CompilerParams(dimension_semantics=("arbitrary",), vmem_limit_bytes=VMEM_LIMIT),
        name="combine_final",
    )(plan["loc"], plan["base"], plan["pc"], x2, plan["pos_cols"], route2, mod3, fw, ys)


def _sort_plan(route, tm, rows):
    t = route.shape[0]
    nt = t // tm
    experts = jnp.arange(N_EXPERTS, dtype=jnp.int32)
    e = route[:, :2].astype(jnp.int32).reshape(nt, 2 * tm)
    hot = (e[:, :, None] == experts[None, None, :]).astype(jnp.int32)
    cs = jnp.cumsum(hot, axis=1)
    rank = jnp.sum(cs * hot, axis=2) - 1
    cnt = cs[:, -1, :]
    pc = (cnt + GROUP_PAD - 1) // GROUP_PAD * GROUP_PAD
    loc = jnp.cumsum(pc, axis=1) - pc
    pos = jnp.sum(hot * loc[:, None, :], axis=2) + rank
    tot = jnp.sum(pc, axis=0)
    padded = (tot + rows - 1) // rows * rows
    e_end = jnp.cumsum(padded)
    e_start = e_end - padded
    base = e_start[None, :] + jnp.cumsum(pc, axis=0) - pc
    n_blocks = -(-(2 * t + nt * N_EXPERTS * (GROUP_PAD - 1)) // rows) + N_EXPERTS
    block_start = jnp.arange(n_blocks, dtype=jnp.int32) * rows
    block_e = jnp.minimum(jnp.sum((block_start[:, None] >= e_end[None, :]).astype(jnp.int32), axis=1),
                          N_EXPERTS - 1)
    pos3 = pos.reshape(nt, tm, 2)
    i32 = lambda a: a.astype(jnp.int32)
    return dict(
        loc=i32(loc.reshape(-1)), base=i32(base.reshape(-1)), pc=i32(pc.reshape(-1)),
        tail=i32(jnp.concatenate([e_start + tot, padded - tot])),
        n_used=i32(e_end[-1] // rows).reshape(1), block_e=i32(block_e), n_blocks=n_blocks,
        pos_rows=i32(jnp.swapaxes(pos3, 1, 2)), pos_cols=i32(pos3.reshape(t, 2)))


def _norm_kernel(x_ref, fw_ref, o_ref):
    o_ref[0] = _rms(x_ref[0]) * fw_ref[...]


def _norm_call(x, fw):
    bsz, n, d = x.shape
    tm = min(512, n)
    tok = lambda b, i: (b, i, 0)
    return pl.pallas_call(
        _norm_kernel,
        grid=(bsz, n // tm),
        in_specs=[pl.BlockSpec((1, tm, d), tok), pl.BlockSpec((1, d), lambda b, i: (0, 0))],
        out_specs=pl.BlockSpec((1, tm, d), tok),
        out_shape=jax.ShapeDtypeStruct((bsz, n, d), F32),
        compiler_params=_cparams(("parallel", "parallel")),
        name="final_norm",
    )(x, fw)


def _rope_perm():
    perm = np.zeros((LANES,), np.int32)
    half = ROPE_AXIS_DIM // 2
    for i in range(LANES // 2):
        comp, axis, j = i // 32, (i % 32) // half, i % half
        perm[i] = comp * DA_HEAD_DIM + axis * ROPE_AXIS_DIM + j
        perm[LANES // 2 + i] = perm[i] + half
    return perm


def _rope_tables(n):
    rows = n // GRID_W
    pos_r = jnp.repeat(jnp.arange(rows, dtype=F32), GRID_W)
    pos_c = jnp.broadcast_to(jnp.arange(GRID_W, dtype=F32), (rows, GRID_W)).reshape(-1)
    inv_freq = ROPE_BASE ** (-jnp.arange(0, ROPE_AXIS_DIM, 2, dtype=F32) / ROPE_AXIS_DIM)
    ang = jnp.stack([pos_r[:, None] * inv_freq, pos_c[:, None] * inv_freq], axis=1)
    cos, sin = jnp.cos(ang), jnp.sin(ang)
    half = jnp.concatenate([cos.reshape(n, 32), cos.reshape(n, 32)], axis=1)
    cos_t = jnp.concatenate([half, half], axis=1)
    sh = jnp.concatenate([sin.reshape(n, 32), sin.reshape(n, 32)], axis=1)
    sin_t = jnp.concatenate([-sh, sh], axis=1)
    return cos_t, sin_t


def _blockdiag_gate(wa, wx):
    per = HALF // (RG_W // RG_BLOCKS)
    eye = jnp.eye(per, dtype=wa.dtype)

    def bd(w):
        w4 = w.reshape(2, RG_BLOCKS // per, per, w.shape[-2], w.shape[-1])
        return jnp.einsum("dhbij,bc->dhbicj", w4, eye).reshape(2, RG_BLOCKS // per, HALF, HALF)

    return jnp.concatenate([bd(wa), bd(wx)], axis=-1).astype(BF16)


def kernel(x, c, ctx, c_ctx, mod_w, mod_b, norm1_w, norm2_w, w_in, conv_a_w, rg_conv_w, rg_conv_b, rg_wa, rg_ba, rg_wx, rg_bx, rg_lambda, da_lambda, da_subln_w, w_branch, w_out, ffn_w13, ffn_w2, router_w, moe_w13, moe_w2, final_norm_w):
    bsz, n, d = x.shape
    nc = ctx.shape[1]
    depth = mod_w.shape[0]
    cos_t, sin_t = _rope_tables(n)
    perm = _rope_perm()
    ref_cols = {name: np.arange(start, start + width, dtype=np.int32) for name, start, width in REF_COLS}
    for name in ("q", "k"):
        for h in range(DA_HEADS):
            base = ref_cols[name][0] + h * LANES
            ref_cols[name][h * LANES:(h + 1) * LANES] = base + perm
    z_cols = np.concatenate([ref_cols[name] for name in Z_ORDER])

    bp = -(-(bsz + 1) // 8) * 8
    cc = jnp.zeros((bp, d), F32).at[:bsz].set(c).at[bsz].set(c_ctx)
    lat_row = lambda b: b
    ctx_row = lambda b: bsz
    cx = ctx
    moe_rows = SWIGLU_ROWS if bsz * n >= 8192 else 128

    for l in range(depth):
        last = l == depth - 1
        lam_init = 0.8 - 0.6 * math.exp(-0.3 * l)
        j = l // 2
        mod, lam = _mod_call(cc, mod_w[l], mod_b[l][None, :], da_lambda[l], lam_init)
        mod3 = mod.reshape(bp, 1, N_MOD * d)
        w_in_l = w_in[l][:, z_cols].astype(BF16)
        nw1 = norm1_w[l][None, :]
        nw2 = norm2_w[l][None, :]
        p = dict(
            conv_a_w=conv_a_w[l], rg_conv_w=rg_conv_w[l], rg_conv_b=rg_conv_b[l][:, None, :],
            rg_wg=_blockdiag_gate(rg_wa[l], rg_wx[l]),
            rg_ba=rg_ba[l].reshape(2, 1, RG_W), rg_bx=rg_bx[l].reshape(2, 1, RG_W),
            rg_lambda=rg_lambda[l][:, None, :])
        subw = da_subln_w[l][None, :]
        wb = w_branch[l].astype(BF16)
        wo = w_out[l].astype(BF16)

        ncol = w_in.shape[2] // COL
        z = _in_call(x, nw1, mod3, lat_row, w_in_l, range(ncol), True, cos_t, sin_t)
        flat = lambda a: a.reshape((1, bsz * nc) + a.shape[2:])
        unflat = lambda a: a.reshape((bsz, nc) + a.shape[2:])
        if last:
            zc = unflat(_in_call(flat(cx), nw1, mod3, ctx_row, w_in_l, CTX_LAST_BLOCKS, False))
            zc_off = Z_CTX_LAST
        else:
            zc = unflat(_in_call(flat(cx), nw1, mod3, ctx_row, w_in_l, range(ncol), False))
            zc_off = Z_FULL

        seq_out = _seq_call(z, zc, zc_off, p, not last)
        yd = _attn_call(z, z, zc, zc_off, lam, subw, lam_init, True)
        moe_layer = l % 2 == 1
        if moe_layer:
            rw = jnp.zeros((d, LANES), BF16).at[:, :N_EXPERTS].set(router_w[j].astype(BF16))
            xn, h, route = _merge_call(seq_out[0], seq_out[1], yd, z, x, mod3, lat_row, nw2, wb, wo, rw)
            t = bsz * n
            tm = min(512, n)
            route2 = route.reshape(t, LANES)
            plan = _sort_plan(route2, tm, moe_rows)
            xs = _dispatch2_call(h.reshape(t, d), plan, tm, moe_rows)
            ys = _moe_call(xs, plan["block_e"], plan["n_used"], moe_w13[j].astype(BF16),
                           moe_w2[j].astype(BF16), moe_rows)
            if last:
                out = _combine2_call(xn.reshape(t, d), ys, plan, route2, mod3, n, final_norm_w[None, :], tm)
                return out.reshape(bsz, n, d)
            raise NotImplementedError("expert layer followed by another layer")
        xn, h = _merge_call(seq_out[0], seq_out[1], yd, z, x, mod3, lat_row, nw2, wb, wo)
        w13 = ffn_w13[j].astype(BF16)
        w2 = ffn_w2[j].astype(BF16)
        x = _ffn_call(h, xn, mod3, lat_row, w13, w2)
        if not last:
            ycd = _attn_call(zc, None, zc, zc_off, lam, subw, lam_init, False)
            cxn, hc = _merge_call(flat(seq_out[2]), flat(seq_out[3]), flat(ycd), flat(zc), flat(cx), mod3,
                                  ctx_row, nw2, wb, wo)
            cx = unflat(_ffn_call(hc, cxn, mod3, ctx_row, w13, w2))
    return _norm_call(x, final_norm_w[None, :])
```

```python
import functools
import math

import jax
import jax.numpy as jnp
import numpy as np
from jax import lax
from jax.experimental import pallas as pl
from jax.experimental.pallas import tpu as pltpu

F32 = jnp.float32
BF16 = jnp.bfloat16

EPS = 1e-6
N_MOD = 6
GRID_W = 64
CONV_W = 512
RG_W = 512
RG_BLOCKS = 8
RG_C = 8.0
DA_HEADS = 4
DA_HEAD_DIM = 64
DA_V_DIM = 128
ROPE_AXIS_DIM = 32
ROPE_BASE = 10000.0
N_EXPERTS = 8
LANES = 128
HALF = 256
VMEM_LIMIT = 56 * 1024 * 1024
IN_PROJ_ROWS = 2048
IN_PROJ_CHUNK = 256
ATTN_Q_ROWS = 512
SWIGLU_ROWS = 512
SWIGLU_FF_TILE = 1792

COL = 1024
REF_COLS = (("a_b", 0, 512), ("a_c", 512, 512), ("a_x", 1024, 512), ("r_g", 1536, 512), ("r_x", 2048, 512),
            ("q", 2560, 512), ("k", 3072, 512), ("v", 3584, 512), ("g_a", 4096, 1024), ("g_r", 5120, 1024),
            ("g_d", 6144, 1024))
Z_ORDER = ("q", "k", "v", "a_b", "a_c", "a_x", "r_g", "r_x", "g_a", "g_r", "g_d")
Z_FULL = dict(q=0, k=4, v=8, a_b=12, a_c=16, a_x=20, r_g=24, r_x=28, g_a=32, g_r=40, g_d=48)
CTX_LAST_BLOCKS = (0, 1, 3)
Z_CTX_LAST = dict(k=4, v=8, r_x=20)
LOG2E = 1.4426950408889634


def _cparams(sem, vmem=VMEM_LIMIT):
    return pltpu.CompilerParams(dimension_semantics=sem, vmem_limit_bytes=vmem)


def _sigmoid(x):
    return 1.0 / (1.0 + jnp.exp2(x * (-LOG2E)))


def _silu(x):
    return x * _sigmoid(x)


def _gelu_tanh(x):
    return 0.5 * x * (1.0 + jnp.tanh(math.sqrt(2.0 / math.pi) * (x + 0.044715 * (x * x * x))))


def _rms(x):
    return x * lax.rsqrt(jnp.mean(x * x, axis=-1, keepdims=True) + EPS)


def _mod_kernel(cc_ref, w_ref, b_ref, dl_ref, mod_ref, lam_ref, *, lam_init):
    cc = cc_ref[...]
    s = _silu(cc)
    mod_ref[...] = jnp.dot(s, w_ref[...], preferred_element_type=F32,
                           precision=lax.Precision.HIGHEST) + b_ref[...]
    dl = dl_ref[...]
    s1 = jnp.sum(dl[0:1] * dl[1:2], axis=-1, keepdims=True)
    s2 = jnp.sum(dl[2:3] * dl[3:4], axis=-1, keepdims=True)
    lam = jnp.exp(s1) - jnp.exp(s2) + lam_init
    lam_ref[...] = jnp.broadcast_to(lam, lam_ref.shape)


def _mod_call(cc, w, b, dl, lam_init):
    bp, d = cc.shape
    nd = w.shape[1]
    tn = 1024
    return pl.pallas_call(
        functools.partial(_mod_kernel, lam_init=lam_init),
        grid=(nd // tn,),
        in_specs=[
            pl.BlockSpec((bp, d), lambda j: (0, 0)),
            pl.BlockSpec((d, tn), lambda j: (0, j)),
            pl.BlockSpec((1, tn), lambda j: (0, j)),
            pl.BlockSpec(dl.shape, lambda j: (0, 0)),
        ],
        out_specs=[
            pl.BlockSpec((bp, tn), lambda j: (0, j)),
            pl.BlockSpec((8, LANES), lambda j: (0, 0)),
        ],
        out_shape=[jax.ShapeDtypeStruct((bp, nd), F32), jax.ShapeDtypeStruct((8, LANES), F32)],
        compiler_params=_cparams(("arbitrary",)),
        name="mod",
    )(cc, w, b, dl)


def _in_kernel(*refs, rope):
    if rope:
        x_ref, nw_ref, sh_ref, sc_ref, cos_ref, sin_ref, w_ref, o_ref, u_scr = refs
    else:
        x_ref, nw_ref, sh_ref, sc_ref, w_ref, o_ref, u_scr = refs
    j = pl.program_id(2)
    q_scale = DA_HEAD_DIM ** -0.5 * LOG2E
    n_q = DA_HEADS * 2 * DA_HEAD_DIM // LANES
    tm = x_ref.shape[1]
    chunk = min(IN_PROJ_CHUNK, tm)

    @pl.when(j == 0)
    def _():
        for c in range(tm // chunk):
            rows = slice(c * chunk, (c + 1) * chunk)
            y = _rms(x_ref[0, rows, :]) * nw_ref[...]
            u = (y * (1.0 + sc_ref[0]) + sh_ref[0]).astype(BF16)
            u_scr[rows, :] = u
            acc = jnp.dot(u, w_ref[...], preferred_element_type=F32)
            for g in range(acc.shape[1] // LANES):
                r = acc[:, g * LANES:(g + 1) * LANES]
                if rope:
                    r = r * cos_ref[rows, :] + pltpu.roll(r, LANES // 2, 1) * sin_ref[rows, :]
                if g < n_q:
                    r = r * q_scale
                o_ref[0, rows, g * LANES:(g + 1) * LANES] = r.astype(o_ref.dtype)

    @pl.when(j != 0)
    def _():
        o_ref[0] = jnp.dot(u_scr[...], w_ref[...], preferred_element_type=F32).astype(o_ref.dtype)


def _in_call(x, nw, mod3, mod_row, w, col_blocks, rope, cos_t=None, sin_t=None):
    bsz, n, d = x.shape
    tm = min(IN_PROJ_ROWS, n)
    nj = len(col_blocks)
    cb = tuple(col_blocks)
    if cb == tuple(range(nj)):
        wmap = lambda b, i, j: (0, j)
    else:
        assert cb == CTX_LAST_BLOCKS
        wmap = lambda b, i, j: (0, j + j // 2)
    in_specs = [
        pl.BlockSpec((1, tm, d), lambda b, i, j: (b, i, 0)),
        pl.BlockSpec((1, d), lambda b, i, j: (0, 0)),
        pl.BlockSpec((1, 1, d), lambda b, i, j: (mod_row(b), 0, 0)),
        pl.BlockSpec((1, 1, d), lambda b, i, j: (mod_row(b), 0, 1)),
    ]
    args = [x, nw, mod3, mod3]
    if rope:
        in_specs += [pl.BlockSpec((tm, LANES), lambda b, i, j: (i, 0)),
                     pl.BlockSpec((tm, LANES), lambda b, i, j: (i, 0))]
        args += [cos_t, sin_t]
    in_specs.append(pl.BlockSpec((d, COL), wmap))
    args.append(w)
    return pl.pallas_call(
        functools.partial(_in_kernel, rope=rope),
        grid=(bsz, n // tm, nj),
        in_specs=in_specs,
        out_specs=pl.BlockSpec((1, tm, COL), lambda b, i, j: (b, i, j)),
        out_shape=jax.ShapeDtypeStruct((bsz, n, nj * COL), BF16),
        scratch_shapes=[pltpu.VMEM((tm, d), BF16)],
        compiler_params=_cparams(("parallel", "parallel", "arbitrary")),
        name="in_proj",
    )(*args)


SEQ_TILE = 64
PAD = 8
SCAN_ROWS = 64


def _seq_kernel(*refs, n, nc, need_ctx_out):
    it = iter(refs)
    ab_ref, ac_ref, ax_ref, rg_ref, rx_ref = (next(it) for _ in range(5))
    if need_ctx_out:
        cab_ref, cac_ref, cax_ref, crg_ref = (next(it) for _ in range(4))
    crx_ref = next(it)
    wa_ref, cw_ref, cbias_ref, wg_ref, ba_ref, bx_ref, lam_ref = (next(it) for _ in range(7))
    ya_ref, yr_ref = next(it), next(it)
    if need_ctx_out:
        yca_ref, ycr_ref = next(it), next(it)
    xp, a_scr, b_scr = next(it), next(it), next(it)
    T = SEQ_TILE
    W = HALF

    def stage(val, ns):
        xp[0:PAD, :] = jnp.zeros((PAD, W), F32)
        xp[PAD:PAD + ns, :] = val
        xp[PAD + ns:2 * PAD + ns, :] = jnp.zeros((PAD, W), F32)

    def conv_mixer(b_ref, c_ref, x_ref, out_ref, ns):
        stage(c_ref[0].astype(F32) * x_ref[0].astype(F32), ns)
        w0, w1, w2 = wa_ref[0:1, :], wa_ref[1:2, :], wa_ref[2:3, :]

        def body(ti, carry):
            t0 = pl.multiple_of(ti * T, T)
            ps = xp[pl.ds(t0, T + 2 * PAD), :]
            conv = w0 * ps[PAD - 1:PAD - 1 + T] + w1 * ps[PAD:PAD + T] + w2 * ps[PAD + 1:PAD + 1 + T]
            gate = b_ref[0, pl.ds(t0, T), :].astype(F32)
            out_ref[0, pl.ds(t0, T), :] = (gate * conv).astype(out_ref.dtype)
            return carry

        lax.fori_loop(0, ns // T, body, 0)

    nl = -lam_ref[...]
    sp = jnp.maximum(nl, 0.0) + jnp.log1p(jnp.exp(-jnp.abs(nl)))
    decay = sp * (-RG_C * LOG2E)
    NG = W // LANES

    def rglru(x_ref, ns, h0):
        stage(x_ref[0].astype(F32), ns)

        def gates_body(ti, carry):
            t0 = pl.multiple_of(ti * T, T)
            xs = xp[pl.ds(t0, T + 2 * PAD), :]
            for d in range(2):
                off = PAD - 3 if d == 0 else PAD
                xc = cbias_ref[d]
                for k in range(4):
                    xc = xc + cw_ref[d, k:k + 1, :] * xs[off + k:off + k + T]
                g = jnp.dot(xc.astype(BF16), wg_ref[d, 0], preferred_element_type=F32)
                r = _sigmoid(g[:, :W] + ba_ref[d])
                i = _sigmoid(g[:, W:] + bx_ref[d])
                a = jnp.exp2(r * decay[d])
                y = 1.0 - a * a
                bb = jnp.where(y > 0.0, y * lax.rsqrt(y), 0.0) * (i * xc)
                for g in range(NG):
                    a_scr[d, g, pl.ds(t0, T), :] = a[:, g * LANES:(g + 1) * LANES]
                    b_scr[d, g, pl.ds(t0, T), :] = bb[:, g * LANES:(g + 1) * LANES]
            return carry

        lax.fori_loop(0, ns // T, gates_body, 0, unroll=2)
        R = SCAN_ROWS
        nch = R // 8
        steps = ns // R
        sub = lax.broadcasted_iota(jnp.int32, (nch, 8, LANES), 1)

        def scan_body(ci, carry):
            out = []
            for d in range(2):
                r0 = pl.multiple_of((ci if d == 0 else steps - 1 - ci) * R, R)
                edge = 7 if d == 0 else 0
                for g in range(NG):
                    a = a_scr[d, g, pl.ds(r0, R), :].reshape(nch, 8, LANES)
                    b = b_scr[d, g, pl.ds(r0, R), :].reshape(nch, 8, LANES)
                    for s in (1, 2, 4):
                        sh = s if d == 0 else 8 - s
                        m = (sub >= s) if d == 0 else (sub < 8 - s)
                        b = b + a * jnp.where(m, pltpu.roll(b, sh, 1), 0.0)
                        a = a * jnp.where(m, pltpu.roll(a, sh, 1), 1.0)
                    c = carry[d * NG + g]
                    hs = [None] * nch
                    for j in (range(nch) if d == 0 else range(nch - 1, -1, -1)):
                        hs[j] = b[j] + a[j] * c
                        c = (jnp.broadcast_to(a[j, edge:edge + 1, :], (8, LANES)) * c
                             + jnp.broadcast_to(b[j, edge:edge + 1, :], (8, LANES)))
                    b_scr[d, g, pl.ds(r0, R), :] = jnp.concatenate(hs, axis=0)
                    out.append(c)
            return tuple(out)

        init = tuple(jnp.broadcast_to(h0[d][g], (8, LANES)) for d in range(2) for g in range(NG))
        ends = lax.fori_loop(0, steps, scan_body, init)
        return [[ends[d * NG + g][0:1, :] for g in range(NG)] for d in range(2)]

    def rg_out(g_ref, out_ref, ns):
        def body(ti, carry):
            t0 = pl.multiple_of(ti * T, T)
            for g in range(NG):
                hs = b_scr[0, g, pl.ds(t0, T), :] + b_scr[1, g, pl.ds(t0, T), :]
                gate = g_ref[0, pl.ds(t0, T), g * LANES:(g + 1) * LANES].astype(F32)
                out_ref[0, pl.ds(t0, T), g * LANES:(g + 1) * LANES] = (_gelu_tanh(gate) * hs).astype(out_ref.dtype)
            return carry

        lax.fori_loop(0, ns // T, body, 0)

    zero = [[jnp.zeros((1, LANES), F32)] * NG] * 2
    h_ctx = rglru(crx_ref, nc, zero)
    if need_ctx_out:
        rg_out(crg_ref, ycr_ref, nc)
        conv_mixer(cab_ref, cac_ref, cax_ref, yca_ref, nc)
    rglru(rx_ref, n, h_ctx)
    rg_out(rg_ref, yr_ref, n)
    conv_mixer(ab_ref, ac_ref, ax_ref, ya_ref, n)


def _seq_call(z, zc, zc_off, p, need_ctx_out):
    bsz, n, _ = z.shape
    nc = zc.shape[1]
    u = HALF // LANES

    def zspec(rows, off):
        return pl.BlockSpec((1, rows, HALF), lambda b, c: (b, 0, off // u + c))

    names = ("a_b", "a_c", "a_x", "r_g", "r_x")
    in_specs = [zspec(n, Z_FULL[k]) for k in names]
    args = [z] * 5
    if need_ctx_out:
        in_specs += [zspec(nc, zc_off[k]) for k in names[:4]]
        args += [zc] * 4
    in_specs.append(zspec(nc, zc_off["r_x"]))
    args.append(zc)
    in_specs += [
        pl.BlockSpec((3, HALF), lambda b, c: (0, c)),
        pl.BlockSpec((2, 4, HALF), lambda b, c: (0, 0, c)),
        pl.BlockSpec((2, 1, HALF), lambda b, c: (0, 0, c)),
        pl.BlockSpec((2, 1, HALF, 2 * HALF), lambda b, c: (0, c, 0, 0)),
        pl.BlockSpec((2, 1, HALF), lambda b, c: (0, 0, c)),
        pl.BlockSpec((2, 1, HALF), lambda b, c: (0, 0, c)),
        pl.BlockSpec((2, 1, HALF), lambda b, c: (0, 0, c)),
    ]
    args += [p["conv_a_w"], p["rg_conv_w"], p["rg_conv_b"], p["rg_wg"], p["rg_ba"], p["rg_bx"], p["rg_lambda"]]
    out_specs = [pl.BlockSpec((1, n, HALF), lambda b, c: (b, 0, c))] * 2
    out_shape = [jax.ShapeDtypeStruct((bsz, n, CONV_W), BF16), jax.ShapeDtypeStruct((bsz, n, RG_W), BF16)]
    if need_ctx_out:
        out_specs += [pl.BlockSpec((1, nc, HALF), lambda b, c: (b, 0, c))] * 2
        out_shape += [jax.ShapeDtypeStruct((bsz, nc, CONV_W), BF16), jax.ShapeDtypeStruct((bsz, nc, RG_W), BF16)]
    npad = max(n, nc) + 2 * PAD
    return pl.pallas_call(
        functools.partial(_seq_kernel, n=n, nc=nc, need_ctx_out=need_ctx_out),
        grid=(bsz, CONV_W // HALF),
        in_specs=in_specs,
        out_specs=out_specs,
        out_shape=out_shape,
        scratch_shapes=[pltpu.VMEM((npad, HALF), F32), pltpu.VMEM((2, u, max(n, nc), LANES), F32),
                        pltpu.VMEM((2, u, max(n, nc), LANES), F32)],
        compiler_params=_cparams(("parallel", "parallel")),
        name="seq_mixers",
    )(*args)


def _attn_kernel(*refs, has_lat, lam_init):
    if has_lat:
        q_ref, k_ref, v_ref, ck_ref, cv_ref, lam_ref, sw_ref, o_ref, k_all, v_ext = refs
    else:
        q_ref, ck_ref, cv_ref, lam_ref, sw_ref, o_ref, k_all, v_ext = refs
    nc = ck_ref.shape[1]
    nk = k_all.shape[0]

    @pl.when(pl.program_id(1) == 0)
    def _():
        k_all[0:nc, :] = ck_ref[0]
        if has_lat:
            k_all[nc:nk, :] = k_ref[0]
        for h in range(DA_HEADS):
            sl = slice(h * LANES, (h + 1) * LANES)
            v_ext[h, 0:nc, 0:LANES] = cv_ref[0, :, sl]
            if has_lat:
                v_ext[h, nc:nk, 0:LANES] = v_ref[0, :, sl]
            v_ext[h, :, LANES:2 * LANES] = jnp.ones((nk, LANES), v_ext.dtype)

    lane = lax.broadcasted_iota(jnp.int32, (1, LANES), 1)
    map1 = (lane % DA_HEAD_DIM) < (DA_HEAD_DIM // 2)
    dn = (((1,), (1,)), ((), ()))
    lam = lam_ref[0:1, 0:1]

    def scores(h):
        sl = slice(h * LANES, (h + 1) * LANES)
        q = q_ref[0, :, sl]
        zero = jnp.zeros_like(q)
        return [lax.dot_general(qm, k_all[:, sl], dn, preferred_element_type=F32)
                for qm in (jnp.where(map1, q, zero), jnp.where(map1, zero, q))]

    def finish(h, ss):
        es = [jnp.exp2(s - jnp.max(s, axis=-1, keepdims=True)).astype(BF16) for s in ss]
        rs = [jnp.dot(e, v_ext[h], preferred_element_type=F32) for e in es]
        o1, o2 = [r[:, :LANES] / r[:, LANES:LANES + 1] for r in rs]
        y = _rms(o1 - lam * o2) * sw_ref[...] * (1.0 - lam_init)
        o_ref[0, :, h * LANES:(h + 1) * LANES] = y.astype(o_ref.dtype)

    pending = scores(0)
    for h in range(DA_HEADS):
        ss = pending
        if h + 1 < DA_HEADS:
            pending = scores(h + 1)
        finish(h, ss)


def _attn_call(zq, z, zc, zc_off, lam, subw, lam_init, has_lat):
    bsz, nq, _ = zq.shape
    nc = zc.shape[1]
    tq = min(ATTN_Q_ROWS, nq)
    w = DA_HEADS * LANES
    u = DA_HEADS
    in_specs = [pl.BlockSpec((1, tq, w), lambda b, i: (b, i, Z_FULL["q"] // u))]
    args = [zq]
    nk = nc
    if has_lat:
        n = z.shape[1]
        nk = nc + n
        in_specs += [pl.BlockSpec((1, n, w), lambda b, i: (b, 0, Z_FULL["k"] // u)),
                     pl.BlockSpec((1, n, w), lambda b, i: (b, 0, Z_FULL["v"] // u))]
        args += [z, z]
    in_specs += [pl.BlockSpec((1, nc, w), lambda b, i: (b, 0, zc_off["k"] // u)),
                 pl.BlockSpec((1, nc, w), lambda b, i: (b, 0, zc_off["v"] // u)),
                 pl.BlockSpec((8, LANES), lambda b, i: (0, 0)),
                 pl.BlockSpec((1, LANES), lambda b, i: (0, 0))]
    args += [zc, zc, lam, subw]
    return pl.pallas_call(
        functools.partial(_attn_kernel, has_lat=has_lat, lam_init=lam_init),
        grid=(bsz, nq // tq),
        in_specs=in_specs,
        out_specs=pl.BlockSpec((1, tq, w), lambda b, i: (b, i, 0)),
        out_shape=jax.ShapeDtypeStruct((bsz, nq, DA_HEADS * DA_V_DIM), BF16),
        scratch_shapes=[pltpu.VMEM((nk, w), BF16), pltpu.VMEM((DA_HEADS, nk, 2 * LANES), BF16)],
        compiler_params=_cparams(("parallel", "arbitrary")),
        name="diff_attn",
    )(*args)


def _merge_kernel(*refs, route):
    if route:
        (ya_ref, yr_ref, yd_ref, ga_ref, gr_ref, gd_ref, x_ref, g1_ref, nw_ref, sh_ref, sc_ref,
         wb_ref, wo_ref, rw_ref, xo_ref, h_ref, rt_ref) = refs
    else:
        (ya_ref, yr_ref, yd_ref, ga_ref, gr_ref, gd_ref, x_ref, g1_ref, nw_ref, sh_ref, sc_ref,
         wb_ref, wo_ref, xo_ref, h_ref) = refs
    m = None
    for i, (y_ref, g_ref) in enumerate(((ya_ref, ga_ref), (yr_ref, gr_ref), (yd_ref, gd_ref))):
        t = _sigmoid(g_ref[0].astype(F32)) * jnp.dot(y_ref[0], wb_ref[i], preferred_element_type=F32)
        m = t if m is None else m + t
    out = jnp.dot(m.astype(BF16), wo_ref[...], preferred_element_type=F32)
    xn = x_ref[0] + g1_ref[0] * out
    xo_ref[0] = xn
    h = (_rms(xn) * nw_ref[...]) * (1.0 + sc_ref[0]) + sh_ref[0]
    hb = h.astype(BF16)
    h_ref[0] = hb
    if route:
        logits = jnp.dot(hb, rw_ref[...], preferred_element_type=F32)
        lane = lax.broadcasted_iota(jnp.int32, logits.shape, 1)
        neg = jnp.float32(-jnp.inf)
        l1 = jnp.where(lane < N_EXPERTS, logits, neg)
        m1 = jnp.max(l1, axis=-1, keepdims=True)
        i1 = jnp.min(jnp.where(l1 == m1, lane, LANES), axis=-1, keepdims=True)
        l2 = jnp.where(lane == i1, neg, l1)
        m2 = jnp.max(l2, axis=-1, keepdims=True)
        i2 = jnp.min(jnp.where(l2 == m2, lane, LANES), axis=-1, keepdims=True)
        ex = jnp.exp(m2 - m1)
        gt1 = 1.0 / (1.0 + ex)
        gt2 = ex * gt1
        rt = jnp.where(lane == 0, i1.astype(F32),
                       jnp.where(lane == 1, i2.astype(F32),
                                 jnp.where(lane == 2, gt1, jnp.where(lane == 3, gt2, 0.0))))
        rt_ref[0] = rt


def _merge_call(ya, yr, yd, z, x, mod3, mod_row, nw2, wb, wo, router_w=None):
    bsz, n, d = x.shape
    tm = min(512, n)
    route = router_w is not None
    tok = lambda b, i: (b, i, 0)
    gu = d // LANES
    in_specs = [
        pl.BlockSpec((1, tm, CONV_W), tok), pl.BlockSpec((1, tm, RG_W), tok),
        pl.BlockSpec((1, tm, DA_HEADS * DA_V_DIM), tok),
        pl.BlockSpec((1, tm, d), lambda b, i: (b, i, Z_FULL["g_a"] // gu)),
        pl.BlockSpec((1, tm, d), lambda b, i: (b, i, Z_FULL["g_r"] // gu)),
        pl.BlockSpec((1, tm, d), lambda b, i: (b, i, Z_FULL["g_d"] // gu)),
        pl.BlockSpec((1, tm, d), tok),
        pl.BlockSpec((1, 1, d), lambda b, i: (mod_row(b), 0, 2)),
        pl.BlockSpec((1, d), lambda b, i: (0, 0)),
        pl.BlockSpec((1, 1, d), lambda b, i: (mod_row(b), 0, 3)),
        pl.BlockSpec((1, 1, d), lambda b, i: (mod_row(b), 0, 4)),
        pl.BlockSpec(wb.shape, lambda b, i: (0, 0, 0)),
        pl.BlockSpec(wo.shape, lambda b, i: (0, 0)),
    ]
    args = [ya, yr, yd, z, z, z, x, mod3, nw2, mod3, mod3, wb, wo]
    out_specs = [pl.BlockSpec((1, tm, d), tok), pl.BlockSpec((1, tm, d), tok)]
    out_shape = [jax.ShapeDtypeStruct((bsz, n, d), F32),
                 jax.ShapeDtypeStruct((bsz, n, d), BF16)]
    if route:
        in_specs.append(pl.BlockSpec(router_w.shape, lambda b, i: (0, 0)))
        args.append(router_w)
        out_specs.append(pl.BlockSpec((1, tm, LANES), tok))
        out_shape.append(jax.ShapeDtypeStruct((bsz, n, LANES), F32))
    return pl.pallas_call(
        functools.partial(_merge_kernel, route=route),
        grid=(bsz, n // tm),
        in_specs=in_specs,
        out_specs=out_specs,
        out_shape=out_shape,
        compiler_params=_cparams(("parallel", "parallel")),
        name="merge",
    )(*args)


def _ffn_kernel(h_ref, x_ref, g2_ref, wg_ref, wu_ref, w2_ref, o_ref, acc_ref, *, nf):
    f = pl.program_id(2)
    h = h_ref[0]
    g = jnp.dot(h, wg_ref[...], preferred_element_type=F32)
    u = jnp.dot(h, wu_ref[...], preferred_element_type=F32)
    act = (_silu(g) * u).astype(BF16)

    def down():
        return jnp.dot(act, w2_ref[...], preferred_element_type=F32)

    if nf == 1:
        o_ref[0] = x_ref[0] + g2_ref[0] * down()
        return

    @pl.when(f == 0)
    def _():
        acc_ref[...] = down()

    if nf > 2:
        @pl.when((f > 0) & (f < nf - 1))
        def _():
            acc_ref[...] += down()

    @pl.when(f == nf - 1)
    def _():
        o_ref[0] = x_ref[0] + g2_ref[0] * (acc_ref[...] + down())


def _ffn_call(h, x, mod3, mod_row, w13, w2):
    bsz, n, d = x.shape
    ff = w2.shape[0]
    tm = min(SWIGLU_ROWS, n)
    tf = SWIGLU_FF_TILE
    nf = ff // tf
    tok = lambda b, i, f: (b, i, 0)
    return pl.pallas_call(
        functools.partial(_ffn_kernel, nf=nf),
        grid=(bsz, n // tm, nf),
        in_specs=[
            pl.BlockSpec((1, tm, d), tok),
            pl.BlockSpec((1, tm, d), tok),
            pl.BlockSpec((1, 1, d), lambda b, i, f: (mod_row(b), 0, 5)),
            pl.BlockSpec((d, tf), lambda b, i, f: (0, f)),
            pl.BlockSpec((d, tf), lambda b, i, f: (0, nf + f)),
            pl.BlockSpec((tf, d), lambda b, i, f: (f, 0)),
        ],
        out_specs=pl.BlockSpec((1, tm, d), tok),
        out_shape=jax.ShapeDtypeStruct((bsz, n, d), F32),
        scratch_shapes=[pltpu.VMEM((tm, d), F32)],
        compiler_params=_cparams(("parallel", "parallel", "arbitrary")),
        name="ffn",
    )(h, x, mod3, w13, w13, w2)


def _moe_kernel(be_ref, nu_ref, x_ref, wg_ref, wu_ref, w2_ref, o_ref, xb):
    i = pl.program_id(0)
    f = pl.program_id(1)
    used = i < nu_ref[0]

    def swiglu(x):
        g = jnp.dot(x, wg_ref[0], preferred_element_type=F32)
        u = jnp.dot(x, wu_ref[0], preferred_element_type=F32)
        act = (_silu(g) * u).astype(BF16)
        return jnp.dot(act, w2_ref[0], preferred_element_type=F32)

    @pl.when(used & (f == 0))
    def _():
        x = x_ref[...].astype(BF16)
        xb[...] = x
        o_ref[...] = swiglu(x)

    @pl.when(used & (f > 0))
    def _():
        o_ref[...] += swiglu(xb[...])

    @pl.when(jnp.logical_not(used) & (f == 0))
    def _():
        o_ref[...] = jnp.zeros(o_ref.shape, o_ref.dtype)


def _moe_call(xs, block_e, n_used, w13, w2, rows):
    n_rows, d = xs.shape
    n_blocks = n_rows // rows
    ff = w2.shape[1]
    tf = SWIGLU_FF_TILE
    nf = ff // tf

    def bi(i, nu):
        return jnp.minimum(i, nu[0] - 1)

    def fi(i, f, nu):
        return jnp.where(i < nu[0], f, nf - 1)

    grid_spec = pltpu.PrefetchScalarGridSpec(
        num_scalar_prefetch=2,
        grid=(n_blocks, nf),
        in_specs=[
            pl.BlockSpec((rows, d), lambda i, f, be, nu: (bi(i, nu), 0)),
            pl.BlockSpec((1, d, tf), lambda i, f, be, nu: (be[i], 0, fi(i, f, nu))),
            pl.BlockSpec((1, d, tf), lambda i, f, be, nu: (be[i], 0, nf + fi(i, f, nu))),
            pl.BlockSpec((1, tf, d), lambda i, f, be, nu: (be[i], fi(i, f, nu), 0)),
        ],
        out_specs=pl.BlockSpec((rows, d), lambda i, f, be, nu: (i, 0)),
        scratch_shapes=[pltpu.VMEM((rows, d), BF16)],
    )
    return pl.pallas_call(
        _moe_kernel,
        grid_spec=grid_spec,
        out_shape=jax.ShapeDtypeStruct((n_rows, d), F32),
        compiler_params=_cparams(("arbitrary", "arbitrary")),
        name="moe",
    )(block_e, n_used, xs, w13, w13, w2)


GROUP_PAD = 8
GROUP_SIZES = tuple(SWIGLU_ROWS >> s for s in range(SWIGLU_ROWS.bit_length()) if SWIGLU_ROWS >> s >= GROUP_PAD)


def _sorted_rows(tm):
    return 2 * tm + N_EXPERTS * GROUP_PAD


def _group_copies(step, src_of, dst_of, loc_ref, base_ref, pc_ref, sem, sizes):
    for e in range(N_EXPERTS):
        loc = loc_ref[step * N_EXPERTS + e]
        base = base_ref[step * N_EXPERTS + e]
        pc = pc_ref[step * N_EXPERTS + e]
        for size in sizes:
            before = pc & (-2 * size)
            src = src_of(pl.multiple_of(loc + before, GROUP_PAD), pl.multiple_of(base + before, GROUP_PAD), size)
            dst = dst_of(pl.multiple_of(loc + before, GROUP_PAD), pl.multiple_of(base + before, GROUP_PAD), size)
            yield (pc & size) != 0, pltpu.make_async_copy(src, dst, sem)


def _dispatch2_kernel(loc_ref, base_ref, pc_ref, tail_ref, nu_ref, pos_ref, h_ref, xs_hbm, srt, zeros, sem, zsem,
                      *, tm, nt, rows, sizes):
    i = pl.program_id(0)
    n = pl.num_programs(0)
    slot = i % 2
    sr = srt.shape[1]
    n_blocks = xs_hbm.shape[0] // rows
    min_used = 2 * tm * nt // rows

    def tail_copies():
        for e in range(N_EXPERTS):
            start = tail_ref[e]
            length = tail_ref[N_EXPERTS + e]
            for size in sizes[1:]:
                before = length & (-2 * size)
                dst = xs_hbm.at[pl.ds(pl.multiple_of(start + before, GROUP_PAD), size)]
                yield (length & size) != 0, pltpu.make_async_copy(zeros.at[pl.ds(0, size)], dst, zsem)
        for b in range(min_used, n_blocks):
            dst = xs_hbm.at[pl.ds(b * rows, rows)]
            yield b >= nu_ref[0], pltpu.make_async_copy(zeros, dst, zsem)

    @pl.when(i == 0)
    def _():
        zeros[...] = jnp.zeros(zeros.shape, zeros.dtype)
        for pred, cp in tail_copies():
            pl.when(pred)(cp.start)

    r = lax.broadcasted_iota(jnp.int32, (sr, tm), 0)
    hit = (r == pos_ref[0, 0:1, :]) | (r == pos_ref[0, 1:2, :])
    sel = jnp.where(hit, 1.0, 0.0).astype(BF16)
    srt[slot] = jnp.dot(sel, h_ref[...], preferred_element_type=F32)

    def group_copies(step, s):
        return _group_copies(step, lambda lo, ba, sz: srt.at[s, pl.ds(lo, sz)],
                             lambda lo, ba, sz: xs_hbm.at[pl.ds(ba, sz)], loc_ref, base_ref, pc_ref, sem.at[s], sizes)

    for pred, cp in group_copies(i, slot):
        pl.when(pred)(cp.start)

    @pl.when(i > 0)
    def _():
        for pred, cp in group_copies(i - 1, 1 - slot):
            pl.when(pred)(cp.wait)

    @pl.when(i == n - 1)
    def _():
        for pred, cp in group_copies(i, slot):
            pl.when(pred)(cp.wait)
        for pred, cp in tail_copies():
            pl.when(pred)(cp.wait)


def _dispatch2_call(h2, plan, tm, rows):
    t, d = h2.shape
    nt = t // tm
    sr = _sorted_rows(tm)
    sizes = tuple(s for s in GROUP_SIZES if s <= tm)
    grid_spec = pltpu.PrefetchScalarGridSpec(
        num_scalar_prefetch=5,
        grid=(nt,),
        in_specs=[pl.BlockSpec((1, 2, tm), lambda i, *_: (i, 0, 0)),
                  pl.BlockSpec((tm, d), lambda i, *_: (i, 0))],
        out_specs=pl.BlockSpec(memory_space=pl.ANY),
        scratch_shapes=[pltpu.VMEM((2, sr, d), F32), pltpu.VMEM((rows, d), F32),
                        pltpu.SemaphoreType.DMA((2,)), pltpu.SemaphoreType.DMA(())],
    )
    return pl.pallas_call(
        functools.partial(_dispatch2_kernel, tm=tm, nt=nt, rows=rows, sizes=sizes),
        grid_spec=grid_spec,
        out_shape=jax.ShapeDtypeStruct((plan["n_blocks"] * rows, d), F32),
        compiler_params=pltpu.CompilerParams(dimension_semantics=("arbitrary",), vmem_limit_bytes=VMEM_LIMIT),
        name="moe_dispatch",
    )(plan["loc"], plan["base"], plan["pc"], plan["tail"], plan["n_used"], plan["pos_rows"], h2)


def _combine2_kernel(loc_ref, base_ref, pc_ref, x_ref, pos_ref, rt_ref, g2_ref, fw_ref, ys_hbm, o_ref, buf, sem,
                     *, tm, sizes):
    i = pl.program_id(0)
    n = pl.num_programs(0)
    slot = i % 2
    sr = buf.shape[1]

    def group_copies(step, s):
        return _group_copies(step, lambda lo, ba, sz: ys_hbm.at[pl.ds(ba, sz)],
                             lambda lo, ba, sz: buf.at[s, pl.ds(lo, sz)], loc_ref, base_ref, pc_ref, sem.at[s], sizes)

    @pl.when(i == 0)
    def _():
        buf[...] = jnp.zeros(buf.shape, buf.dtype)
        for pred, cp in group_copies(0, 0):
            pl.when(pred)(cp.start)

    @pl.when(i + 1 < n)
    def _():
        for pred, cp in group_copies(i + 1, 1 - slot):
            pl.when(pred)(cp.start)

    for pred, cp in group_copies(i, slot):
        pl.when(pred)(cp.wait)

    rt = rt_ref[...]
    col = lax.broadcasted_iota(jnp.int32, (tm, sr), 1)
    pos = pos_ref[...]
    w = jnp.where(col == pos[:, 0:1], rt[:, 2:3], jnp.where(col == pos[:, 1:2], rt[:, 3:4], 0.0))
    y = jnp.dot(w.astype(BF16), buf[slot].astype(BF16), preferred_element_type=F32)
    xn = x_ref[...] + g2_ref[0] * y
    o_ref[...] = _rms(xn) * fw_ref[...]


def _combine2_call(x2, ys, plan, route2, mod3, n_per_seq, fw, tm):
    t, d = x2.shape
    nt = t // tm
    per = n_per_seq // tm
    sr = _sorted_rows(tm)
    sizes = tuple(s for s in GROUP_SIZES if s <= tm)
    grid_spec = pltpu.PrefetchScalarGridSpec(
        num_scalar_prefetch=3,
        grid=(nt,),
        in_specs=[
            pl.BlockSpec((tm, d), lambda i, *_: (i, 0)),
            pl.BlockSpec((tm, 2), lambda i, *_: (i, 0)),
            pl.BlockSpec((tm, LANES), lambda i, *_: (i, 0)),
            pl.BlockSpec((1, 1, d), lambda i, *_: (i // per, 0, 5)),
            pl.BlockSpec((1, d), lambda i, *_: (0, 0)),
            pl.BlockSpec(memory_space=pl.ANY),
        ],
        out_specs=pl.BlockSpec((tm, d), lambda i, *_: (i, 0)),
        scratch_shapes=[pltpu.VMEM((2, sr, d), F32), pltpu.SemaphoreType.DMA((2,))],
    )
    return pl.pallas_call(
        functools.partial(_combine2_kernel, tm=tm, sizes=sizes),
        grid_spec=grid_spec,
        out_shape=jax.ShapeDtypeStruct((t, d), F32),
        compiler_params=pltpu.CompilerParams(dimension_semantics=("arbitrary",), vmem_limit_bytes=VMEM_LIMIT),
        name="combine_final",
    )(plan["loc"], plan["base"], plan["pc"], x2, plan["pos_cols"], route2, mod3, fw, ys)


def _sort_plan(route, tm, rows):
    t = route.shape[0]
    nt = t // tm
    experts = jnp.arange(N_EXPERTS, dtype=jnp.int32)
    e = route[:, :2].astype(jnp.int32).reshape(nt, 2 * tm)
    hot = (e[:, :, None] == experts[None, None, :]).astype(jnp.int32)
    cs = jnp.cumsum(hot, axis=1)
    rank = jnp.sum(cs * hot, axis=2) - 1
    cnt = cs[:, -1, :]
    pc = (cnt + GROUP_PAD - 1) // GROUP_PAD * GROUP_PAD
    loc = jnp.cumsum(pc, axis=1) - pc
    pos = jnp.sum(hot * loc[:, None, :], axis=2) + rank
    tot = jnp.sum(pc, axis=0)
    padded = (tot + rows - 1) // rows * rows
    e_end = jnp.cumsum(padded)
    e_start = e_end - padded
    base = e_start[None, :] + jnp.cumsum(pc, axis=0) - pc
    n_blocks = -(-(2 * t + nt * N_EXPERTS * (GROUP_PAD - 1)) // rows) + N_EXPERTS
    block_start = jnp.arange(n_blocks, dtype=jnp.int32) * rows
    block_e = jnp.minimum(jnp.sum((block_start[:, None] >= e_end[None, :]).astype(jnp.int32), axis=1),
                          N_EXPERTS - 1)
    pos3 = pos.reshape(nt, tm, 2)
    i32 = lambda a: a.astype(jnp.int32)
    return dict(
        loc=i32(loc.reshape(-1)), base=i32(base.reshape(-1)), pc=i32(pc.reshape(-1)),
        tail=i32(jnp.concatenate([e_start + tot, padded - tot])),
        n_used=i32(e_end[-1] // rows).reshape(1), block_e=i32(block_e), n_blocks=n_blocks,
        pos_rows=i32(jnp.swapaxes(pos3, 1, 2)), pos_cols=i32(pos3.reshape(t, 2)))


def _norm_kernel(x_ref, fw_ref, o_ref):
    o_ref[0] = _rms(x_ref[0]) * fw_ref[...]


def _norm_call(x, fw):
    bsz, n, d = x.shape
    tm = min(512, n)
    tok = lambda b, i: (b, i, 0)
    return pl.pallas_call(
        _norm_kernel,
        grid=(bsz, n // tm),
        in_specs=[pl.BlockSpec((1, tm, d), tok), pl.BlockSpec((1, d), lambda b, i: (0, 0))],
        out_specs=pl.BlockSpec((1, tm, d), tok),
        out_shape=jax.ShapeDtypeStruct((bsz, n, d), F32),
        compiler_params=_cparams(("parallel", "parallel")),
        name="final_norm",
    )(x, fw)


def _rope_perm():
    perm = np.zeros((LANES,), np.int32)
    half = ROPE_AXIS_DIM // 2
    for i in range(LANES // 2):
        comp, axis, j = i // 32, (i % 32) // half, i % half
        perm[i] = comp * DA_HEAD_DIM + axis * ROPE_AXIS_DIM + j
        perm[LANES // 2 + i] = perm[i] + half
    return perm


def _rope_tables(n):
    rows = n // GRID_W
    pos_r = jnp.repeat(jnp.arange(rows, dtype=F32), GRID_W)
    pos_c = jnp.broadcast_to(jnp.arange(GRID_W, dtype=F32), (rows, GRID_W)).reshape(-1)
    inv_freq = ROPE_BASE ** (-jnp.arange(0, ROPE_AXIS_DIM, 2, dtype=F32) / ROPE_AXIS_DIM)
    ang = jnp.stack([pos_r[:, None] * inv_freq, pos_c[:, None] * inv_freq], axis=1)
    cos, sin = jnp.cos(ang), jnp.sin(ang)
    half = jnp.concatenate([cos.reshape(n, 32), cos.reshape(n, 32)], axis=1)
    cos_t = jnp.concatenate([half, half], axis=1)
    sh = jnp.concatenate([sin.reshape(n, 32), sin.reshape(n, 32)], axis=1)
    sin_t = jnp.concatenate([-sh, sh], axis=1)
    return cos_t, sin_t


def _blockdiag_gate(wa, wx):
    per = HALF // (RG_W // RG_BLOCKS)
    eye = jnp.eye(per, dtype=wa.dtype)

    def bd(w):
        w4 = w.reshape(2, RG_BLOCKS // per, per, w.shape[-2], w.shape[-1])
        return jnp.einsum("dhbij,bc->dhbicj", w4, eye).reshape(2, RG_BLOCKS // per, HALF, HALF)

    return jnp.concatenate([bd(wa), bd(wx)], axis=-1).astype(BF16)


def kernel(x, c, ctx, c_ctx, mod_w, mod_b, norm1_w, norm2_w, w_in, conv_a_w, rg_conv_w, rg_conv_b, rg_wa, rg_ba, rg_wx, rg_bx, rg_lambda, da_lambda, da_subln_w, w_branch, w_out, ffn_w13, ffn_w2, router_w, moe_w13, moe_w2, final_norm_w):
    bsz, n, d = x.shape
    nc = ctx.shape[1]
    depth = mod_w.shape[0]
    cos_t, sin_t = _rope_tables(n)
    perm = _rope_perm()
    ref_cols = {name: np.arange(start, start + width, dtype=np.int32) for name, start, width in REF_COLS}
    for name in ("q", "k"):
        for h in range(DA_HEADS):
            base = ref_cols[name][0] + h * LANES
            ref_cols[name][h * LANES:(h + 1) * LANES] = base + perm
    z_cols = np.concatenate([ref_cols[name] for name in Z_ORDER])

    bp = -(-(bsz + 1) // 8) * 8
    cc = jnp.zeros((bp, d), F32).at[:bsz].set(c).at[bsz].set(c_ctx)
    lat_row = lambda b: b
    ctx_row = lambda b: bsz
    cx = ctx
    moe_rows = SWIGLU_ROWS if bsz * n >= 8192 else 128

    for l in range(depth):
        last = l == depth - 1
        lam_init = 0.8 - 0.6 * math.exp(-0.3 * l)
        j = l // 2
        mod, lam = _mod_call(cc, mod_w[l], mod_b[l][None, :], da_lambda[l], lam_init)
        mod3 = mod.reshape(bp, 1, N_MOD * d)
        w_in_l = w_in[l][:, z_cols].astype(BF16)
        nw1 = norm1_w[l][None, :]
        nw2 = norm2_w[l][None, :]
        p = dict(
            conv_a_w=conv_a_w[l], rg_conv_w=rg_conv_w[l], rg_conv_b=rg_conv_b[l][:, None, :],
            rg_wg=_blockdiag_gate(rg_wa[l], rg_wx[l]),
            rg_ba=rg_ba[l].reshape(2, 1, RG_W), rg_bx=rg_bx[l].reshape(2, 1, RG_W),
            rg_lambda=rg_lambda[l][:, None, :])
        subw = da_subln_w[l][None, :]
        wb = w_branch[l].astype(BF16)
        wo = w_out[l].astype(BF16)

        ncol = w_in.shape[2] // COL
        z = _in_call(x, nw1, mod3, lat_row, w_in_l, range(ncol), True, cos_t, sin_t)
        flat = lambda a: a.reshape((1, bsz * nc) + a.shape[2:])
        unflat = lambda a: a.reshape((bsz, nc) + a.shape[2:])
        if last:
            zc = unflat(_in_call(flat(cx), nw1, mod3, ctx_row, w_in_l, CTX_LAST_BLOCKS, False))
            zc_off = Z_CTX_LAST
        else:
            zc = unflat(_in_call(flat(cx), nw1, mod3, ctx_row, w_in_l, range(ncol), False))
            zc_off = Z_FULL

        seq_out = _seq_call(z, zc, zc_off, p, not last)
        yd = _attn_call(z, z, zc, zc_off, lam, subw, lam_init, True)
        moe_layer = l % 2 == 1
        if moe_layer:
            rw = jnp.zeros((d, LANES), BF16).at[:, :N_EXPERTS].set(router_w[j].astype(BF16))
            xn, h, route = _merge_call(seq_out[0], seq_out[1], yd, z, x, mod3, lat_row, nw2, wb, wo, rw)
            t = bsz * n
            tm = min(512, n)
            route2 = route.reshape(t, LANES)
            plan = _sort_plan(route2, tm, moe_rows)
            xs = _dispatch2_call(h.reshape(t, d), plan, tm, moe_rows)
            ys = _moe_call(xs, plan["block_e"], plan["n_used"], moe_w13[j].astype(BF16),
                           moe_w2[j].astype(BF16), moe_rows)
            if last:
                out = _combine2_call(xn.reshape(t, d), ys, plan, route2, mod3, n, final_norm_w[None, :], tm)
                return out.reshape(bsz, n, d)
            raise NotImplementedError("expert layer followed by another layer")
        xn, h = _merge_call(seq_out[0], seq_out[1], yd, z, x, mod3, lat_row, nw2, wb, wo)
        w13 = ffn_w13[j].astype(BF16)
        w2 = ffn_w2[j].astype(BF16)
        x = _ffn_call(h, xn, mod3, lat_row, w13, w2)
        if not last:
            ycd = _attn_call(zc, None, zc, zc_off, lam, subw, lam_init, False)
            cxn, hc = _merge_call(flat(seq_out[2]), flat(seq_out[3]), flat(ycd), flat(zc), flat(cx), mod3,
                                  ctx_row, nw2, wb, wo)
            cx = unflat(_ffn_call(hc, cxn, mod3, ctx_row, w13, w2))
    return _norm_call(x, final_norm_w[None, :])
```

```python
import functools
import math

import jax
import jax.numpy as jnp
import numpy as np
from jax import lax
from jax.experimental import pallas as pl
from jax.experimental.pallas import tpu as pltpu

F32 = jnp.float32
BF16 = jnp.bfloat16

EPS = 1e-6
N_MOD = 6
GRID_W = 64
CONV_W = 512
RG_W = 512
RG_BLOCKS = 8
RG_C = 8.0
DA_HEADS = 4
DA_HEAD_DIM = 64
DA_V_DIM = 128
ROPE_AXIS_DIM = 32
ROPE_BASE = 10000.0
N_EXPERTS = 8
LANES = 128
HALF = 256
VMEM_LIMIT = 56 * 1024 * 1024
IN_PROJ_ROWS = 2048
IN_PROJ_CHUNK = 256
ATTN_Q_ROWS = 512
SWIGLU_ROWS = 512
SWIGLU_FF_TILE = 1792

COL = 1024
REF_COLS = (("a_b", 0, 512), ("a_c", 512, 512), ("a_x", 1024, 512), ("r_g", 1536, 512), ("r_x", 2048, 512),
            ("q", 2560, 512), ("k", 3072, 512), ("v", 3584, 512), ("g_a", 4096, 1024), ("g_r", 5120, 1024),
            ("g_d", 6144, 1024))
Z_ORDER = ("q", "k", "v", "a_b", "a_c", "a_x", "r_g", "r_x", "g_a", "g_r", "g_d")
Z_FULL = dict(q=0, k=4, v=8, a_b=12, a_c=16, a_x=20, r_g=24, r_x=28, g_a=32, g_r=40, g_d=48)
CTX_LAST_BLOCKS = (0, 1, 3)
Z_CTX_LAST = dict(k=4, v=8, r_x=20)
LOG2E = 1.4426950408889634


def _cparams(sem, vmem=VMEM_LIMIT):
    return pltpu.CompilerParams(dimension_semantics=sem, vmem_limit_bytes=vmem)


def _sigmoid(x):
    return 1.0 / (1.0 + jnp.exp2(x * (-LOG2E)))


def _silu(x):
    return x * _sigmoid(x)


def _gelu_tanh(x):
    return 0.5 * x * (1.0 + jnp.tanh(math.sqrt(2.0 / math.pi) * (x + 0.044715 * (x * x * x))))


def _rms(x):
    return x * lax.rsqrt(jnp.mean(x * x, axis=-1, keepdims=True) + EPS)


def _mod_kernel(cc_ref, w_ref, b_ref, dl_ref, mod_ref, lam_ref, *, lam_init):
    cc = cc_ref[...]
    s = _silu(cc)
    mod_ref[...] = jnp.dot(s, w_ref[...], preferred_element_type=F32,
                           precision=lax.Precision.HIGHEST) + b_ref[...]
    dl = dl_ref[...]
    s1 = jnp.sum(dl[0:1] * dl[1:2], axis=-1, keepdims=True)
    s2 = jnp.sum(dl[2:3] * dl[3:4], axis=-1, keepdims=True)
    lam = jnp.exp(s1) - jnp.exp(s2) + lam_init
    lam_ref[...] = jnp.broadcast_to(lam, lam_ref.shape)


def _mod_call(cc, w, b, dl, lam_init):
    bp, d = cc.shape
    nd = w.shape[1]
    tn = 1024
    return pl.pallas_call(
        functools.partial(_mod_kernel, lam_init=lam_init),
        grid=(nd // tn,),
        in_specs=[
            pl.BlockSpec((bp, d), lambda j: (0, 0)),
            pl.BlockSpec((d, tn), lambda j: (0, j)),
            pl.BlockSpec((1, tn), lambda j: (0, j)),
            pl.BlockSpec(dl.shape, lambda j: (0, 0)),
        ],
        out_specs=[
            pl.BlockSpec((bp, tn), lambda j: (0, j)),
            pl.BlockSpec((8, LANES), lambda j: (0, 0)),
        ],
        out_shape=[jax.ShapeDtypeStruct((bp, nd), F32), jax.ShapeDtypeStruct((8, LANES), F32)],
        compiler_params=_cparams(("arbitrary",)),
        name="mod",
    )(cc, w, b, dl)


def _in_kernel(*refs, rope):
    if rope:
        x_ref, nw_ref, sh_ref, sc_ref, cos_ref, sin_ref, w_ref, o_ref, u_scr = refs
    else:
        x_ref, nw_ref, sh_ref, sc_ref, w_ref, o_ref, u_scr = refs
    j = pl.program_id(2)
    q_scale = DA_HEAD_DIM ** -0.5 * LOG2E
    n_q = DA_HEADS * 2 * DA_HEAD_DIM // LANES
    tm = x_ref.shape[1]
    chunk = min(IN_PROJ_CHUNK, tm)

    @pl.when(j == 0)
    def _():
        for c in range(tm // chunk):
            rows = slice(c * chunk, (c + 1) * chunk)
            y = _rms(x_ref[0, rows, :]) * nw_ref[...]
            u = (y * (1.0 + sc_ref[0]) + sh_ref[0]).astype(BF16)
            u_scr[rows, :] = u
            acc = jnp.dot(u, w_ref[...], preferred_element_type=F32)
            for g in range(acc.shape[1] // LANES):
                r = acc[:, g * LANES:(g + 1) * LANES]
                if rope:
                    r = r * cos_ref[rows, :] + pltpu.roll(r, LANES // 2, 1) * sin_ref[rows, :]
                if g < n_q:
                    r = r * q_scale
                o_ref[0, rows, g * LANES:(g + 1) * LANES] = r.astype(o_ref.dtype)

    @pl.when(j != 0)
    def _():
        o_ref[0] = jnp.dot(u_scr[...], w_ref[...], preferred_element_type=F32).astype(o_ref.dtype)


def _in_call(x, nw, mod3, mod_row, w, col_blocks, rope, cos_t=None, sin_t=None):
    bsz, n, d = x.shape
    tm = min(IN_PROJ_ROWS, n)
    nj = len(col_blocks)
    cb = tuple(col_blocks)
    if cb == tuple(range(nj)):
        wmap = lambda b, i, j: (0, j)
    else:
        assert cb == CTX_LAST_BLOCKS
        wmap = lambda b, i, j: (0, j + j // 2)
    in_specs = [
        pl.BlockSpec((1, tm, d), lambda b, i, j: (b, i, 0)),
        pl.BlockSpec((1, d), lambda b, i, j: (0, 0)),
        pl.BlockSpec((1, 1, d), lambda b, i, j: (mod_row(b), 0, 0)),
        pl.BlockSpec((1, 1, d), lambda b, i, j: (mod_row(b), 0, 1)),
    ]
    args = [x, nw, mod3, mod3]
    if rope:
        in_specs += [pl.BlockSpec((tm, LANES), lambda b, i, j: (i, 0)),
                     pl.BlockSpec((tm, LANES), lambda b, i, j: (i, 0))]
        args += [cos_t, sin_t]
    in_specs.append(pl.BlockSpec((d, COL), wmap))
    args.append(w)
    return pl.pallas_call(
        functools.partial(_in_kernel, rope=rope),
        grid=(bsz, n // tm, nj),
        in_specs=in_specs,
        out_specs=pl.BlockSpec((1, tm, COL), lambda b, i, j: (b, i, j)),
        out_shape=jax.ShapeDtypeStruct((bsz, n, nj * COL), BF16),
        scratch_shapes=[pltpu.VMEM((tm, d), BF16)],
        compiler_params=_cparams(("parallel", "parallel", "arbitrary")),
        name="in_proj",
    )(*args)


SEQ_TILE = 64
PAD = 8
SCAN_ROWS = 64


def _seq_kernel(*refs, n, nc, need_ctx_out):
    it = iter(refs)
    ab_ref, ac_ref, ax_ref, rg_ref, rx_ref = (next(it) for _ in range(5))
    if need_ctx_out:
        cab_ref, cac_ref, cax_ref, crg_ref = (next(it) for _ in range(4))
    crx_ref = next(it)
    wa_ref, cw_ref, cbias_ref, wg_ref, ba_ref, bx_ref, lam_ref = (next(it) for _ in range(7))
    ya_ref, yr_ref = next(it), next(it)
    if need_ctx_out:
        yca_ref, ycr_ref = next(it), next(it)
    xp, a_scr, b_scr = next(it), next(it), next(it)
    T = SEQ_TILE
    W = HALF

    def stage(val, ns):
        xp[0:PAD, :] = jnp.zeros((PAD, W), F32)
        xp[PAD:PAD + ns, :] = val
        xp[PAD + ns:2 * PAD + ns, :] = jnp.zeros((PAD, W), F32)

    def conv_mixer(b_ref, c_ref, x_ref, out_ref, ns):
        stage(c_ref[0].astype(F32) * x_ref[0].astype(F32), ns)
        w0, w1, w2 = wa_ref[0:1, :], wa_ref[1:2, :], wa_ref[2:3, :]

        def body(ti, carry):
            t0 = pl.multiple_of(ti * T, T)
            ps = xp[pl.ds(t0, T + 2 * PAD), :]
            conv = w0 * ps[PAD - 1:PAD - 1 + T] + w1 * ps[PAD:PAD + T] + w2 * ps[PAD + 1:PAD + 1 + T]
            gate = b_ref[0, pl.ds(t0, T), :].astype(F32)
            out_ref[0, pl.ds(t0, T), :] = (gate * conv).astype(out_ref.dtype)
            return carry

        lax.fori_loop(0, ns // T, body, 0)

    nl = -lam_ref[...]
    sp = jnp.maximum(nl, 0.0) + jnp.log1p(jnp.exp(-jnp.abs(nl)))
    decay = sp * (-RG_C * LOG2E)
    NG = W // LANES

    def rglru(x_ref, ns, h0):
        stage(x_ref[0].astype(F32), ns)

        def gates_body(ti, carry):
            t0 = pl.multiple_of(ti * T, T)
            xs = xp[pl.ds(t0, T + 2 * PAD), :]
            for d in range(2):
                off = PAD - 3 if d == 0 else PAD
                xc = cbias_ref[d]
                for k in range(4):
                    xc = xc + cw_ref[d, k:k + 1, :] * xs[off + k:off + k + T]
                g = jnp.dot(xc.astype(BF16), wg_ref[d, 0], preferred_element_type=F32)
                r = _sigmoid(g[:, :W] + ba_ref[d])
                i = _sigmoid(g[:, W:] + bx_ref[d])
                a = jnp.exp2(r * decay[d])
                y = 1.0 - a * a
                bb = jnp.where(y > 0.0, y * lax.rsqrt(y), 0.0) * (i * xc)
                for g in range(NG):
                    a_scr[d, g, pl.ds(t0, T), :] = a[:, g * LANES:(g + 1) * LANES]
                    b_scr[d, g, pl.ds(t0, T), :] = bb[:, g * LANES:(g + 1) * LANES]
            return carry

        lax.fori_loop(0, ns // T, gates_body, 0, unroll=2)
        R = SCAN_ROWS
        nch = R // 8
        steps = ns // R
        sub = lax.broadcasted_iota(jnp.int32, (nch, 8, LANES), 1)

        def scan_body(ci, carry):
            out = []
            for d in range(2):
                r0 = pl.multiple_of((ci if d == 0 else steps - 1 - ci) * R, R)
                edge = 7 if d == 0 else 0
                for g in range(NG):
                    a = a_scr[d, g, pl.ds(r0, R), :].reshape(nch, 8, LANES)
                    b = b_scr[d, g, pl.ds(r0, R), :].reshape(nch, 8, LANES)
                    for s in (1, 2, 4):
                        sh = s if d == 0 else 8 - s
                        m = (sub >= s) if d == 0 else (sub < 8 - s)
                        b = b + a * jnp.where(m, pltpu.roll(b, sh, 1), 0.0)
                        a = a * jnp.where(m, pltpu.roll(a, sh, 1), 1.0)
                    c = carry[d * NG + g]
                    hs = [None] * nch
                    for j in (range(nch) if d == 0 else range(nch - 1, -1, -1)):
                        hs[j] = b[j] + a[j] * c
                        c = (jnp.broadcast_to(a[j, edge:edge + 1, :], (8, LANES)) * c
                             + jnp.broadcast_to(b[j, edge:edge + 1, :], (8, LANES)))
                    b_scr[d, g, pl.ds(r0, R), :] = jnp.concatenate(hs, axis=0)
                    out.append(c)
            return tuple(out)

        init = tuple(jnp.broadcast_to(h0[d][g], (8, LANES)) for d in range(2) for g in range(NG))
        ends = lax.fori_loop(0, steps, scan_body, init)
        return [[ends[d * NG + g][0:1, :] for g in range(NG)] for d in range(2)]

    def rg_out(g_ref, out_ref, ns):
        def body(ti, carry):
            t0 = pl.multiple_of(ti * T, T)
            for g in range(NG):
                hs = b_scr[0, g, pl.ds(t0, T), :] + b_scr[1, g, pl.ds(t0, T), :]
                gate = g_ref[0, pl.ds(t0, T), g * LANES:(g + 1) * LANES].astype(F32)
                out_ref[0, pl.ds(t0, T), g * LANES:(g + 1) * LANES] = (_gelu_tanh(gate) * hs).astype(out_ref.dtype)
            return carry

        lax.fori_loop(0, ns // T, body, 0)

    zero = [[jnp.zeros((1, LANES), F32)] * NG] * 2
    h_ctx = rglru(crx_ref, nc, zero)
    if need_ctx_out:
        rg_out(crg_ref, ycr_ref, nc)
        conv_mixer(cab_ref, cac_ref, cax_ref, yca_ref, nc)
    rglru(rx_ref, n, h_ctx)
    rg_out(rg_ref, yr_ref, n)
    conv_mixer(ab_ref, ac_ref, ax_ref, ya_ref, n)


def _seq_call(z, zc, zc_off, p, need_ctx_out):
    bsz, n, _ = z.shape
    nc = zc.shape[1]
    u = HALF // LANES

    def zspec(rows, off):
        return pl.BlockSpec((1, rows, HALF), lambda b, c: (b, 0, off // u + c))

    names = ("a_b", "a_c", "a_x", "r_g", "r_x")
    in_specs = [zspec(n, Z_FULL[k]) for k in names]
    args = [z] * 5
    if need_ctx_out:
        in_specs += [zspec(nc, zc_off[k]) for k in names[:4]]
        args += [zc] * 4
    in_specs.append(zspec(nc, zc_off["r_x"]))
    args.append(zc)
    in_specs += [
        pl.BlockSpec((3, HALF), lambda b, c: (0, c)),
        pl.BlockSpec((2, 4, HALF), lambda b, c: (0, 0, c)),
        pl.BlockSpec((2, 1, HALF), lambda b, c: (0, 0, c)),
        pl.BlockSpec((2, 1, HALF, 2 * HALF), lambda b, c: (0, c, 0, 0)),
        pl.BlockSpec((2, 1, HALF), lambda b, c: (0, 0, c)),
        pl.BlockSpec((2, 1, HALF), lambda b, c: (0, 0, c)),
        pl.BlockSpec((2, 1, HALF), lambda b, c: (0, 0, c)),
    ]
    args += [p["conv_a_w"], p["rg_conv_w"], p["rg_conv_b"], p["rg_wg"], p["rg_ba"], p["rg_bx"], p["rg_lambda"]]
    out_specs = [pl.BlockSpec((1, n, HALF), lambda b, c: (b, 0, c))] * 2
    out_shape = [jax.ShapeDtypeStruct((bsz, n, CONV_W), BF16), jax.ShapeDtypeStruct((bsz, n, RG_W), BF16)]
    if need_ctx_out:
        out_specs += [pl.BlockSpec((1, nc, HALF), lambda b, c: (b, 0, c))] * 2
        out_shape += [jax.ShapeDtypeStruct((bsz, nc, CONV_W), BF16), jax.ShapeDtypeStruct((bsz, nc, RG_W), BF16)]
    npad = max(n, nc) + 2 * PAD
    return pl.pallas_call(
        functools.partial(_seq_kernel, n=n, nc=nc, need_ctx_out=need_ctx_out),
        grid=(bsz, CONV_W // HALF),
        in_specs=in_specs,
        out_specs=out_specs,
        out_shape=out_shape,
        scratch_shapes=[pltpu.VMEM((npad, HALF), F32), pltpu.VMEM((2, u, max(n, nc), LANES), F32),
                        pltpu.VMEM((2, u, max(n, nc), LANES), F32)],
        compiler_params=_cparams(("parallel", "parallel")),
        name="seq_mixers",
    )(*args)


def _attn_kernel(*refs, has_lat, lam_init):
    if has_lat:
        q_ref, k_ref, v_ref, ck_ref, cv_ref, lam_ref, sw_ref, o_ref, k_all, v_ext = refs
    else:
        q_ref, ck_ref, cv_ref, lam_ref, sw_ref, o_ref, k_all, v_ext = refs
    nc = ck_ref.shape[1]
    nk = k_all.shape[0]

    @pl.when(pl.program_id(1) == 0)
    def _():
        k_all[0:nc, :] = ck_ref[0]
        if has_lat:
            k_all[nc:nk, :] = k_ref[0]
        for h in range(DA_HEADS):
            sl = slice(h * LANES, (h + 1) * LANES)
            v_ext[h, 0:nc, 0:LANES] = cv_ref[0, :, sl]
            if has_lat:
                v_ext[h, nc:nk, 0:LANES] = v_ref[0, :, sl]
            v_ext[h, :, LANES:2 * LANES] = jnp.ones((nk, LANES), v_ext.dtype)

    lane = lax.broadcasted_iota(jnp.int32, (1, LANES), 1)
    map1 = (lane % DA_HEAD_DIM) < (DA_HEAD_DIM // 2)
    dn = (((1,), (1,)), ((), ()))
    lam = lam_ref[0:1, 0:1]

    def scores(h):
        sl = slice(h * LANES, (h + 1) * LANES)
        q = q_ref[0, :, sl]
        zero = jnp.zeros_like(q)
        return [lax.dot_general(qm, k_all[:, sl], dn, preferred_element_type=F32)
                for qm in (jnp.where(map1, q, zero), jnp.where(map1, zero, q))]

    def finish(h, ss):
        es = [jnp.exp2(s - jnp.max(s, axis=-1, keepdims=True)) for s in ss]
        w1 = 1.0 / jnp.sum(es[0], axis=-1, keepdims=True)
        w2 = lam / jnp.sum(es[1], axis=-1, keepdims=True)
        p = es[0].astype(BF16) * w1.astype(BF16) - es[1].astype(BF16) * w2.astype(BF16)
        o = jnp.dot(p, v_ext[h, :, 0:LANES], preferred_element_type=F32)
        y = _rms(o) * sw_ref[...] * (1.0 - lam_init)
        o_ref[0, :, h * LANES:(h + 1) * LANES] = y.astype(o_ref.dtype)

    pending = scores(0)
    for h in range(DA_HEADS):
        ss = pending
        if h + 1 < DA_HEADS:
            pending = scores(h + 1)
        finish(h, ss)


def _attn_call(zq, z, zc, zc_off, lam, subw, lam_init, has_lat):
    bsz, nq, _ = zq.shape
    nc = zc.shape[1]
    tq = min(ATTN_Q_ROWS, nq)
    w = DA_HEADS * LANES
    u = DA_HEADS
    in_specs = [pl.BlockSpec((1, tq, w), lambda b, i: (b, i, Z_FULL["q"] // u))]
    args = [zq]
    nk = nc
    if has_lat:
        n = z.shape[1]
        nk = nc + n
        in_specs += [pl.BlockSpec((1, n, w), lambda b, i: (b, 0, Z_FULL["k"] // u)),
                     pl.BlockSpec((1, n, w), lambda b, i: (b, 0, Z_FULL["v"] // u))]
        args += [z, z]
    in_specs += [pl.BlockSpec((1, nc, w), lambda b, i: (b, 0, zc_off["k"] // u)),
                 pl.BlockSpec((1, nc, w), lambda b, i: (b, 0, zc_off["v"] // u)),
                 pl.BlockSpec((8, LANES), lambda b, i: (0, 0)),
                 pl.BlockSpec((1, LANES), lambda b, i: (0, 0))]
    args += [zc, zc, lam, subw]
    return pl.pallas_call(
        functools.partial(_attn_kernel, has_lat=has_lat, lam_init=lam_init),
        grid=(bsz, nq // tq),
        in_specs=in_specs,
        out_specs=pl.BlockSpec((1, tq, w), lambda b, i: (b, i, 0)),
        out_shape=jax.ShapeDtypeStruct((bsz, nq, DA_HEADS * DA_V_DIM), BF16),
        scratch_shapes=[pltpu.VMEM((nk, w), BF16), pltpu.VMEM((DA_HEADS, nk, 2 * LANES), BF16)],
        compiler_params=_cparams(("parallel", "arbitrary")),
        name="diff_attn",
    )(*args)


def _merge_kernel(*refs, route):
    if route:
        (ya_ref, yr_ref, yd_ref, ga_ref, gr_ref, gd_ref, x_ref, g1_ref, nw_ref, sh_ref, sc_ref,
         wb_ref, wo_ref, rw_ref, xo_ref, h_ref, rt_ref) = refs
    else:
        (ya_ref, yr_ref, yd_ref, ga_ref, gr_ref, gd_ref, x_ref, g1_ref, nw_ref, sh_ref, sc_ref,
         wb_ref, wo_ref, xo_ref, h_ref) = refs
    m = None
    for i, (y_ref, g_ref) in enumerate(((ya_ref, ga_ref), (yr_ref, gr_ref), (yd_ref, gd_ref))):
        t = _sigmoid(g_ref[0].astype(F32)) * jnp.dot(y_ref[0], wb_ref[i], preferred_element_type=F32)
        m = t if m is None else m + t
    out = jnp.dot(m.astype(BF16), wo_ref[...], preferred_element_type=F32)
    xn = x_ref[0] + g1_ref[0] * out
    xo_ref[0] = xn
    h = (_rms(xn) * nw_ref[...]) * (1.0 + sc_ref[0]) + sh_ref[0]
    hb = h.astype(BF16)
    h_ref[0] = hb
    if route:
        logits = jnp.dot(hb, rw_ref[...], preferred_element_type=F32)
        lane = lax.broadcasted_iota(jnp.int32, logits.shape, 1)
        neg = jnp.float32(-jnp.inf)
        l1 = jnp.where(lane < N_EXPERTS, logits, neg)
        m1 = jnp.max(l1, axis=-1, keepdims=True)
        i1 = jnp.min(jnp.where(l1 == m1, lane, LANES), axis=-1, keepdims=True)
        l2 = jnp.where(lane == i1, neg, l1)
        m2 = jnp.max(l2, axis=-1, keepdims=True)
        i2 = jnp.min(jnp.where(l2 == m2, lane, LANES), axis=-1, keepdims=True)
        ex = jnp.exp(m2 - m1)
        gt1 = 1.0 / (1.0 + ex)
        gt2 = ex * gt1
        rt = jnp.where(lane == 0, i1.astype(F32),
                       jnp.where(lane == 1, i2.astype(F32),
                                 jnp.where(lane == 2, gt1, jnp.where(lane == 3, gt2, 0.0))))
        rt_ref[0] = rt


def _merge_call(ya, yr, yd, z, x, mod3, mod_row, nw2, wb, wo, router_w=None):
    bsz, n, d = x.shape
    tm = min(512, n)
    route = router_w is not None
    tok = lambda b, i: (b, i, 0)
    gu = d // LANES
    in_specs = [
        pl.BlockSpec((1, tm, CONV_W), tok), pl.BlockSpec((1, tm, RG_W), tok),
        pl.BlockSpec((1, tm, DA_HEADS * DA_V_DIM), tok),
        pl.BlockSpec((1, tm, d), lambda b, i: (b, i, Z_FULL["g_a"] // gu)),
        pl.BlockSpec((1, tm, d), lambda b, i: (b, i, Z_FULL["g_r"] // gu)),
        pl.BlockSpec((1, tm, d), lambda b, i: (b, i, Z_FULL["g_d"] // gu)),
        pl.BlockSpec((1, tm, d), tok),
        pl.BlockSpec((1, 1, d), lambda b, i: (mod_row(b), 0, 2)),
        pl.BlockSpec((1, d), lambda b, i: (0, 0)),
        pl.BlockSpec((1, 1, d), lambda b, i: (mod_row(b), 0, 3)),
        pl.BlockSpec((1, 1, d), lambda b, i: (mod_row(b), 0, 4)),
        pl.BlockSpec(wb.shape, lambda b, i: (0, 0, 0)),
        pl.BlockSpec(wo.shape, lambda b, i: (0, 0)),
    ]
    args = [ya, yr, yd, z, z, z, x, mod3, nw2, mod3, mod3, wb, wo]
    out_specs = [pl.BlockSpec((1, tm, d), tok), pl.BlockSpec((1, tm, d), tok)]
    out_shape = [jax.ShapeDtypeStruct((bsz, n, d), F32),
                 jax.ShapeDtypeStruct((bsz, n, d), BF16)]
    if route:
        in_specs.append(pl.BlockSpec(router_w.shape, lambda b, i: (0, 0)))
        args.append(router_w)
        out_specs.append(pl.BlockSpec((1, tm, LANES), tok))
        out_shape.append(jax.ShapeDtypeStruct((bsz, n, LANES), F32))
    return pl.pallas_call(
        functools.partial(_merge_kernel, route=route),
        grid=(bsz, n // tm),
        in_specs=in_specs,
        out_specs=out_specs,
        out_shape=out_shape,
        compiler_params=_cparams(("parallel", "parallel")),
        name="merge",
    )(*args)


def _ffn_kernel(h_ref, x_ref, g2_ref, wg_ref, wu_ref, w2_ref, o_ref, acc_ref, *, nf):
    f = pl.program_id(2)
    h = h_ref[0]
    g = jnp.dot(h, wg_ref[...], preferred_element_type=F32)
    u = jnp.dot(h, wu_ref[...], preferred_element_type=F32)
    act = (_silu(g) * u).astype(BF16)

    def down():
        return jnp.dot(act, w2_ref[...], preferred_element_type=F32)

    if nf == 1:
        o_ref[0] = x_ref[0] + g2_ref[0] * down()
        return

    @pl.when(f == 0)
    def _():
        acc_ref[...] = down()

    if nf > 2:
        @pl.when((f > 0) & (f < nf - 1))
        def _():
            acc_ref[...] += down()

    @pl.when(f == nf - 1)
    def _():
        o_ref[0] = x_ref[0] + g2_ref[0] * (acc_ref[...] + down())


def _ffn_call(h, x, mod3, mod_row, w13, w2):
    bsz, n, d = x.shape
    ff = w2.shape[0]
    tm = min(SWIGLU_ROWS, n)
    tf = SWIGLU_FF_TILE
    nf = ff // tf
    tok = lambda b, i, f: (b, i, 0)
    return pl.pallas_call(
        functools.partial(_ffn_kernel, nf=nf),
        grid=(bsz, n // tm, nf),
        in_specs=[
            pl.BlockSpec((1, tm, d), tok),
            pl.BlockSpec((1, tm, d), tok),
            pl.BlockSpec((1, 1, d), lambda b, i, f: (mod_row(b), 0, 5)),
            pl.BlockSpec((d, tf), lambda b, i, f: (0, f)),
            pl.BlockSpec((d, tf), lambda b, i, f: (0, nf + f)),
            pl.BlockSpec((tf, d), lambda b, i, f: (f, 0)),
        ],
        out_specs=pl.BlockSpec((1, tm, d), tok),
        out_shape=jax.ShapeDtypeStruct((bsz, n, d), F32),
        scratch_shapes=[pltpu.VMEM((tm, d), F32)],
        compiler_params=_cparams(("parallel", "parallel", "arbitrary")),
        name="ffn",
    )(h, x, mod3, w13, w13, w2)


def _moe_kernel(be_ref, nu_ref, x_ref, wg_ref, wu_ref, w2_ref, o_ref, xb):
    i = pl.program_id(0)
    f = pl.program_id(1)
    used = i < nu_ref[0]

    def swiglu(x):
        g = jnp.dot(x, wg_ref[0], preferred_element_type=F32)
        u = jnp.dot(x, wu_ref[0], preferred_element_type=F32)
        act = (_silu(g) * u).astype(BF16)
        return jnp.dot(act, w2_ref[0], preferred_element_type=F32)

    @pl.when(used & (f == 0))
    def _():
        x = x_ref[...].astype(BF16)
        xb[...] = x
        o_ref[...] = swiglu(x)

    @pl.when(used & (f > 0))
    def _():
        o_ref[...] += swiglu(xb[...])

    @pl.when(jnp.logical_not(used) & (f == 0))
    def _():
        o_ref[...] = jnp.zeros(o_ref.shape, o_ref.dtype)


def _moe_call(xs, block_e, n_used, w13, w2, rows):
    n_rows, d = xs.shape
    n_blocks = n_rows // rows
    ff = w2.shape[1]
    tf = SWIGLU_FF_TILE
    nf = ff // tf

    def bi(i, nu):
        return jnp.minimum(i, nu[0] - 1)

    def fi(i, f, nu):
        return jnp.where(i < nu[0], f, nf - 1)

    grid_spec = pltpu.PrefetchScalarGridSpec(
        num_scalar_prefetch=2,
        grid=(n_blocks, nf),
        in_specs=[
            pl.BlockSpec((rows, d), lambda i, f, be, nu: (bi(i, nu), 0)),
            pl.BlockSpec((1, d, tf), lambda i, f, be, nu: (be[i], 0, fi(i, f, nu))),
            pl.BlockSpec((1, d, tf), lambda i, f, be, nu: (be[i], 0, nf + fi(i, f, nu))),
            pl.BlockSpec((1, tf, d), lambda i, f, be, nu: (be[i], fi(i, f, nu), 0)),
        ],
        out_specs=pl.BlockSpec((rows, d), lambda i, f, be, nu: (i, 0)),
        scratch_shapes=[pltpu.VMEM((rows, d), BF16)],
    )
    return pl.pallas_call(
        _moe_kernel,
        grid_spec=grid_spec,
        out_shape=jax.ShapeDtypeStruct((n_rows, d), F32),
        compiler_params=_cparams(("arbitrary", "arbitrary")),
        name="moe",
    )(block_e, n_used, xs, w13, w13, w2)


GROUP_PAD = 8
GROUP_SIZES = tuple(SWIGLU_ROWS >> s for s in range(SWIGLU_ROWS.bit_length()) if SWIGLU_ROWS >> s >= GROUP_PAD)


def _sorted_rows(tm):
    return 2 * tm + N_EXPERTS * GROUP_PAD


def _group_copies(step, src_of, dst_of, loc_ref, base_ref, pc_ref, sem, sizes):
    for e in range(N_EXPERTS):
        loc = loc_ref[step * N_EXPERTS + e]
        base = base_ref[step * N_EXPERTS + e]
        pc = pc_ref[step * N_EXPERTS + e]
        for size in sizes:
            before = pc & (-2 * size)
            src = src_of(pl.multiple_of(loc + before, GROUP_PAD), pl.multiple_of(base + before, GROUP_PAD), size)
            dst = dst_of(pl.multiple_of(loc + before, GROUP_PAD), pl.multiple_of(base + before, GROUP_PAD), size)
            yield (pc & size) != 0, pltpu.make_async_copy(src, dst, sem)


def _dispatch2_kernel(loc_ref, base_ref, pc_ref, tail_ref, nu_ref, pos_ref, h_ref, xs_hbm, srt, zeros, sem, zsem,
                      *, tm, nt, rows, sizes):
    i = pl.program_id(0)
    n = pl.num_programs(0)
    slot = i % 2
    sr = srt.shape[1]
    n_blocks = xs_hbm.shape[0] // rows
    min_used = 2 * tm * nt // rows

    def tail_copies():
        for e in range(N_EXPERTS):
            start = tail_ref[e]
            length = tail_ref[N_EXPERTS + e]
            for size in sizes[1:]:
                before = length & (-2 * size)
                dst = xs_hbm.at[pl.ds(pl.multiple_of(start + before, GROUP_PAD), size)]
                yield (length & size) != 0, pltpu.make_async_copy(zeros.at[pl.ds(0, size)], dst, zsem)
        for b in range(min_used, n_blocks):
            dst = xs_hbm.at[pl.ds(b * rows, rows)]
            yield b >= nu_ref[0], pltpu.make_async_copy(zeros, dst, zsem)

    @pl.when(i == 0)
    def _():
        zeros[...] = jnp.zeros(zeros.shape, zeros.dtype)
        for pred, cp in tail_copies():
            pl.when(pred)(cp.start)

    r = lax.broadcasted_iota(jnp.int32, (sr, tm), 0)
    hit = (r == pos_ref[0, 0:1, :]) | (r == pos_ref[0, 1:2, :])
    sel = jnp.where(hit, 1.0, 0.0).astype(BF16)
    srt[slot] = jnp.dot(sel, h_ref[...], preferred_element_type=F32)

    def group_copies(step, s):
        return _group_copies(step, lambda lo, ba, sz: srt.at[s, pl.ds(lo, sz)],
                             lambda lo, ba, sz: xs_hbm.at[pl.ds(ba, sz)], loc_ref, base_ref, pc_ref, sem.at[s], sizes)

    for pred, cp in group_copies(i, slot):
        pl.when(pred)(cp.start)

    @pl.when(i > 0)
    def _():
        for pred, cp in group_copies(i - 1, 1 - slot):
            pl.when(pred)(cp.wait)

    @pl.when(i == n - 1)
    def _():
        for pred, cp in group_copies(i, slot):
            pl.when(pred)(cp.wait)
        for pred, cp in tail_copies():
            pl.when(pred)(cp.wait)


def _dispatch2_call(h2, plan, tm, rows):
    t, d = h2.shape
    nt = t // tm
    sr = _sorted_rows(tm)
    sizes = tuple(s for s in GROUP_SIZES if s <= tm)
    grid_spec = pltpu.PrefetchScalarGridSpec(
        num_scalar_prefetch=5,
        grid=(nt,),
        in_specs=[pl.BlockSpec((1, 2, tm), lambda i, *_: (i, 0, 0)),
                  pl.BlockSpec((tm, d), lambda i, *_: (i, 0))],
        out_specs=pl.BlockSpec(memory_space=pl.ANY),
        scratch_shapes=[pltpu.VMEM((2, sr, d), F32), pltpu.VMEM((rows, d), F32),
                        pltpu.SemaphoreType.DMA((2,)), pltpu.SemaphoreType.DMA(())],
    )
    return pl.pallas_call(
        functools.partial(_dispatch2_kernel, tm=tm, nt=nt, rows=rows, sizes=sizes),
        grid_spec=grid_spec,
        out_shape=jax.ShapeDtypeStruct((plan["n_blocks"] * rows, d), F32),
        compiler_params=pltpu.CompilerParams(dimension_semantics=("arbitrary",), vmem_limit_bytes=VMEM_LIMIT),
        name="moe_dispatch",
    )(plan["loc"], plan["base"], plan["pc"], plan["tail"], plan["n_used"], plan["pos_rows"], h2)


def _combine2_kernel(loc_ref, base_ref, pc_ref, x_ref, pos_ref, rt_ref, g2_ref, fw_ref, ys_hbm, o_ref, buf, sem,
                     *, tm, sizes):
    i = pl.program_id(0)
    n = pl.num_programs(0)
    slot = i % 2
    sr = buf.shape[1]

    def group_copies(step, s):
        return _group_copies(step, lambda lo, ba, sz: ys_hbm.at[pl.ds(ba, sz)],
                             lambda lo, ba, sz: buf.at[s, pl.ds(lo, sz)], loc_ref, base_ref, pc_ref, sem.at[s], sizes)

    @pl.when(i == 0)
    def _():
        buf[...] = jnp.zeros(buf.shape, buf.dtype)
        for pred, cp in group_copies(0, 0):
            pl.when(pred)(cp.start)

    @pl.when(i + 1 < n)
    def _():
        for pred, cp in group_copies(i + 1, 1 - slot):
            pl.when(pred)(cp.start)

    for pred, cp in group_copies(i, slot):
        pl.when(pred)(cp.wait)

    rt = rt_ref[...]
    col = lax.broadcasted_iota(jnp.int32, (tm, sr), 1)
    pos = pos_ref[...]
    w = jnp.where(col == pos[:, 0:1], rt[:, 2:3], jnp.where(col == pos[:, 1:2], rt[:, 3:4], 0.0))
    y = jnp.dot(w.astype(BF16), buf[slot].astype(BF16), preferred_element_type=F32)
    xn = x_ref[...] + g2_ref[0] * y
    o_ref[...] = _rms(xn) * fw_ref[...]


def _combine2_call(x2, ys, plan, route2, mod3, n_per_seq, fw, tm):
    t, d = x2.shape
    nt = t // tm
    per = n_per_seq // tm
    sr = _sorted_rows(tm)
    sizes = tuple(s for s in GROUP_SIZES if s <= tm)
    grid_spec = pltpu.PrefetchScalarGridSpec(
        num_scalar_prefetch=3,
        grid=(nt,),
        in_specs=[
            pl.BlockSpec((tm, d), lambda i, *_: (i, 0)),
            pl.BlockSpec((tm, 2), lambda i, *_: (i, 0)),
            pl.BlockSpec((tm, LANES), lambda i, *_: (i, 0)),
            pl.BlockSpec((1, 1, d), lambda i, *_: (i // per, 0, 5)),
            pl.BlockSpec((1, d), lambda i, *_: (0, 0)),
            pl.BlockSpec(memory_space=pl.ANY),
        ],
        out_specs=pl.BlockSpec((tm, d), lambda i, *_: (i, 0)),
        scratch_shapes=[pltpu.VMEM((2, sr, d), F32), pltpu.SemaphoreType.DMA((2,))],
    )
    return pl.pallas_call(
        functools.partial(_combine2_kernel, tm=tm, sizes=sizes),
        grid_spec=grid_spec,
        out_shape=jax.ShapeDtypeStruct((t, d), F32),
        compiler_params=pltpu.CompilerParams(dimension_semantics=("arbitrary",), vmem_limit_bytes=VMEM_LIMIT),
        name="combine_final",
    )(plan["loc"], plan["base"], plan["pc"], x2, plan["pos_cols"], route2, mod3, fw, ys)


def _sort_plan(route, tm, rows):
    t = route.shape[0]
    nt = t // tm
    experts = jnp.arange(N_EXPERTS, dtype=jnp.int32)
    e = route[:, :2].astype(jnp.int32).reshape(nt, 2 * tm)
    hot = (e[:, :, None] == experts[None, None, :]).astype(jnp.int32)
    cs = jnp.cumsum(hot, axis=1)
    rank = jnp.sum(cs * hot, axis=2) - 1
    cnt = cs[:, -1, :]
    pc = (cnt + GROUP_PAD - 1) // GROUP_PAD * GROUP_PAD
    loc = jnp.cumsum(pc, axis=1) - pc
    pos = jnp.sum(hot * loc[:, None, :], axis=2) + rank
    tot = jnp.sum(pc, axis=0)
    padded = (tot + rows - 1) // rows * rows
    e_end = jnp.cumsum(padded)
    e_start = e_end - padded
    base = e_start[None, :] + jnp.cumsum(pc, axis=0) - pc
    n_blocks = -(-(2 * t + nt * N_EXPERTS * (GROUP_PAD - 1)) // rows) + N_EXPERTS
    block_start = jnp.arange(n_blocks, dtype=jnp.int32) * rows
    block_e = jnp.minimum(jnp.sum((block_start[:, None] >= e_end[None, :]).astype(jnp.int32), axis=1),
                          N_EXPERTS - 1)
    pos3 = pos.reshape(nt, tm, 2)
    i32 = lambda a: a.astype(jnp.int32)
    return dict(
        loc=i32(loc.reshape(-1)), base=i32(base.reshape(-1)), pc=i32(pc.reshape(-1)),
        tail=i32(jnp.concatenate([e_start + tot, padded - tot])),
        n_used=i32(e_end[-1] // rows).reshape(1), block_e=i32(block_e), n_blocks=n_blocks,
        pos_rows=i32(jnp.swapaxes(pos3, 1, 2)), pos_cols=i32(pos3.reshape(t, 2)))


def _norm_kernel(x_ref, fw_ref, o_ref):
    o_ref[0] = _rms(x_ref[0]) * fw_ref[...]


def _norm_call(x, fw):
    bsz, n, d = x.shape
    tm = min(512, n)
    tok = lambda b, i: (b, i, 0)
    return pl.pallas_call(
        _norm_kernel,
        grid=(bsz, n // tm),
        in_specs=[pl.BlockSpec((1, tm, d), tok), pl.BlockSpec((1, d), lambda b, i: (0, 0))],
        out_specs=pl.BlockSpec((1, tm, d), tok),
        out_shape=jax.ShapeDtypeStruct((bsz, n, d), F32),
        compiler_params=_cparams(("parallel", "parallel")),
        name="final_norm",
    )(x, fw)


def _rope_perm():
    perm = np.zeros((LANES,), np.int32)
    half = ROPE_AXIS_DIM // 2
    for i in range(LANES // 2):
        comp, axis, j = i // 32, (i % 32) // half, i % half
        perm[i] = comp * DA_HEAD_DIM + axis * ROPE_AXIS_DIM + j
        perm[LANES // 2 + i] = perm[i] + half
    return perm


def _rope_tables(n):
    rows = n // GRID_W
    pos_r = jnp.repeat(jnp.arange(rows, dtype=F32), GRID_W)
    pos_c = jnp.broadcast_to(jnp.arange(GRID_W, dtype=F32), (rows, GRID_W)).reshape(-1)
    inv_freq = ROPE_BASE ** (-jnp.arange(0, ROPE_AXIS_DIM, 2, dtype=F32) / ROPE_AXIS_DIM)
    ang = jnp.stack([pos_r[:, None] * inv_freq, pos_c[:, None] * inv_freq], axis=1)
    cos, sin = jnp.cos(ang), jnp.sin(ang)
    half = jnp.concatenate([cos.reshape(n, 32), cos.reshape(n, 32)], axis=1)
    cos_t = jnp.concatenate([half, half], axis=1)
    sh = jnp.concatenate([sin.reshape(n, 32), sin.reshape(n, 32)], axis=1)
    sin_t = jnp.concatenate([-sh, sh], axis=1)
    return cos_t, sin_t


def _blockdiag_gate(wa, wx):
    per = HALF // (RG_W // RG_BLOCKS)
    eye = jnp.eye(per, dtype=wa.dtype)

    def bd(w):
        w4 = w.reshape(2, RG_BLOCKS // per, per, w.shape[-2], w.shape[-1])
        return jnp.einsum("dhbij,bc->dhbicj", w4, eye).reshape(2, RG_BLOCKS // per, HALF, HALF)

    return jnp.concatenate([bd(wa), bd(wx)], axis=-1).astype(BF16)


def kernel(x, c, ctx, c_ctx, mod_w, mod_b, norm1_w, norm2_w, w_in, conv_a_w, rg_conv_w, rg_conv_b, rg_wa, rg_ba, rg_wx, rg_bx, rg_lambda, da_lambda, da_subln_w, w_branch, w_out, ffn_w13, ffn_w2, router_w, moe_w13, moe_w2, final_norm_w):
    bsz, n, d = x.shape
    nc = ctx.shape[1]
    depth = mod_w.shape[0]
    cos_t, sin_t = _rope_tables(n)
    perm = _rope_perm()
    ref_cols = {name: np.arange(start, start + width, dtype=np.int32) for name, start, width in REF_COLS}
    for name in ("q", "k"):
        for h in range(DA_HEADS):
            base = ref_cols[name][0] + h * LANES
            ref_cols[name][h * LANES:(h + 1) * LANES] = base + perm
    z_cols = np.concatenate([ref_cols[name] for name in Z_ORDER])

    bp = -(-(bsz + 1) // 8) * 8
    cc = jnp.zeros((bp, d), F32).at[:bsz].set(c).at[bsz].set(c_ctx)
    lat_row = lambda b: b
    ctx_row = lambda b: bsz
    cx = ctx
    moe_rows = SWIGLU_ROWS if bsz * n >= 8192 else 128

    for l in range(depth):
        last = l == depth - 1
        lam_init = 0.8 - 0.6 * math.exp(-0.3 * l)
        j = l // 2
        mod, lam = _mod_call(cc, mod_w[l], mod_b[l][None, :], da_lambda[l], lam_init)
        mod3 = mod.reshape(bp, 1, N_MOD * d)
        w_in_l = w_in[l][:, z_cols].astype(BF16)
        nw1 = norm1_w[l][None, :]
        nw2 = norm2_w[l][None, :]
        p = dict(
            conv_a_w=conv_a_w[l], rg_conv_w=rg_conv_w[l], rg_conv_b=rg_conv_b[l][:, None, :],
            rg_wg=_blockdiag_gate(rg_wa[l], rg_wx[l]),
            rg_ba=rg_ba[l].reshape(2, 1, RG_W), rg_bx=rg_bx[l].reshape(2, 1, RG_W),
            rg_lambda=rg_lambda[l][:, None, :])
        subw = da_subln_w[l][None, :]
        wb = w_branch[l].astype(BF16)
        wo = w_out[l].astype(BF16)

        ncol = w_in.shape[2] // COL
        z = _in_call(x, nw1, mod3, lat_row, w_in_l, range(ncol), True, cos_t, sin_t)
        flat = lambda a: a.reshape((1, bsz * nc) + a.shape[2:])
        unflat = lambda a: a.reshape((bsz, nc) + a.shape[2:])
        if last:
            zc = unflat(_in_call(flat(cx), nw1, mod3, ctx_row, w_in_l, CTX_LAST_BLOCKS, False))
            zc_off = Z_CTX_LAST
        else:
            zc = unflat(_in_call(flat(cx), nw1, mod3, ctx_row, w_in_l, range(ncol), False))
            zc_off = Z_FULL

        seq_out = _seq_call(z, zc, zc_off, p, not last)
        yd = _attn_call(z, z, zc, zc_off, lam, subw, lam_init, True)
        moe_layer = l % 2 == 1
        if moe_layer:
            rw = jnp.zeros((d, LANES), BF16).at[:, :N_EXPERTS].set(router_w[j].astype(BF16))
            xn, h, route = _merge_call(seq_out[0], seq_out[1], yd, z, x, mod3, lat_row, nw2, wb, wo, rw)
            t = bsz * n
            tm = min(512, n)
            route2 = route.reshape(t, LANES)
            plan = _sort_plan(route2, tm, moe_rows)
            xs = _dispatch2_call(h.reshape(t, d), plan, tm, moe_rows)
            ys = _moe_call(xs, plan["block_e"], plan["n_used"], moe_w13[j].astype(BF16),
                           moe_w2[j].astype(BF16), moe_rows)
            if last:
                out = _combine2_call(xn.reshape(t, d), ys, plan, route2, mod3, n, final_norm_w[None, :], tm)
                return out.reshape(bsz, n, d)
            raise NotImplementedError("expert layer followed by another layer")
        xn, h = _merge_call(seq_out[0], seq_out[1], yd, z, x, mod3, lat_row, nw2, wb, wo)
        w13 = ffn_w13[j].astype(BF16)
        w2 = ffn_w2[j].astype(BF16)
        x = _ffn_call(h, xn, mod3, lat_row, w13, w2)
        if not last:
            ycd = _attn_call(zc, None, zc, zc_off, lam, subw, lam_init, False)
            cxn, hc = _merge_call(flat(seq_out[2]), flat(seq_out[3]), flat(ycd), flat(zc), flat(cx), mod3,
                                  ctx_row, nw2, wb, wo)
            cx = unflat(_ffn_call(hc, cxn, mod3, ctx_row, w13, w2))
    return _norm_call(x, final_norm_w[None, :])
```
